```python
import math
import jax, jax.numpy as jnp
from jax import lax
import numpy as np

D_MODEL = 1024
BATCH = 16
SEQ = 256
DEPTH = 4
DEC_BATCH = 4
DEC_SEQ = 1024
PAST_LEN = 512

GRID_W = 64
HEAD_DIM = 64
A_HEADS = 8
A_KV_HEADS = 2
A_REP = A_HEADS // A_KV_HEADS
A_WIDTH = A_HEADS * HEAD_DIM
A_KV_WIDTH = A_KV_HEADS * HEAD_DIM
Q_BLOCK = 128
ROPE_THETA = 10000.0
B_HEADS = 4
B_DK = 32
B_DV = 64
B_QK_WIDTH = B_HEADS * B_DK
B_WIDTH = B_HEADS * B_DV
GATE_RANK = 16
GATE_TAU = 16.0
C_HEADS = 4
C_DK = 64
C_DV = 64
C_QK_WIDTH = C_HEADS * C_DK
C_WIDTH = C_HEADS * C_DV
CONV_WIDTH = 5
CONV_CH = 2 * C_QK_WIDTH + C_WIDTH
CHUNK = 64
N_DIR = 2
MIX_WIDTH = A_WIDTH + B_WIDTH + C_WIDTH
D_FF = -(-(8 * D_MODEL) // (3 * 256)) * 256
PROJ_SIZES = (A_WIDTH, A_KV_WIDTH, A_KV_WIDTH,
              B_QK_WIDTH, B_QK_WIDTH, B_WIDTH, B_WIDTH, N_DIR * GATE_RANK,
              C_QK_WIDTH, C_QK_WIDTH, C_WIDTH, C_WIDTH, N_DIR * C_HEADS, N_DIR * C_HEADS)
PROJ_WIDTH = sum(PROJ_SIZES)
EPS = 1e-6

kernel_name = 'hybrid_dit_gqa_gla_deltanet_step'


def rms_norm(x, g):
    xf = x.astype(jnp.float32)
    y = xf * lax.rsqrt(jnp.mean(xf * xf, axis=-1, keepdims=True) + EPS)
    return (y * g.astype(jnp.float32)).astype(x.dtype)


def l2_normalize(x):
    return x * lax.rsqrt(jnp.sum(x * x, axis=-1, keepdims=True) + EPS)


def split_cols(z, sizes):
    return jnp.split(z, np.cumsum(sizes)[:-1].tolist(), axis=-1)


def to_heads(z, n):
    b, t, _ = z.shape
    return z.reshape(b, t, n, -1).transpose(0, 2, 1, 3).astype(jnp.float32)


def axial_rope(n_tokens):
    rows = n_tokens // GRID_W
    row = jnp.repeat(jnp.arange(rows, dtype=jnp.float32), GRID_W)
    col = jnp.tile(jnp.arange(GRID_W, dtype=jnp.float32), rows)
    n_freq = HEAD_DIM // 4
    inv_freq = ROPE_THETA ** (-jnp.arange(n_freq, dtype=jnp.float32) / n_freq)
    ang_r = row[:, None] * inv_freq
    ang_c = col[:, None] * inv_freq
    ang = jnp.concatenate([ang_r, ang_r, ang_c, ang_c], axis=-1)
    return jnp.cos(ang), jnp.sin(ang)


def apply_rope(x, cos, sin):
    a1, a2, b1, b2 = jnp.split(x, 4, axis=-1)
    rot = jnp.concatenate([-a2, a1, -b2, b1], axis=-1)
    y = x.astype(jnp.float32) * cos[:, None, :] + rot.astype(jnp.float32) * sin[:, None, :]
    return y.astype(x.dtype)


def attend_blocks(q, k, v):
    bsz, t = q.shape[0], q.shape[1]
    nb = t // Q_BLOCK
    qb = q.reshape(bsz, nb, Q_BLOCK, A_KV_HEADS, A_REP, HEAD_DIM).transpose(1, 0, 2, 3, 4, 5)
    kf = k.astype(jnp.float32)
    vf = v.astype(jnp.float32)
    scale = HEAD_DIM ** -0.5

    def one_block(q_blk):
        s = jnp.einsum('bqgrd,bsgd->bgrqs', q_blk.astype(jnp.float32), kf) * scale
        p = jax.nn.softmax(s, axis=-1)
        return jnp.einsum('bgrqs,bsgd->bqgrd', p, vf).astype(q.dtype)

    o = lax.map(one_block, qb)
    return o.transpose(1, 0, 2, 3, 4, 5).reshape(bsz, t, A_WIDTH)


def to_chunks(a):
    n = a.shape[2] // CHUNK
    return jnp.moveaxis(a.reshape(a.shape[:2] + (n, CHUNK) + a.shape[3:]), 2, 0)


def from_chunks(o):
    o = jnp.moveaxis(o, 0, 2)
    return o.reshape(o.shape[:2] + (-1,) + o.shape[4:])


def gla_scan(q, k, v, g, s0):
    causal = jnp.tril(jnp.ones((CHUNK, CHUNK), dtype=bool))[:, :, None]

    def body(s, inp):
        qc, kc, vc, gc = inp
        b = jnp.cumsum(gc, axis=2)
        diff = b[:, :, :, None, :] - b[:, :, None, :, :]
        decay = jnp.exp(jnp.where(causal, diff, -jnp.inf))
        att = jnp.einsum('bhtk,bhsk,bhtsk->bhts', qc, kc, decay)
        o = (jnp.einsum('bhtk,bhkv->bhtv', qc * jnp.exp(b), s)
             + jnp.einsum('bhts,bhsv->bhtv', att, vc))
        bl = b[:, :, -1:, :]
        s = (jnp.exp(bl)[:, :, 0, :, None] * s
             + jnp.einsum('bhsk,bhsv->bhkv', kc * jnp.exp(bl - b), vc))
        return s, o

    s, o = lax.scan(body, s0, (to_chunks(q), to_chunks(k), to_chunks(v), to_chunks(g)))
    return from_chunks(o), s


def delta_scan(q, k, v, beta, g, s0):
    incl = jnp.tril(jnp.ones((CHUNK, CHUNK), dtype=bool))
    strict = jnp.tril(jnp.ones((CHUNK, CHUNK), dtype=bool), -1)
    eye = jnp.eye(CHUNK, dtype=jnp.float32)
    dv = v.shape[-1]

    def body(s, inp):
        qc, kc, vc, bc, gc = inp
        gam = jnp.cumsum(gc, axis=-1)
        diff = gam[..., :, None] - gam[..., None, :]
        dec = jnp.exp(jnp.where(incl, diff, -jnp.inf))
        kb = kc * bc[..., None]
        m = jnp.where(strict, jnp.einsum('bhtk,bhsk->bhts', kb, kc) * dec, 0.0)
        rhs = jnp.concatenate([vc * bc[..., None], kb * jnp.exp(gam)[..., None]], axis=-1)
        sol = lax.linalg.triangular_solve(eye + m, rhs, left_side=True, lower=True,
                                          unit_diagonal=True)
        u, w = sol[..., :dv], sol[..., dv:]
        v_new = u - jnp.einsum('bhtk,bhkv->bhtv', w, s)
        att = jnp.einsum('bhtk,bhsk->bhts', qc, kc) * dec
        o = (jnp.einsum('bhtk,bhkv->bhtv', qc * jnp.exp(gam)[..., None], s)
             + jnp.einsum('bhts,bhsv->bhtv', att, v_new))
        gl = gam[..., -1:]
        s = (jnp.exp(gl)[..., None] * s
             + jnp.einsum('bhsk,bhsv->bhkv', kc * jnp.exp(gl - gam)[..., None], v_new))
        return s, o

    s, o = lax.scan(body, s0, (to_chunks(q), to_chunks(k), to_chunks(v),
                              to_chunks(beta), to_chunks(g)))
    return from_chunks(o), s


def run_direction(scan_fn, arrays, s0, reverse):
    if reverse:
        arrays = tuple(jnp.flip(a, axis=2) for a in arrays)
    o, s = scan_fn(*arrays, s0)
    if reverse:
        o = jnp.flip(o, axis=2)
    return o, s


def short_conv(z, w):
    y = lax.conv_general_dilated(
        z, w[:, None, :].astype(z.dtype), window_strides=(1,),
        padding=((CONV_WIDTH // 2, CONV_WIDTH // 2),),
        dimension_numbers=('NWC', 'WIO', 'NWC'), feature_group_count=z.shape[-1])
    return jax.nn.silu(y)


def mixer(h, w_in, qk_g, w_gg, b_gg, gla_g, conv_w, a_log, dt_bias, delta_g, w_out,
          ctx_kv, s0_gla, s0_delta, rope):
    f32 = jnp.float32
    bsz, t, _ = h.shape
    (qa, ka, va, qb, kb, vb, rb, gcode, qc, kc, vc, gc, bc, ac) = split_cols(h @ w_in, PROJ_SIZES)

    qa = rms_norm(qa.reshape(bsz, t, A_HEADS, HEAD_DIM), qk_g[0])
    ka = rms_norm(ka.reshape(bsz, t, A_KV_HEADS, HEAD_DIM), qk_g[1])
    va = va.reshape(bsz, t, A_KV_HEADS, HEAD_DIM)
    if ctx_kv is None:
        oa = attend_blocks(qa, ka, va)
    else:
        cos, sin = rope
        k_all = jnp.concatenate([ctx_kv[0].astype(ka.dtype), apply_rope(ka, cos, sin)], axis=1)
        v_all = jnp.concatenate([ctx_kv[1].astype(va.dtype), va], axis=1)
        oa = attend_blocks(apply_rope(qa, cos, sin), k_all, v_all)

    qb = to_heads(qb, B_HEADS) * B_DK ** -0.5
    kb = to_heads(kb, B_HEADS)
    vb = to_heads(vb, B_HEADS)
    glog = jax.nn.log_sigmoid(
        jnp.einsum('btdr,drk->btdk', gcode.reshape(bsz, t, N_DIR, GATE_RANK).astype(f32),
                   w_gg.astype(f32)) + b_gg.astype(f32)) / GATE_TAU
    ob = 0.0
    s_gla = []
    for d in range(N_DIR):
        o_d, s_d = run_direction(gla_scan, (qb, kb, vb, to_heads(glog[:, :, d], B_HEADS)),
                                 s0_gla[:, d], d == 1)
        ob = ob + o_d
        s_gla.append(s_d)
    ob = rms_norm(ob.transpose(0, 2, 1, 3), gla_g).reshape(bsz, t, B_WIDTH) * jax.nn.silu(rb.astype(f32))

    qkv = short_conv(jnp.concatenate([qc, kc, vc], axis=-1), conv_w)
    qc, kc, vc = split_cols(qkv, (C_QK_WIDTH, C_QK_WIDTH, C_WIDTH))
    qc = l2_normalize(to_heads(qc, C_HEADS)) * C_DK ** -0.5
    kc = l2_normalize(to_heads(kc, C_HEADS))
    vc = to_heads(vc, C_HEADS)
    beta = jax.nn.sigmoid(bc.reshape(bsz, t, N_DIR, C_HEADS).astype(f32))
    glog_c = -jnp.exp(a_log.astype(f32)) * jax.nn.softplus(
        ac.reshape(bsz, t, N_DIR, C_HEADS).astype(f32) + dt_bias.astype(f32))
    oc = 0.0
    s_delta = []
    for d in range(N_DIR):
        o_d, s_d = run_direction(delta_scan,
                                 (qc, kc, vc, beta[:, :, d].transpose(0, 2, 1),
                                  glog_c[:, :, d].transpose(0, 2, 1)),
                                 s0_delta[:, d], d == 1)
        oc = oc + o_d
        s_delta.append(s_d)
    oc = rms_norm(oc.transpose(0, 2, 1, 3), delta_g).reshape(bsz, t, C_WIDTH) * jax.nn.silu(gc.astype(f32))

    out = jnp.concatenate([oa, ob.astype(h.dtype), oc.astype(h.dtype)], axis=-1) @ w_out
    return out, ka, va, jnp.stack(s_gla, axis=1), jnp.stack(s_delta, axis=1)


def block(x, mod, lw, ctx_kv, s0_gla, s0_delta, rope):
    (norm_g, w_in, qk_g, w_gg, b_gg, gla_g, conv_w, a_log, dt_bias, delta_g, w_out,
     w_gate, w_up, w_down) = lw
    shift_m, scale_m, gate_m, shift_f, scale_f, gate_f = jnp.split(mod, 6, axis=-1)
    h = rms_norm(x, norm_g[0]) * (1 + scale_m) + shift_m
    mix, k_l, v_l, s_gla, s_delta = mixer(h, w_in, qk_g, w_gg, b_gg, gla_g, conv_w, a_log,
                                          dt_bias, delta_g, w_out, ctx_kv, s0_gla, s0_delta, rope)
    x = x + gate_m * rms_norm(mix, norm_g[1])
    h = rms_norm(x, norm_g[2]) * (1 + scale_f) + shift_f
    f = (jax.nn.silu(h @ w_gate) * (h @ w_up)) @ w_down
    x = x + gate_f * rms_norm(f, norm_g[3])
    return x, k_l, v_l, s_gla, s_delta


def setup_inputs(seed: int = 0) -> dict:
    key = jax.random.key(seed)
    ks = jax.random.split(key, 26)
    f32 = jnp.float32

    def nrm(k, shape, s):
        return jax.random.normal(k, shape, f32) * s

    dt = jnp.exp(jax.random.uniform(ks[18], (DEPTH, N_DIR, C_HEADS), f32,
                                    math.log(1e-3), math.log(1e-1)))
    return {
        'x_prompt': nrm(ks[0], (BATCH, SEQ, D_MODEL), 1.0),
        'x_sample': nrm(ks[1], (DEC_BATCH, DEC_SEQ, D_MODEL), 1.0),
        'cache_k': nrm(ks[3], (DEC_BATCH, DEPTH, PAST_LEN, A_KV_HEADS, HEAD_DIM), 1.0),
        'cache_v': nrm(ks[4], (DEC_BATCH, DEPTH, PAST_LEN, A_KV_HEADS, HEAD_DIM), 1.0),
        'state_gla': nrm(ks[5], (DEC_BATCH, DEPTH, N_DIR, B_HEADS, B_DK, B_DV), 0.3),
        'state_delta': nrm(ks[6], (DEC_BATCH, DEPTH, N_DIR, C_HEADS, C_DK, C_DV), 0.3),
        'c': nrm(ks[2], (DEC_BATCH, D_MODEL), 1.0),
        'c_ctx': nrm(ks[7], (D_MODEL,), 1.0),
        'w_mod': nrm(ks[8], (DEPTH, D_MODEL, 6 * D_MODEL), D_MODEL ** -0.5),
        'b_mod': nrm(ks[9], (DEPTH, 6 * D_MODEL), 0.02),
        'norm_gains': 1.0 + nrm(ks[10], (DEPTH, 4, D_MODEL), 0.05),
        'w_in': nrm(ks[11], (DEPTH, D_MODEL, PROJ_WIDTH), D_MODEL ** -0.5),
        'qk_gain': 1.0 + nrm(ks[12], (DEPTH, 2, HEAD_DIM), 0.05),
        'w_gla_gate': nrm(ks[13], (DEPTH, N_DIR, GATE_RANK, B_QK_WIDTH), GATE_RANK ** -0.5),
        'b_gla_gate': nrm(ks[14], (DEPTH, N_DIR, B_QK_WIDTH), 0.1),
        'gla_norm': 1.0 + nrm(ks[15], (DEPTH, B_DV), 0.05),
        'conv_w': nrm(ks[16], (DEPTH, CONV_WIDTH, CONV_CH), CONV_WIDTH ** -0.5),
        'a_log': jnp.log(jax.random.uniform(ks[17], (DEPTH, N_DIR, C_HEADS), f32, 1.0, 16.0)),
        'dt_bias': dt + jnp.log(-jnp.expm1(-dt)),
        'delta_norm': 1.0 + nrm(ks[19], (DEPTH, C_DV), 0.05),
        'w_out': nrm(ks[20], (DEPTH, MIX_WIDTH, D_MODEL), MIX_WIDTH ** -0.5),
        'w_gate': nrm(ks[21], (DEPTH, D_MODEL, D_FF), D_MODEL ** -0.5),
        'w_up': nrm(ks[22], (DEPTH, D_MODEL, D_FF), D_MODEL ** -0.5),
        'w_down': nrm(ks[23], (DEPTH, D_FF, D_MODEL), D_FF ** -0.5),
    }


def reference(x_prompt, x_sample, cache_k, cache_v, state_gla, state_delta, c, c_ctx,
              w_mod, b_mod, norm_gains, w_in, qk_gain, w_gla_gate, b_gla_gate, gla_norm,
              conv_w, a_log, dt_bias, delta_norm, w_out, w_gate, w_up, w_down):
    f32 = jnp.float32
    rope = axial_rope(x_sample.shape[1])
    n_ctx_req = x_prompt.shape[0]
    zero_gla = jnp.zeros((n_ctx_req, N_DIR, B_HEADS, B_DK, B_DV), f32)
    zero_delta = jnp.zeros((n_ctx_req, N_DIR, C_HEADS, C_DK, C_DV), f32)
    xp, xs = x_prompt, x_sample
    new_k, new_v, new_gla, new_delta = [], [], [], []
    for l in range(DEPTH):
        lw = (norm_gains[l], w_in[l], qk_gain[l], w_gla_gate[l], b_gla_gate[l], gla_norm[l],
              conv_w[l], a_log[l], dt_bias[l], delta_norm[l], w_out[l],
              w_gate[l], w_up[l], w_down[l])
        mod_ctx = (jax.nn.silu(c_ctx) @ w_mod[l] + b_mod[l])[None, None, :]
        xp, k_l, v_l, sg_l, sd_l = block(xp, mod_ctx, lw, None, zero_gla, zero_delta, None)
        new_k.append(k_l)
        new_v.append(v_l)
        new_gla.append(sg_l)
        new_delta.append(sd_l)
        mod_lat = (jax.nn.silu(c) @ w_mod[l] + b_mod[l])[:, None, :]
        xs = block(xs, mod_lat, lw, (cache_k[:, l], cache_v[:, l]),
                   state_gla[:, l].astype(f32), state_delta[:, l].astype(f32), rope)[0]
    out_dtype = x_prompt.dtype
    new_cache_k = jnp.stack(new_k, axis=1)
    new_cache_v = jnp.stack(new_v, axis=1)
    new_state_gla = jnp.stack(new_gla, axis=1).astype(out_dtype)
    new_state_delta = jnp.stack(new_delta, axis=1).astype(out_dtype)
    return (xp, xs, new_cache_k, new_cache_v, new_state_gla, new_state_delta)
```

```python
import functools

import jax
import jax.numpy as jnp
import numpy as np
from jax import lax
from jax.experimental import pallas as pl
from jax.experimental.pallas import tpu as pltpu

F32 = jnp.float32
BF16 = jnp.bfloat16

D_MODEL = 1024
BATCH = 16
SEQ = 256
DEPTH = 4
DEC_BATCH = 4
DEC_SEQ = 1024
PAST_LEN = 512
GRID_W = 64
HEAD_DIM = 64
A_HEADS = 8
A_KV_HEADS = 2
A_REP = A_HEADS // A_KV_HEADS
A_WIDTH = A_HEADS * HEAD_DIM
A_KV_WIDTH = A_KV_HEADS * HEAD_DIM
ROPE_THETA = 10000.0
B_HEADS = 4
B_DK = 32
B_DV = 64
B_QK_WIDTH = B_HEADS * B_DK
B_WIDTH = B_HEADS * B_DV
GATE_RANK = 16
GATE_TAU = 16.0
C_HEADS = 4
C_DK = 64
C_DV = 64
C_QK_WIDTH = C_HEADS * C_DK
C_WIDTH = C_HEADS * C_DV
CONV_WIDTH = 5
CONV_CH = 2 * C_QK_WIDTH + C_WIDTH
CHUNK = 64
N_DIR = 2
MIX_WIDTH = A_WIDTH + B_WIDTH + C_WIDTH
D_FF = -(-(8 * D_MODEL) // (3 * 256)) * 256
EPS = 1e-6

LANES = 128
N_CTX_ROWS = BATCH * SEQ
N_LAT_ROWS = DEC_BATCH * DEC_SEQ
N_ROWS = N_CTX_ROWS + N_LAT_ROWS
N_COND = 8

ZA_W = A_WIDTH + 2 * A_KV_WIDTH
ZB_W = 2 * B_QK_WIDTH + 2 * B_WIDTH + LANES
ZC_W = 2 * C_QK_WIDTH + 2 * C_WIDTH
PROJ_PAD = ZA_W + ZB_W + ZC_W
SMALL_OFF = 2 * B_QK_WIDTH + 2 * B_WIDTH
BETA_LANE = N_DIR * GATE_RANK
DECAY_LANE = BETA_LANE + N_DIR * C_HEADS

VMEM_LIMIT = 56 * 1024 * 1024


def _split(x):
    hi = x.astype(BF16)
    lo = (x - hi.astype(F32)).astype(BF16)
    return hi, lo


def _bdot(a, b):
    return jnp.dot(a.astype(BF16), b.astype(BF16), preferred_element_type=F32)


def _bdot_nt(a, b):
    return lax.dot_general(a.astype(BF16), b.astype(BF16), (((1,), (1,)), ((), ())),
                           preferred_element_type=F32)


def _bdot_tn(a, b):
    return lax.dot_general(a.astype(BF16), b.astype(BF16), (((0,), (0,)), ((), ())),
                           preferred_element_type=F32)


def _dot3(a, b):
    ah, al = _split(a)
    bh, bl = _split(b)
    return (jnp.dot(ah, bh, preferred_element_type=F32)
            + jnp.dot(ah, bl, preferred_element_type=F32)
            + jnp.dot(al, bh, preferred_element_type=F32))


def _dot_lhs_exact(m_bf16, x):
    xh, xl = _split(x)
    return (jnp.dot(m_bf16, xh, preferred_element_type=F32)
            + jnp.dot(m_bf16, xl, preferred_element_type=F32))


def _dot_rhs_exact(x, m_bf16):
    xh, xl = _split(x)
    return (jnp.dot(xh, m_bf16, preferred_element_type=F32)
            + jnp.dot(xl, m_bf16, preferred_element_type=F32))


def _group_ones(width, group):
    r = lax.broadcasted_iota(jnp.int32, (width, width), 0) // group
    c = lax.broadcasted_iota(jnp.int32, (width, width), 1) // group
    return (r == c).astype(BF16)


def _sigmoid(x):
    return 1.0 / (1.0 + jnp.exp(-x))


def _silu(x):
    return x * _sigmoid(x)


def _softplus(x):
    return jnp.maximum(x, 0.0) + jnp.log1p(jnp.exp(-jnp.abs(x)))


def _rms(x, g):
    return x * lax.rsqrt(jnp.mean(x * x, axis=-1, keepdims=True) + EPS) * g


def _chunk_tri(t, reverse):
    r = lax.broadcasted_iota(jnp.int32, (t, t), 0)
    c = lax.broadcasted_iota(jnp.int32, (t, t), 1)
    same = (r // CHUNK) == (c // CHUNK)
    tri = (c >= r) if reverse else (c <= r)
    return (same & tri).astype(BF16)


def _mod_row(i, tm):
    start = i * tm
    return jnp.where(start < N_CTX_ROWS, 0, 1 + (start - N_CTX_ROWS) // DEC_SEQ)


def _mod_kernel(cond_ref, w_ref, b_ref, o_ref):
    c = cond_ref[...]
    o_ref[0] = _bdot(_silu(c), w_ref[0]) + b_ref[0]


def _modulation(cond, w_mod, b_mod):
    tn = 1536
    return pl.pallas_call(
        _mod_kernel,
        grid=(DEPTH, 6 * D_MODEL // tn),
        in_specs=[pl.BlockSpec((N_COND, D_MODEL), lambda l, j: (0, 0)),
                  pl.BlockSpec((1, D_MODEL, tn), lambda l, j: (l, 0, j)),
                  pl.BlockSpec((1, 1, tn), lambda l, j: (l, 0, j))],
        out_specs=pl.BlockSpec((1, N_COND, tn), lambda l, j: (l, 0, j)),
        out_shape=jax.ShapeDtypeStruct((DEPTH, N_COND, 6 * D_MODEL), F32),
        compiler_params=pltpu.CompilerParams(vmem_limit_bytes=VMEM_LIMIT),
        name="adaln_mod",
    )(cond, w_mod, b_mod.reshape(DEPTH, 1, 6 * D_MODEL))


def _inproj_kernel(x_ref, mod_ref, ng_ref, w_ref, za_ref, zb_ref, zc_ref):
    x = x_ref[...]
    mod = mod_ref[0]
    shift = mod[:, 0:D_MODEL]
    scale = mod[:, D_MODEL:2 * D_MODEL]
    h = _rms(x, ng_ref[0, 0:1, :]) * (1.0 + scale) + shift
    z = _bdot(h, w_ref[0])
    za_ref[...] = z[:, 0:ZA_W]
    zb_ref[...] = z[:, ZA_W:ZA_W + ZB_W]
    zc_ref[...] = z[:, ZA_W + ZB_W:PROJ_PAD]


def _in_projection(x, mods, norm_gains, w_in_p, layer):
    tm = 512
    return pl.pallas_call(
        _inproj_kernel,
        grid=(N_ROWS // tm,),
        in_specs=[pl.BlockSpec((tm, D_MODEL), lambda i: (i, 0)),
                  pl.BlockSpec((1, 1, 6 * D_MODEL), lambda i: (layer * N_COND + _mod_row(i, tm), 0, 0)),
                  pl.BlockSpec((1, 4, D_MODEL), lambda i: (layer, 0, 0)),
                  pl.BlockSpec((1, D_MODEL, PROJ_PAD), lambda i: (layer, 0, 0))],
        out_specs=[pl.BlockSpec((tm, ZA_W), lambda i: (i, 0)),
                   pl.BlockSpec((tm, ZB_W), lambda i: (i, 0)),
                   pl.BlockSpec((tm, ZC_W), lambda i: (i, 0))],
        out_shape=[jax.ShapeDtypeStruct((N_ROWS, ZA_W), F32),
                   jax.ShapeDtypeStruct((N_ROWS, ZB_W), F32),
                   jax.ShapeDtypeStruct((N_ROWS, ZC_W), F32)],
        compiler_params=pltpu.CompilerParams(vmem_limit_bytes=VMEM_LIMIT),
        name="in_proj",
    )(x, mods, norm_gains, w_in_p)


FF_TILE = 256


def _ffn_kernel(x_ref, oa_ref, ob_ref, oc_ref, mod_ref, ng_ref, wo_ref, wg_ref, wu_ref, wd_ref, y_ref):
    x = x_ref[...]
    mod = mod_ref[0]
    gate_m = mod[:, 2 * D_MODEL:3 * D_MODEL]
    shift_f = mod[:, 3 * D_MODEL:4 * D_MODEL]
    scale_f = mod[:, 4 * D_MODEL:5 * D_MODEL]
    gate_f = mod[:, 5 * D_MODEL:6 * D_MODEL]
    mix_in = jnp.concatenate([oa_ref[...], ob_ref[...], oc_ref[...]], axis=-1)
    mix = _bdot(mix_in, wo_ref[0])
    x1 = x + gate_m * _rms(mix, ng_ref[0, 1:2, :])
    h = (_rms(x1, ng_ref[0, 2:3, :]) * (1.0 + scale_f) + shift_f).astype(BF16)
    f = jnp.zeros(x.shape, F32)
    for j in range(D_FF // FF_TILE):
        cols = slice(j * FF_TILE, (j + 1) * FF_TILE)
        g = jnp.dot(h, wg_ref[0, :, cols], preferred_element_type=F32)
        u = jnp.dot(h, wu_ref[0, :, cols], preferred_element_type=F32)
        f = f + _bdot(_silu(g) * u, wd_ref[0, cols, :])
    y_ref[...] = x1 + gate_f * _rms(f, ng_ref[0, 3:4, :])


def _out_ffn(x, oa, ob, oc, mods, norm_gains, w_out_b, w_gate_b, w_up_b, w_down_b, layer):
    tm = 512
    resident = dict(pipeline_mode=pl.Buffered(1))
    return pl.pallas_call(
        _ffn_kernel,
        grid=(N_ROWS // tm,),
        in_specs=[pl.BlockSpec((tm, D_MODEL), lambda i: (i, 0)),
                  pl.BlockSpec((tm, A_WIDTH), lambda i: (i, 0)),
                  pl.BlockSpec((tm, B_WIDTH), lambda i: (i, 0)),
                  pl.BlockSpec((tm, C_WIDTH), lambda i: (i, 0)),
                  pl.BlockSpec((1, 1, 6 * D_MODEL), lambda i: (layer * N_COND + _mod_row(i, tm), 0, 0)),
                  pl.BlockSpec((1, 4, D_MODEL), lambda i: (layer, 0, 0)),
                  pl.BlockSpec((1, MIX_WIDTH, D_MODEL), lambda i: (layer, 0, 0), **resident),
                  pl.BlockSpec((1, D_MODEL, D_FF), lambda i: (layer, 0, 0), **resident),
                  pl.BlockSpec((1, D_MODEL, D_FF), lambda i: (layer, 0, 0), **resident),
                  pl.BlockSpec((1, D_FF, D_MODEL), lambda i: (layer, 0, 0), **resident)],
        out_specs=pl.BlockSpec((tm, D_MODEL), lambda i: (i, 0)),
        out_shape=jax.ShapeDtypeStruct((N_ROWS, D_MODEL), F32),
        compiler_params=pltpu.CompilerParams(vmem_limit_bytes=VMEM_LIMIT),
        name="out_ffn",
    )(x, oa, ob, oc, mods, norm_gains, w_out_b, w_gate_b, w_up_b, w_down_b)


ATT_TQ = 256


def _head_norm(x, gain_row, ones_g):
    ss = _dot_rhs_exact(x * x, ones_g)
    return x * lax.rsqrt(ss * (1.0 / HEAD_DIM) + EPS) * gain_row


def _rope(x, cos, sin_a, sin_b):
    return x * cos + pltpu.roll(x, LANES - 16, 1) * sin_a + pltpu.roll(x, 16, 1) * sin_b


def _attn_kernel(*refs, seq_len, latent):
    if latent:
        (q_ref, kv_ref, ck_ref, cv_ref, qkg_ref, cos_ref, sa_ref, sb_ref, o_ref, k_s, v_s) = refs
    else:
        (q_ref, kv_ref, qkg_ref, o_ref, kn_ref, vn_ref, k_s, v_s) = refs
    j = pl.program_id(1)
    ones_g = _group_ones(LANES, HEAD_DIM)
    gq = qkg_ref[0, 0:1, :]
    gk = qkg_ref[0, 1:2, :]
    past = PAST_LEN if latent else 0

    @pl.when(j == 0)
    def _():
        ka = kv_ref[:, 0:A_KV_WIDTH]
        va = kv_ref[:, A_KV_WIDTH:2 * A_KV_WIDTH]
        kn = _head_norm(ka, gk, ones_g)
        if latent:
            kr = _rope(kn, cos_ref[...], sa_ref[...], sb_ref[...])
            ck = ck_ref[...]
            cv = cv_ref[...]
            for g in range(A_KV_HEADS):
                cols = slice(g * HEAD_DIM, (g + 1) * HEAD_DIM)
                k_s[g, 0:past, :] = ck[:, cols].astype(BF16)
                v_s[g, 0:past, :] = cv[:, cols].astype(BF16)
                k_s[g, past:past + seq_len, :] = kr[:, cols].astype(BF16)
                v_s[g, past:past + seq_len, :] = va[:, cols].astype(BF16)
        else:
            kn_ref[...] = kn
            vn_ref[...] = va
            for g in range(A_KV_HEADS):
                cols = slice(g * HEAD_DIM, (g + 1) * HEAD_DIM)
                k_s[g] = kn[:, cols].astype(BF16)
                v_s[g] = va[:, cols].astype(BF16)

    if latent:
        rows = pl.ds(pl.multiple_of(j * ATT_TQ, ATT_TQ), ATT_TQ)
        cos = cos_ref[rows, :]
        sin_a = sa_ref[rows, :]
        sin_b = sb_ref[rows, :]
    outs = []
    for t in range(A_WIDTH // LANES):
        qt = _head_norm(q_ref[:, t * LANES:(t + 1) * LANES], gq, ones_g)
        if latent:
            qt = _rope(qt, cos, sin_a, sin_b)
        qt = (qt * (HEAD_DIM ** -0.5)).astype(BF16)
        for hh in range(LANES // HEAD_DIM):
            h = t * (LANES // HEAD_DIM) + hh
            g = h // A_REP
            qh = qt[:, hh * HEAD_DIM:(hh + 1) * HEAD_DIM]
            s = lax.dot_general(qh, k_s[g], (((1,), (1,)), ((), ())), preferred_element_type=F32)
            m = jnp.max(s, axis=-1, keepdims=True)
            p = jnp.exp(s - m)
            l = jnp.sum(p, axis=-1, keepdims=True)
            o = jnp.dot(p.astype(BF16), v_s[g], preferred_element_type=F32)
            outs.append(o / l)
    o_ref[...] = jnp.concatenate(outs, axis=-1)


def _attention(za, qk_gain2, layer, latent, cache_k=None, cache_v=None, rope=None):
    seq_len = DEC_SEQ if latent else SEQ
    n_seq = DEC_BATCH if latent else BATCH
    row0 = N_CTX_ROWS // seq_len if latent else 0
    nq = seq_len // ATT_TQ
    row0q = N_CTX_ROWS // ATT_TQ if latent else 0
    s_len = seq_len + (PAST_LEN if latent else 0)
    in_specs = [pl.BlockSpec((ATT_TQ, A_WIDTH), lambda s, j: (row0q + s * nq + j, 0)),
                pl.BlockSpec((seq_len, 2 * A_KV_WIDTH), lambda s, j: (row0 + s, 2))]
    args = [za, za]
    if latent:
        in_specs += [pl.BlockSpec((None, None, PAST_LEN, A_KV_WIDTH), lambda s, j: (s, layer, 0, 0)),
                     pl.BlockSpec((None, None, PAST_LEN, A_KV_WIDTH), lambda s, j: (s, layer, 0, 0))]
        args += [cache_k, cache_v]
    in_specs.append(pl.BlockSpec((1, 2, LANES), lambda s, j: (layer, 0, 0)))
    args.append(qk_gain2)
    if latent:
        in_specs += [pl.BlockSpec((seq_len, LANES), lambda s, j: (0, 0))] * 3
        args += list(rope)
    out_specs = [pl.BlockSpec((ATT_TQ, A_WIDTH), lambda s, j: (s * nq + j, 0))]
    out_shape = [jax.ShapeDtypeStruct((n_seq * seq_len, A_WIDTH), F32)]
    if not latent:
        out_specs += [pl.BlockSpec((None, seq_len, A_KV_WIDTH), lambda s, j: (s, 0, 0))] * 2
        out_shape += [jax.ShapeDtypeStruct((n_seq, seq_len, A_KV_WIDTH), F32)] * 2
    return pl.pallas_call(
        functools.partial(_attn_kernel, seq_len=seq_len, latent=latent),
        grid=(n_seq, nq),
        in_specs=in_specs,
        out_specs=out_specs,
        out_shape=out_shape,
        scratch_shapes=[pltpu.VMEM((A_KV_HEADS, s_len, HEAD_DIM), BF16),
                        pltpu.VMEM((A_KV_HEADS, s_len, HEAD_DIM), BF16)],
        compiler_params=pltpu.CompilerParams(
            dimension_semantics=("arbitrary", "arbitrary"), vmem_limit_bytes=VMEM_LIMIT),
        name="attn_latent" if latent else "attn_ctx",
    )(*args)


def _gla_kernel(*refs, seq_len, latent):
    if latent:
        zb_ref, wg_ref, bg_ref, gn_ref, s0_ref, ob_ref = refs
    else:
        zb_ref, wg_ref, bg_ref, gn_ref, ob_ref, sfin_ref = refs
    nc = seq_len // CHUNK
    q = zb_ref[:, 0:B_QK_WIDTH] * (B_DK ** -0.5)
    k = zb_ref[:, B_QK_WIDTH:2 * B_QK_WIDTH]
    v = zb_ref[:, 2 * B_QK_WIDTH:2 * B_QK_WIDTH + B_WIDTH]
    small = zb_ref[:, SMALL_OFF:SMALL_OFF + LANES]
    pre = _dot3(small, wg_ref[0]) + bg_ref[0]
    glog = (jnp.minimum(pre, 0.0) - jnp.log1p(jnp.exp(-jnp.abs(pre)))) * (1.0 / GATE_TAU)

    kr = lax.broadcasted_iota(jnp.int32, (B_QK_WIDTH, B_WIDTH), 0) // B_DK
    vc = lax.broadcasted_iota(jnp.int32, (B_QK_WIDTH, B_WIDTH), 1) // B_DV
    bd_kv = kr == vc
    sr = lax.broadcasted_iota(jnp.int32, (B_WIDTH, B_WIDTH), 0) // CHUNK
    sc = lax.broadcasted_iota(jnp.int32, (B_WIDTH, B_WIDTH), 1) // B_DV
    bd_sv = sr == sc
    tt = lax.broadcasted_iota(jnp.int32, (CHUNK, B_WIDTH), 0)
    ss = lax.broadcasted_iota(jnp.int32, (CHUNK, B_WIDTH), 1) % CHUNK

    q3 = q.reshape(nc, CHUNK, B_QK_WIDTH)
    k3 = k.reshape(nc, CHUNK, B_QK_WIDTH)
    o_dirs = []
    for d in range(N_DIR):
        g = glog[:, d * B_QK_WIDTH:(d + 1) * B_QK_WIDTH]
        b = _dot_lhs_exact(_chunk_tri(seq_len, d == 1), g)
        b3 = b.reshape(nc, CHUNK, B_QK_WIDTH)
        mid = b3[:, CHUNK // 2:CHUNK // 2 + 1, :]
        last = b3[:, 0:1, :] if d == 1 else b3[:, CHUNK - 1:CHUNK, :]
        q_in = q3 * jnp.exp(b3 - mid)
        k_in = k3 * jnp.exp(mid - b3)
        q_st = q3 * jnp.exp(b3)
        k_st = k3 * jnp.exp(last - b3)
        causal = (ss >= tt) if d == 1 else (ss <= tt)
        state = s0_ref[d] if latent else jnp.zeros((B_QK_WIDTH, B_WIDTH), F32)
        o_chunks = [None] * nc
        order = range(nc - 1, -1, -1) if d == 1 else range(nc)
        for c in order:
            rows = slice(c * CHUNK, (c + 1) * CHUNK)
            vc_ = v[rows, :]
            v_bd = jnp.where(bd_sv, jnp.concatenate([vc_] * B_HEADS, axis=0), 0.0)
            k_in_t = k_in[c].T
            k_bd = jnp.where(bd_kv, jnp.concatenate([k_in_t] * B_HEADS, axis=1), 0.0)
            att = jnp.where(causal, _bdot(q_in[c], k_bd), 0.0)
            o_chunks[c] = _bdot(att, v_bd) + _bdot(q_st[c], state)
            decay = jnp.exp(b3[c].T[:, (0 if d == 1 else CHUNK - 1):(1 if d == 1 else CHUNK)])
            upd = jnp.where(bd_kv, _bdot_tn(k_st[c], vc_), 0.0)
            state = decay * state + upd
        o_dirs.append(jnp.concatenate(o_chunks, axis=0))
        if not latent:
            sfin_ref[d] = state
    o = o_dirs[0] + o_dirs[1]
    ms = _dot_rhs_exact(o * o, _group_ones(B_WIDTH, B_DV)) * (1.0 / B_DV)
    r = zb_ref[:, 2 * B_QK_WIDTH + B_WIDTH:2 * B_QK_WIDTH + 2 * B_WIDTH]
    ob_ref[...] = o * lax.rsqrt(ms + EPS) * gn_ref[0] * _silu(r)


def _gla(zb, wg_p, bg_p, gn_p, layer, latent, s0_bd=None):
    seq_len = DEC_SEQ if latent else SEQ
    n_seq = DEC_BATCH if latent else BATCH
    row0 = N_CTX_ROWS // seq_len if latent else 0
    in_specs = [pl.BlockSpec((seq_len, ZB_W), lambda s: (row0 + s, 0)),
                pl.BlockSpec((1, LANES, N_DIR * B_QK_WIDTH), lambda s: (layer, 0, 0)),
                pl.BlockSpec((1, 1, N_DIR * B_QK_WIDTH), lambda s: (layer, 0, 0)),
                pl.BlockSpec((1, 1, B_WIDTH), lambda s: (layer, 0, 0))]
    args = [zb, wg_p, bg_p, gn_p]
    out_specs = [pl.BlockSpec((seq_len, B_WIDTH), lambda s: (s, 0))]
    out_shape = [jax.ShapeDtypeStruct((n_seq * seq_len, B_WIDTH), F32)]
    if latent:
        in_specs.append(pl.BlockSpec((None, None, N_DIR, B_QK_WIDTH, B_WIDTH), lambda s: (s, layer, 0, 0, 0)))
        args.append(s0_bd)
    else:
        out_specs.append(pl.BlockSpec((None, N_DIR, B_QK_WIDTH, B_WIDTH), lambda s: (s, 0, 0, 0)))
        out_shape.append(jax.ShapeDtypeStruct((n_seq, N_DIR, B_QK_WIDTH, B_WIDTH), F32))
    return pl.pallas_call(
        functools.partial(_gla_kernel, seq_len=seq_len, latent=latent),
        grid=(n_seq,),
        in_specs=in_specs,
        out_specs=out_specs,
        out_shape=out_shape,
        compiler_params=pltpu.CompilerParams(vmem_limit_bytes=VMEM_LIMIT),
        name="gla_latent" if latent else "gla_ctx",
    )(*args)


TRI_BASE = 8


def _unit_tri_inverse(m):
    rr = lax.broadcasted_iota(jnp.int32, (CHUNK, CHUNK), 0)
    cc = lax.broadcasted_iota(jnp.int32, (CHUNK, CHUNK), 1)
    pw = jnp.where((rr // TRI_BASE) == (cc // TRI_BASE), -m, 0.0)
    inv = (rr == cc).astype(F32) + pw
    span = 2
    while span < TRI_BASE:
        pw = _dot3(pw, pw)
        inv = inv + _dot3(inv, pw)
        span *= 2
    size = 2 * TRI_BASE
    while size <= CHUNK:
        off = ((rr // size) == (cc // size)) & ((rr // (size // 2)) != (cc // (size // 2)))
        inv = inv - _dot3(inv, _dot3(jnp.where(off, m, 0.0), inv))
        size *= 2
    return inv


def _delta_kernel(*refs, seq_len, latent):
    if latent:
        (zc_ref, sm_ref, cw_ref, al_ref, dtb_ref, dn_ref, s0_ref, oc_ref,
         q_s, k_s, v_s, beta_s, gam_s, o_s, st_s) = refs
    else:
        (zc_ref, sm_ref, cw_ref, al_ref, dtb_ref, dn_ref, oc_ref, sfin_ref,
         q_s, k_s, v_s, beta_s, gam_s, o_s, st_s) = refs
    nc = seq_len // CHUNK

    x = zc_ref[:, 0:CONV_CH]
    t_idx = lax.broadcasted_iota(jnp.int32, (seq_len, 1), 0)
    y = x * cw_ref[0, CONV_WIDTH // 2:CONV_WIDTH // 2 + 1, :]
    for tap in range(CONV_WIDTH):
        off = tap - CONV_WIDTH // 2
        if off == 0:
            continue
        shifted = pltpu.roll(x, (-off) % seq_len, 0)
        valid = (t_idx + off >= 0) & (t_idx + off < seq_len)
        y = y + jnp.where(valid, shifted, 0.0) * cw_ref[0, tap:tap + 1, :]
    y = _silu(y)
    ones_g = _group_ones(C_QK_WIDTH, C_DK)
    qc = y[:, 0:C_QK_WIDTH]
    kc = y[:, C_QK_WIDTH:2 * C_QK_WIDTH]
    q_s[...] = qc * lax.rsqrt(_dot_rhs_exact(qc * qc, ones_g) + EPS) * (C_DK ** -0.5)
    k_s[...] = kc * lax.rsqrt(_dot_rhs_exact(kc * kc, ones_g) + EPS)
    v_s[...] = y[:, 2 * C_QK_WIDTH:CONV_CH]

    small = sm_ref[...]
    beta_s[...] = _sigmoid(small)
    glog = -jnp.exp(al_ref[0]) * _softplus(small + dtb_ref[0])
    for d in range(N_DIR):
        gam_s[d] = _dot_lhs_exact(_chunk_tri(seq_len, d == 1), glog)
        for h in range(C_HEADS):
            st_s[d * C_HEADS + h] = s0_ref[d, h] if latent else jnp.zeros((C_DK, C_DV), F32)

    rr = lax.broadcasted_iota(jnp.int32, (CHUNK, CHUNK), 0)
    cc = lax.broadcasted_iota(jnp.int32, (CHUNK, CHUNK), 1)

    def chunk_step(c, carry):
        for d in range(N_DIR):
            cidx = (nc - 1 - c) if d == 1 else c
            rows = pl.ds(pl.multiple_of(cidx * CHUNK, CHUNK), CHUNK)
            incl = (cc >= rr) if d == 1 else (cc <= rr)
            strict = (cc > rr) if d == 1 else (cc < rr)
            gam_blk = gam_s[d, rows, :]
            gam_t = gam_blk.T
            beta_blk = beta_s[rows, :]
            last = 0 if d == 1 else CHUNK - 1
            for h in range(C_HEADS):
                cols = slice(h * C_DK, (h + 1) * C_DK)
                lane = d * C_HEADS + h
                qh = q_s[rows, cols]
                kh = k_s[rows, cols]
                vh = v_s[rows, cols]
                bcol = beta_blk[:, BETA_LANE + lane:BETA_LANE + lane + 1]
                gcol = gam_blk[:, DECAY_LANE + lane:DECAY_LANE + lane + 1]
                grow = gam_t[DECAY_LANE + lane:DECAY_LANE + lane + 1, :]
                dec = jnp.exp(jnp.where(incl, gcol - grow, -jnp.inf))
                kb = kh * bcol
                m = jnp.where(strict, _bdot_nt(kb, kh) * dec, 0.0)
                inv = _unit_tri_inverse(m)
                u = _dot3(inv, vh * bcol)
                w = _dot3(inv, kb * jnp.exp(gcol))
                state = st_s[lane]
                v_new = u - _bdot(w, state)
                att = _bdot_nt(qh, kh) * dec
                o = _bdot(qh * jnp.exp(gcol), state) + _bdot(att, v_new)
                glast = gcol[last:last + 1, :]
                st_s[lane] = jnp.exp(glast) * state + _bdot_tn(kh * jnp.exp(glast - gcol), v_new)
                o_s[d, rows, cols] = o
        return carry

    lax.fori_loop(0, nc, chunk_step, 0)

    o = o_s[0] + o_s[1]
    ms = _dot_rhs_exact(o * o, _group_ones(C_WIDTH, C_DV)) * (1.0 / C_DV)
    gate = zc_ref[:, CONV_CH:CONV_CH + C_WIDTH]
    oc_ref[...] = o * lax.rsqrt(ms + EPS) * dn_ref[0] * _silu(gate)
    if not latent:
        for lane in range(N_DIR * C_HEADS):
            sfin_ref[lane // C_HEADS, lane % C_HEADS] = st_s[lane]


def _delta(zc, zb, conv_w, al_p, dtb_p, dn_p, layer, latent, state_delta=None):
    seq_len = DEC_SEQ if latent else SEQ
    n_seq = DEC_BATCH if latent else BATCH
    row0 = N_CTX_ROWS // seq_len if latent else 0
    in_specs = [pl.BlockSpec((seq_len, ZC_W), lambda s: (row0 + s, 0)),
                pl.BlockSpec((seq_len, LANES), lambda s: (row0 + s, SMALL_OFF // LANES)),
                pl.BlockSpec((1, CONV_WIDTH, CONV_CH), lambda s: (layer, 0, 0)),
                pl.BlockSpec((1, 1, LANES), lambda s: (layer, 0, 0)),
                pl.BlockSpec((1, 1, LANES), lambda s: (layer, 0, 0)),
                pl.BlockSpec((1, 1, C_WIDTH), lambda s: (layer, 0, 0))]
    args = [zc, zb, conv_w, al_p, dtb_p, dn_p]
    out_specs = [pl.BlockSpec((seq_len, C_WIDTH), lambda s: (s, 0))]
    out_shape = [jax.ShapeDtypeStruct((n_seq * seq_len, C_WIDTH), F32)]
    if latent:
        in_specs.append(pl.BlockSpec((None, None, N_DIR, C_HEADS, C_DK, C_DV), lambda s: (s, layer, 0, 0, 0, 0)))
        args.append(state_delta)
    else:
        out_specs.append(pl.BlockSpec((None, N_DIR, C_HEADS, C_DK, C_DV), lambda s: (s, 0, 0, 0, 0)))
        out_shape.append(jax.ShapeDtypeStruct((n_seq, N_DIR, C_HEADS, C_DK, C_DV), F32))
    return pl.pallas_call(
        functools.partial(_delta_kernel, seq_len=seq_len, latent=latent),
        grid=(n_seq,),
        in_specs=in_specs,
        out_specs=out_specs,
        out_shape=out_shape,
        scratch_shapes=[pltpu.VMEM((seq_len, C_QK_WIDTH), F32),
                        pltpu.VMEM((seq_len, C_QK_WIDTH), F32),
                        pltpu.VMEM((seq_len, C_WIDTH), F32),
                        pltpu.VMEM((seq_len, LANES), F32),
                        pltpu.VMEM((N_DIR, seq_len, LANES), F32),
                        pltpu.VMEM((N_DIR, seq_len, C_WIDTH), F32),
                        pltpu.VMEM((N_DIR * C_HEADS, C_DK, C_DV), F32)],
        compiler_params=pltpu.CompilerParams(vmem_limit_bytes=VMEM_LIMIT),
        name="delta_latent" if latent else "delta_ctx",
    )(*args)


def _rope_tables():
    rows = DEC_SEQ // GRID_W
    row = jnp.repeat(jnp.arange(rows, dtype=F32), GRID_W)
    col = jnp.tile(jnp.arange(GRID_W, dtype=F32), rows)
    n_freq = HEAD_DIM // 4
    inv_freq = ROPE_THETA ** (-jnp.arange(n_freq, dtype=F32) / n_freq)
    ang_r = row[:, None] * inv_freq
    ang_c = col[:, None] * inv_freq
    ang = jnp.concatenate([ang_r, ang_r, ang_c, ang_c], axis=-1)
    cos, sin = jnp.cos(ang), jnp.sin(ang)
    first = (jnp.arange(HEAD_DIM) % 32) < 16
    sin_a = jnp.where(first, -sin, 0.0)
    sin_b = jnp.where(first, 0.0, sin)
    reps = LANES // HEAD_DIM
    return tuple(jnp.tile(t, (1, reps)) for t in (cos, sin_a, sin_b))


def _block_diag_state(s):
    eye = jnp.eye(B_HEADS, dtype=s.dtype)
    out = jnp.einsum('...hkv,hg->...hkgv', s, eye)
    return out.reshape(s.shape[:-3] + (B_QK_WIDTH, B_WIDTH))


def _diag_blocks(s_bd):
    s5 = s_bd.reshape(s_bd.shape[:-2] + (B_HEADS, B_DK, B_HEADS, B_DV))
    return jnp.stack([s5[..., h, :, h, :] for h in range(B_HEADS)], axis=-3)


def kernel(x_prompt, x_sample, cache_k, cache_v, state_gla, state_delta, c, c_ctx, w_mod, b_mod, norm_gains, w_in, qk_gain, w_gla_gate, b_gla_gate, gla_norm, conv_w, a_log, dt_bias, delta_norm, w_out, w_gate, w_up, w_down):
    x = jnp.concatenate([x_prompt.reshape(N_CTX_ROWS, D_MODEL), x_sample.reshape(N_LAT_ROWS, D_MODEL)], axis=0)
    cond = jnp.concatenate([c_ctx[None, :], c, jnp.zeros((N_COND - 1 - DEC_BATCH, D_MODEL), F32)], axis=0)
    nat_small = A_WIDTH + 2 * A_KV_WIDTH + 2 * B_QK_WIDTH + 2 * B_WIDTH
    nat_c = nat_small + N_DIR * GATE_RANK
    nat_bc = nat_c + ZC_W
    w_in_p = jnp.concatenate(
        [w_in[:, :, 0:nat_small], w_in[:, :, nat_small:nat_c], w_in[:, :, nat_bc:nat_bc + 2 * N_DIR * C_HEADS],
         jnp.zeros((DEPTH, D_MODEL, LANES - N_DIR * GATE_RANK - 2 * N_DIR * C_HEADS), w_in.dtype),
         w_in[:, :, nat_c:nat_bc]], axis=-1).astype(BF16)
    w_out_b = w_out.astype(BF16)
    w_gate_b = w_gate.astype(BF16)
    w_up_b = w_up.astype(BF16)
    w_down_b = w_down.astype(BF16)
    qk_gain2 = jnp.tile(qk_gain, (1, 1, LANES // HEAD_DIM))
    wg_p = jnp.zeros((DEPTH, LANES, N_DIR * B_QK_WIDTH), F32)
    for d in range(N_DIR):
        wg_p = wg_p.at[:, d * GATE_RANK:(d + 1) * GATE_RANK, d * B_QK_WIDTH:(d + 1) * B_QK_WIDTH].set(w_gla_gate[:, d])
    bg_p = b_gla_gate.reshape(DEPTH, 1, N_DIR * B_QK_WIDTH)
    gn_p = jnp.tile(gla_norm, (1, B_HEADS)).reshape(DEPTH, 1, B_WIDTH)
    dn_p = jnp.tile(delta_norm, (1, C_HEADS)).reshape(DEPTH, 1, C_WIDTH)
    al_p = jnp.zeros((DEPTH, 1, LANES), F32).at[:, 0, DECAY_LANE:DECAY_LANE + N_DIR * C_HEADS].set(
        a_log.reshape(DEPTH, N_DIR * C_HEADS))
    dtb_p = jnp.zeros((DEPTH, 1, LANES), F32).at[:, 0, DECAY_LANE:DECAY_LANE + N_DIR * C_HEADS].set(
        dt_bias.reshape(DEPTH, N_DIR * C_HEADS))
    cache_k2 = cache_k.reshape(DEC_BATCH, DEPTH, PAST_LEN, A_KV_WIDTH)
    cache_v2 = cache_v.reshape(DEC_BATCH, DEPTH, PAST_LEN, A_KV_WIDTH)
    s0_gla_bd = _block_diag_state(state_gla.astype(F32))
    s0_delta = state_delta.astype(F32)
    rope = _rope_tables()

    mods = _modulation(cond, w_mod, b_mod).reshape(DEPTH * N_COND, 1, 6 * D_MODEL)

    new_k, new_v, new_gla, new_delta = [], [], [], []
    for l in range(DEPTH):
        za, zb, zc = _in_projection(x, mods, norm_gains, w_in_p, l)
        oa_c, k_l, v_l = _attention(za, qk_gain2, l, False)
        (oa_l,) = _attention(za, qk_gain2, l, True, cache_k2, cache_v2, rope)
        ob_c, sg_l = _gla(zb, wg_p, bg_p, gn_p, l, False)
        (ob_l,) = _gla(zb, wg_p, bg_p, gn_p, l, True, s0_gla_bd)
        oc_c, sd_l = _delta(zc, zb, conv_w, al_p, dtb_p, dn_p, l, False)
        (oc_l,) = _delta(zc, zb, conv_w, al_p, dtb_p, dn_p, l, True, s0_delta)
        oa = jnp.concatenate([oa_c, oa_l], axis=0)
        ob = jnp.concatenate([ob_c, ob_l], axis=0)
        oc = jnp.concatenate([oc_c, oc_l], axis=0)
        x = _out_ffn(x, oa, ob, oc, mods, norm_gains, w_out_b, w_gate_b, w_up_b, w_down_b, l)
        new_k.append(k_l.reshape(BATCH, SEQ, A_KV_HEADS, HEAD_DIM))
        new_v.append(v_l.reshape(BATCH, SEQ, A_KV_HEADS, HEAD_DIM))
        new_gla.append(_diag_blocks(sg_l))
        new_delta.append(sd_l)

    out_dtype = x_prompt.dtype
    y_prompt = x[:N_CTX_ROWS].reshape(BATCH, SEQ, D_MODEL)
    y_sample = x[N_CTX_ROWS:].reshape(DEC_BATCH, DEC_SEQ, D_MODEL)
    return (y_prompt, y_sample,
            jnp.stack(new_k, axis=1), jnp.stack(new_v, axis=1),
            jnp.stack(new_gla, axis=1).astype(out_dtype),
            jnp.stack(new_delta, axis=1).astype(out_dtype))
```

```python
import functools

import jax
import jax.numpy as jnp
import numpy as np
from jax import lax
from jax.experimental import pallas as pl
from jax.experimental.pallas import tpu as pltpu

F32 = jnp.float32
BF16 = jnp.bfloat16

D_MODEL = 1024
BATCH = 16
SEQ = 256
DEPTH = 4
DEC_BATCH = 4
DEC_SEQ = 1024
PAST_LEN = 512
GRID_W = 64
HEAD_DIM = 64
A_HEADS = 8
A_KV_HEADS = 2
A_REP = A_HEADS // A_KV_HEADS
A_WIDTH = A_HEADS * HEAD_DIM
A_KV_WIDTH = A_KV_HEADS * HEAD_DIM
ROPE_THETA = 10000.0
B_HEADS = 4
B_DK = 32
B_DV = 64
B_QK_WIDTH = B_HEADS * B_DK
B_WIDTH = B_HEADS * B_DV
GATE_RANK = 16
GATE_TAU = 16.0
C_HEADS = 4
C_DK = 64
C_DV = 64
C_QK_WIDTH = C_HEADS * C_DK
C_WIDTH = C_HEADS * C_DV
CONV_WIDTH = 5
CONV_CH = 2 * C_QK_WIDTH + C_WIDTH
CHUNK = 64
N_DIR = 2
MIX_WIDTH = A_WIDTH + B_WIDTH + C_WIDTH
D_FF = -(-(8 * D_MODEL) // (3 * 256)) * 256
EPS = 1e-6

LANES = 128
N_CTX_ROWS = BATCH * SEQ
N_LAT_ROWS = DEC_BATCH * DEC_SEQ
N_ROWS = N_CTX_ROWS + N_LAT_ROWS
N_COND = 8

ZA_W = A_WIDTH + 2 * A_KV_WIDTH
ZB_W = 2 * B_QK_WIDTH + 2 * B_WIDTH + LANES
ZC_W = 2 * C_QK_WIDTH + 2 * C_WIDTH
PROJ_PAD = ZA_W + ZB_W + ZC_W
SMALL_OFF = 2 * B_QK_WIDTH + 2 * B_WIDTH
BETA_LANE = N_DIR * GATE_RANK
DECAY_LANE = BETA_LANE + N_DIR * C_HEADS

VMEM_LIMIT = 56 * 1024 * 1024


def _split(x):
    hi = x.astype(BF16)
    lo = (x - hi.astype(F32)).astype(BF16)
    return hi, lo


def _bdot(a, b):
    return jnp.dot(a.astype(BF16), b.astype(BF16), preferred_element_type=F32)


def _bdot_nt(a, b):
    return lax.dot_general(a.astype(BF16), b.astype(BF16), (((1,), (1,)), ((), ())),
                           preferred_element_type=F32)


def _bdot_tn(a, b):
    return lax.dot_general(a.astype(BF16), b.astype(BF16), (((0,), (0,)), ((), ())),
                           preferred_element_type=F32)


def _dot3(a, b):
    ah, al = _split(a)
    bh, bl = _split(b)
    return (jnp.dot(ah, bh, preferred_element_type=F32)
            + jnp.dot(ah, bl, preferred_element_type=F32)
            + jnp.dot(al, bh, preferred_element_type=F32))


def _dot_lhs_exact(m_bf16, x):
    xh, xl = _split(x)
    return (jnp.dot(m_bf16, xh, preferred_element_type=F32)
            + jnp.dot(m_bf16, xl, preferred_element_type=F32))


def _dot_rhs_exact(x, m_bf16):
    xh, xl = _split(x)
    return (jnp.dot(xh, m_bf16, preferred_element_type=F32)
            + jnp.dot(xl, m_bf16, preferred_element_type=F32))


def _group_ones(width, group):
    r = lax.broadcasted_iota(jnp.int32, (width, width), 0) // group
    c = lax.broadcasted_iota(jnp.int32, (width, width), 1) // group
    return (r == c).astype(BF16)


def _sigmoid(x):
    return 1.0 / (1.0 + jnp.exp(-x))


def _silu(x):
    return x * _sigmoid(x)


def _softplus(x):
    return jnp.maximum(x, 0.0) + jnp.log1p(jnp.exp(-jnp.abs(x)))


def _rms(x, g):
    return x * lax.rsqrt(jnp.mean(x * x, axis=-1, keepdims=True) + EPS) * g


def _chunk_tri(t, reverse):
    r = lax.broadcasted_iota(jnp.int32, (t, t), 0)
    c = lax.broadcasted_iota(jnp.int32, (t, t), 1)
    same = (r // CHUNK) == (c // CHUNK)
    tri = (c >= r) if reverse else (c <= r)
    return (same & tri).astype(BF16)


def _mod_row(i, tm):
    start = i * tm
    return jnp.where(start < N_CTX_ROWS, 0, 1 + (start - N_CTX_ROWS) // DEC_SEQ)


def _mod_kernel(cond_ref, w_ref, b_ref, o_ref):
    c = cond_ref[...]
    o_ref[0] = _bdot(_silu(c), w_ref[0]) + b_ref[0]


def _modulation(cond, w_mod, b_mod):
    tn = 1536
    return pl.pallas_call(
        _mod_kernel,
        grid=(DEPTH, 6 * D_MODEL // tn),
        in_specs=[pl.BlockSpec((N_COND, D_MODEL), lambda l, j: (0, 0)),
                  pl.BlockSpec((1, D_MODEL, tn), lambda l, j: (l, 0, j)),
                  pl.BlockSpec((1, 1, tn), lambda l, j: (l, 0, j))],
        out_specs=pl.BlockSpec((1, N_COND, tn), lambda l, j: (l, 0, j)),
        out_shape=jax.ShapeDtypeStruct((DEPTH, N_COND, 6 * D_MODEL), F32),
        compiler_params=pltpu.CompilerParams(vmem_limit_bytes=VMEM_LIMIT),
        name="adaln_mod",
    )(cond, w_mod, b_mod.reshape(DEPTH, 1, 6 * D_MODEL))


def _inproj_kernel(x_ref, mod_ref, ng_ref, w_ref, za_ref, zb_ref, zc_ref):
    x = x_ref[...]
    mod = mod_ref[0]
    shift = mod[:, 0:D_MODEL]
    scale = mod[:, D_MODEL:2 * D_MODEL]
    h = _rms(x, ng_ref[0, 0:1, :]) * (1.0 + scale) + shift
    z = _bdot(h, w_ref[0])
    za_ref[...] = z[:, 0:ZA_W]
    zb_ref[...] = z[:, ZA_W:ZA_W + ZB_W]
    zc_ref[...] = z[:, ZA_W + ZB_W:PROJ_PAD]


def _in_projection(x, mods, norm_gains, w_in_p, layer):
    tm = 512
    return pl.pallas_call(
        _inproj_kernel,
        grid=(N_ROWS // tm,),
        in_specs=[pl.BlockSpec((tm, D_MODEL), lambda i: (i, 0)),
                  pl.BlockSpec((1, 1, 6 * D_MODEL), lambda i: (layer * N_COND + _mod_row(i, tm), 0, 0)),
                  pl.BlockSpec((1, 4, D_MODEL), lambda i: (layer, 0, 0)),
                  pl.BlockSpec((1, D_MODEL, PROJ_PAD), lambda i: (layer, 0, 0))],
        out_specs=[pl.BlockSpec((tm, ZA_W), lambda i: (i, 0)),
                   pl.BlockSpec((tm, ZB_W), lambda i: (i, 0)),
                   pl.BlockSpec((tm, ZC_W), lambda i: (i, 0))],
        out_shape=[jax.ShapeDtypeStruct((N_ROWS, ZA_W), F32),
                   jax.ShapeDtypeStruct((N_ROWS, ZB_W), F32),
                   jax.ShapeDtypeStruct((N_ROWS, ZC_W), F32)],
        compiler_params=pltpu.CompilerParams(vmem_limit_bytes=VMEM_LIMIT),
        name="in_proj",
    )(x, mods, norm_gains, w_in_p)


FF_TILE = 256


def _ffn_kernel(x_ref, oa_ref, ob_ref, oc_ref, mod_ref, ng_ref, wo_ref, wg_ref, wu_ref, wd_ref, y_ref):
    x = x_ref[...]
    mod = mod_ref[0]
    gate_m = mod[:, 2 * D_MODEL:3 * D_MODEL]
    shift_f = mod[:, 3 * D_MODEL:4 * D_MODEL]
    scale_f = mod[:, 4 * D_MODEL:5 * D_MODEL]
    gate_f = mod[:, 5 * D_MODEL:6 * D_MODEL]
    mix_in = jnp.concatenate([oa_ref[...], ob_ref[...], oc_ref[...]], axis=-1)
    mix = _bdot(mix_in, wo_ref[0])
    x1 = x + gate_m * _rms(mix, ng_ref[0, 1:2, :])
    h = (_rms(x1, ng_ref[0, 2:3, :]) * (1.0 + scale_f) + shift_f).astype(BF16)
    f = jnp.zeros(x.shape, F32)
    for j in range(D_FF // FF_TILE):
        cols = slice(j * FF_TILE, (j + 1) * FF_TILE)
        g = jnp.dot(h, wg_ref[0, :, cols], preferred_element_type=F32)
        u = jnp.dot(h, wu_ref[0, :, cols], preferred_element_type=F32)
        f = f + _bdot(_silu(g) * u, wd_ref[0, cols, :])
    y_ref[...] = x1 + gate_f * _rms(f, ng_ref[0, 3:4, :])


def _out_ffn(x, oa, ob, oc, mods, norm_gains, w_out_b, w_gate_b, w_up_b, w_down_b, layer):
    tm = 512
    resident = dict(pipeline_mode=pl.Buffered(1))
    return pl.pallas_call(
        _ffn_kernel,
        grid=(N_ROWS // tm,),
        in_specs=[pl.BlockSpec((tm, D_MODEL), lambda i: (i, 0)),
                  pl.BlockSpec((tm, A_WIDTH), lambda i: (i, 0)),
                  pl.BlockSpec((tm, B_WIDTH), lambda i: (i, 0)),
                  pl.BlockSpec((tm, C_WIDTH), lambda i: (i, 0)),
                  pl.BlockSpec((1, 1, 6 * D_MODEL), lambda i: (layer * N_COND + _mod_row(i, tm), 0, 0)),
                  pl.BlockSpec((1, 4, D_MODEL), lambda i: (layer, 0, 0)),
                  pl.BlockSpec((1, MIX_WIDTH, D_MODEL), lambda i: (layer, 0, 0), **resident),
                  pl.BlockSpec((1, D_MODEL, D_FF), lambda i: (layer, 0, 0), **resident),
                  pl.BlockSpec((1, D_MODEL, D_FF), lambda i: (layer, 0, 0), **resident),
                  pl.BlockSpec((1, D_FF, D_MODEL), lambda i: (layer, 0, 0), **resident)],
        out_specs=pl.BlockSpec((tm, D_MODEL), lambda i: (i, 0)),
        out_shape=jax.ShapeDtypeStruct((N_ROWS, D_MODEL), F32),
        compiler_params=pltpu.CompilerParams(vmem_limit_bytes=VMEM_LIMIT),
        name="out_ffn",
    )(x, oa, ob, oc, mods, norm_gains, w_out_b, w_gate_b, w_up_b, w_down_b)


ATT_TQ = 256


def _head_norm(x, gain_row, ones_g):
    ss = _dot_rhs_exact(x * x, ones_g)
    return x * lax.rsqrt(ss * (1.0 / HEAD_DIM) + EPS) * gain_row


def _rope(x, cos, sin_a, sin_b):
    return x * cos + pltpu.roll(x, LANES - 16, 1) * sin_a + pltpu.roll(x, 16, 1) * sin_b


def _attn_kernel(*refs, seq_len, latent):
    if latent:
        (q_ref, kv_ref, ck_ref, cv_ref, qkg_ref, cos_ref, sa_ref, sb_ref, o_ref, k_s, v_s) = refs
    else:
        (q_ref, kv_ref, qkg_ref, o_ref, kn_ref, vn_ref, k_s, v_s) = refs
    j = pl.program_id(1)
    ones_g = _group_ones(LANES, HEAD_DIM)
    gq = qkg_ref[0, 0:1, :]
    gk = qkg_ref[0, 1:2, :]
    past = PAST_LEN if latent else 0

    @pl.when(j == 0)
    def _():
        ka = kv_ref[:, 0:A_KV_WIDTH]
        va = kv_ref[:, A_KV_WIDTH:2 * A_KV_WIDTH]
        kn = _head_norm(ka, gk, ones_g)
        if latent:
            kr = _rope(kn, cos_ref[...], sa_ref[...], sb_ref[...])
            ck = ck_ref[...]
            cv = cv_ref[...]
            for g in range(A_KV_HEADS):
                cols = slice(g * HEAD_DIM, (g + 1) * HEAD_DIM)
                k_s[g, 0:past, :] = ck[:, cols].astype(BF16)
                v_s[g, 0:past, :] = cv[:, cols].astype(BF16)
                k_s[g, past:past + seq_len, :] = kr[:, cols].astype(BF16)
                v_s[g, past:past + seq_len, :] = va[:, cols].astype(BF16)
        else:
            kn_ref[...] = kn
            vn_ref[...] = va
            for g in range(A_KV_HEADS):
                cols = slice(g * HEAD_DIM, (g + 1) * HEAD_DIM)
                k_s[g] = kn[:, cols].astype(BF16)
                v_s[g] = va[:, cols].astype(BF16)

    if latent:
        rows = pl.ds(pl.multiple_of(j * ATT_TQ, ATT_TQ), ATT_TQ)
        cos = cos_ref[rows, :]
        sin_a = sa_ref[rows, :]
        sin_b = sb_ref[rows, :]
    outs = []
    for t in range(A_WIDTH // LANES):
        qt = _head_norm(q_ref[:, t * LANES:(t + 1) * LANES], gq, ones_g)
        if latent:
            qt = _rope(qt, cos, sin_a, sin_b)
        qt = (qt * (HEAD_DIM ** -0.5)).astype(BF16)
        for hh in range(LANES // HEAD_DIM):
            h = t * (LANES // HEAD_DIM) + hh
            g = h // A_REP
            qh = qt[:, hh * HEAD_DIM:(hh + 1) * HEAD_DIM]
            s = lax.dot_general(qh, k_s[g], (((1,), (1,)), ((), ())), preferred_element_type=F32)
            m = jnp.max(s, axis=-1, keepdims=True)
            p = jnp.exp(s - m)
            l = jnp.sum(p, axis=-1, keepdims=True)
            o = jnp.dot(p.astype(BF16), v_s[g], preferred_element_type=F32)
            outs.append(o / l)
    o_ref[...] = jnp.concatenate(outs, axis=-1)


def _attention(za, qk_gain2, layer, latent, cache_k=None, cache_v=None, rope=None):
    seq_len = DEC_SEQ if latent else SEQ
    n_seq = DEC_BATCH if latent else BATCH
    row0 = N_CTX_ROWS // seq_len if latent else 0
    nq = seq_len // ATT_TQ
    row0q = N_CTX_ROWS // ATT_TQ if latent else 0
    s_len = seq_len + (PAST_LEN if latent else 0)
    in_specs = [pl.BlockSpec((ATT_TQ, A_WIDTH), lambda s, j: (row0q + s * nq + j, 0)),
                pl.BlockSpec((seq_len, 2 * A_KV_WIDTH), lambda s, j: (row0 + s, 2))]
    args = [za, za]
    if latent:
        in_specs += [pl.BlockSpec((None, None, PAST_LEN, A_KV_WIDTH), lambda s, j: (s, layer, 0, 0)),
                     pl.BlockSpec((None, None, PAST_LEN, A_KV_WIDTH), lambda s, j: (s, layer, 0, 0))]
        args += [cache_k, cache_v]
    in_specs.append(pl.BlockSpec((1, 2, LANES), lambda s, j: (layer, 0, 0)))
    args.append(qk_gain2)
    if latent:
        in_specs += [pl.BlockSpec((seq_len, LANES), lambda s, j: (0, 0))] * 3
        args += list(rope)
    out_specs = [pl.BlockSpec((ATT_TQ, A_WIDTH), lambda s, j: (s * nq + j, 0))]
    out_shape = [jax.ShapeDtypeStruct((n_seq * seq_len, A_WIDTH), F32)]
    if not latent:
        out_specs += [pl.BlockSpec((None, seq_len, A_KV_WIDTH), lambda s, j: (s, 0, 0))] * 2
        out_shape += [jax.ShapeDtypeStruct((n_seq, seq_len, A_KV_WIDTH), F32)] * 2
    return pl.pallas_call(
        functools.partial(_attn_kernel, seq_len=seq_len, latent=latent),
        grid=(n_seq, nq),
        in_specs=in_specs,
        out_specs=out_specs,
        out_shape=out_shape,
        scratch_shapes=[pltpu.VMEM((A_KV_HEADS, s_len, HEAD_DIM), BF16),
                        pltpu.VMEM((A_KV_HEADS, s_len, HEAD_DIM), BF16)],
        compiler_params=pltpu.CompilerParams(
            dimension_semantics=("arbitrary", "arbitrary"), vmem_limit_bytes=VMEM_LIMIT),
        name="attn_latent" if latent else "attn_ctx",
    )(*args)


def _gla_kernel(*refs, seq_len, latent):
    if latent:
        zb_ref, wg_ref, bg_ref, gn_ref, s0_ref, ob_ref = refs
    else:
        zb_ref, wg_ref, bg_ref, gn_ref, ob_ref, sfin_ref = refs
    nc = seq_len // CHUNK
    q = zb_ref[:, 0:B_QK_WIDTH] * (B_DK ** -0.5)
    k = zb_ref[:, B_QK_WIDTH:2 * B_QK_WIDTH]
    v = zb_ref[:, 2 * B_QK_WIDTH:2 * B_QK_WIDTH + B_WIDTH]
    small = zb_ref[:, SMALL_OFF:SMALL_OFF + LANES]
    pre = _dot3(small, wg_ref[0]) + bg_ref[0]
    glog = (jnp.minimum(pre, 0.0) - jnp.log1p(jnp.exp(-jnp.abs(pre)))) * (1.0 / GATE_TAU)

    kr = lax.broadcasted_iota(jnp.int32, (B_QK_WIDTH, B_WIDTH), 0) // B_DK
    vc = lax.broadcasted_iota(jnp.int32, (B_QK_WIDTH, B_WIDTH), 1) // B_DV
    bd_kv = kr == vc
    sr = lax.broadcasted_iota(jnp.int32, (B_WIDTH, B_WIDTH), 0) // CHUNK
    sc = lax.broadcasted_iota(jnp.int32, (B_WIDTH, B_WIDTH), 1) // B_DV
    bd_sv = sr == sc
    tt = lax.broadcasted_iota(jnp.int32, (CHUNK, B_WIDTH), 0)
    ss = lax.broadcasted_iota(jnp.int32, (CHUNK, B_WIDTH), 1) % CHUNK

    q3 = q.reshape(nc, CHUNK, B_QK_WIDTH)
    k3 = k.reshape(nc, CHUNK, B_QK_WIDTH)
    o_dirs = []
    for d in range(N_DIR):
        g = glog[:, d * B_QK_WIDTH:(d + 1) * B_QK_WIDTH]
        b = _dot_lhs_exact(_chunk_tri(seq_len, d == 1), g)
        b3 = b.reshape(nc, CHUNK, B_QK_WIDTH)
        mid = b3[:, CHUNK // 2:CHUNK // 2 + 1, :]
        last = b3[:, 0:1, :] if d == 1 else b3[:, CHUNK - 1:CHUNK, :]
        q_in = q3 * jnp.exp(b3 - mid)
        k_in = k3 * jnp.exp(mid - b3)
        q_st = q3 * jnp.exp(b3)
        k_st = k3 * jnp.exp(last - b3)
        causal = (ss >= tt) if d == 1 else (ss <= tt)
        state = s0_ref[d] if latent else jnp.zeros((B_QK_WIDTH, B_WIDTH), F32)
        o_chunks = [None] * nc
        order = range(nc - 1, -1, -1) if d == 1 else range(nc)
        for c in order:
            rows = slice(c * CHUNK, (c + 1) * CHUNK)
            vc_ = v[rows, :]
            v_bd = jnp.where(bd_sv, jnp.concatenate([vc_] * B_HEADS, axis=0), 0.0)
            k_in_t = k_in[c].T
            k_bd = jnp.where(bd_kv, jnp.concatenate([k_in_t] * B_HEADS, axis=1), 0.0)
            att = jnp.where(causal, _bdot(q_in[c], k_bd), 0.0)
            o_chunks[c] = _bdot(att, v_bd) + _bdot(q_st[c], state)
            decay = jnp.exp(b3[c].T[:, (0 if d == 1 else CHUNK - 1):(1 if d == 1 else CHUNK)])
            upd = jnp.where(bd_kv, _bdot_tn(k_st[c], vc_), 0.0)
            state = decay * state + upd
        o_dirs.append(jnp.concatenate(o_chunks, axis=0))
        if not latent:
            sfin_ref[d] = state
    o = o_dirs[0] + o_dirs[1]
    ms = _dot_rhs_exact(o * o, _group_ones(B_WIDTH, B_DV)) * (1.0 / B_DV)
    r = zb_ref[:, 2 * B_QK_WIDTH + B_WIDTH:2 * B_QK_WIDTH + 2 * B_WIDTH]
    ob_ref[...] = o * lax.rsqrt(ms + EPS) * gn_ref[0] * _silu(r)


def _gla(zb, wg_p, bg_p, gn_p, layer, latent, s0_bd=None):
    seq_len = DEC_SEQ if latent else SEQ
    n_seq = DEC_BATCH if latent else BATCH
    row0 = N_CTX_ROWS // seq_len if latent else 0
    in_specs = [pl.BlockSpec((seq_len, ZB_W), lambda s: (row0 + s, 0)),
                pl.BlockSpec((1, LANES, N_DIR * B_QK_WIDTH), lambda s: (layer, 0, 0)),
                pl.BlockSpec((1, 1, N_DIR * B_QK_WIDTH), lambda s: (layer, 0, 0)),
                pl.BlockSpec((1, 1, B_WIDTH), lambda s: (layer, 0, 0))]
    args = [zb, wg_p, bg_p, gn_p]
    out_specs = [pl.BlockSpec((seq_len, B_WIDTH), lambda s: (s, 0))]
    out_shape = [jax.ShapeDtypeStruct((n_seq * seq_len, B_WIDTH), F32)]
    if latent:
        in_specs.append(pl.BlockSpec((None, None, N_DIR, B_QK_WIDTH, B_WIDTH), lambda s: (s, layer, 0, 0, 0)))
        args.append(s0_bd)
    else:
        out_specs.append(pl.BlockSpec((None, N_DIR, B_QK_WIDTH, B_WIDTH), lambda s: (s, 0, 0, 0)))
        out_shape.append(jax.ShapeDtypeStruct((n_seq, N_DIR, B_QK_WIDTH, B_WIDTH), F32))
    return pl.pallas_call(
        functools.partial(_gla_kernel, seq_len=seq_len, latent=latent),
        grid=(n_seq,),
        in_specs=in_specs,
        out_specs=out_specs,
        out_shape=out_shape,
        compiler_params=pltpu.CompilerParams(vmem_limit_bytes=VMEM_LIMIT),
        name="gla_latent" if latent else "gla_ctx",
    )(*args)


TRI_BASE = 8
N_SYS = N_DIR * C_HEADS


def _bmm(a, b):
    return jnp.einsum('bij,bjk->bik', a.astype(BF16), b.astype(BF16), preferred_element_type=F32)


def _bmm_nt(a, b):
    return jnp.einsum('bik,bjk->bij', a.astype(BF16), b.astype(BF16), preferred_element_type=F32)


def _bmm3(a, b):
    ah, al = _split(a)
    bh, bl = _split(b)
    mm = functools.partial(jnp.einsum, 'bij,bjk->bik', preferred_element_type=F32)
    return mm(ah, bh) + mm(ah, bl) + mm(al, bh)


def _unit_tri_inverse(m, rr, cc):
    pw = jnp.where((rr // TRI_BASE) == (cc // TRI_BASE), -m, 0.0)
    inv = (rr == cc).astype(F32) + pw
    span = 2
    while span < TRI_BASE:
        pw = _bmm(pw, pw)
        inv = inv + _bmm(inv, pw)
        span *= 2
    size = 2 * TRI_BASE
    while size <= CHUNK:
        off = ((rr // size) == (cc // size)) & ((rr // (size // 2)) != (cc // (size // 2)))
        inv = inv - _bmm(inv, _bmm(jnp.where(off, m, 0.0), inv))
        size *= 2
    return inv


def _delta_kernel(*refs, seq_len, latent):
    if latent:
        (zc_ref, sm_ref, cw_ref, al_ref, dtb_ref, dn_ref, s0_ref, oc_ref,
         q_s, k_s, v_s, beta_s, gam_s, o_s, u_s, wq_s, att_s, kdt_s, eg_s) = refs
    else:
        (zc_ref, sm_ref, cw_ref, al_ref, dtb_ref, dn_ref, oc_ref, sfin_ref,
         q_s, k_s, v_s, beta_s, gam_s, o_s, u_s, wq_s, att_s, kdt_s, eg_s) = refs
    nc = seq_len // CHUNK

    x = zc_ref[:, 0:CONV_CH]
    t_idx = lax.broadcasted_iota(jnp.int32, (seq_len, 1), 0)
    y = x * cw_ref[0, CONV_WIDTH // 2:CONV_WIDTH // 2 + 1, :]
    for tap in range(CONV_WIDTH):
        off = tap - CONV_WIDTH // 2
        if off == 0:
            continue
        shifted = pltpu.roll(x, (-off) % seq_len, 0)
        valid = (t_idx + off >= 0) & (t_idx + off < seq_len)
        y = y + jnp.where(valid, shifted, 0.0) * cw_ref[0, tap:tap + 1, :]
    y = _silu(y)
    ones_g = _group_ones(C_QK_WIDTH, C_DK)
    qc = y[:, 0:C_QK_WIDTH]
    kc = y[:, C_QK_WIDTH:2 * C_QK_WIDTH]
    q_s[...] = qc * lax.rsqrt(_dot_rhs_exact(qc * qc, ones_g) + EPS) * (C_DK ** -0.5)
    k_s[...] = kc * lax.rsqrt(_dot_rhs_exact(kc * kc, ones_g) + EPS)
    v_s[...] = y[:, 2 * C_QK_WIDTH:CONV_CH]

    small = sm_ref[...]
    beta_s[...] = _sigmoid(small)
    glog = -jnp.exp(al_ref[0]) * _softplus(small + dtb_ref[0])
    for d in range(N_DIR):
        gam_s[d] = _dot_lhs_exact(_chunk_tri(seq_len, d == 1), glog)

    rr = lax.broadcasted_iota(jnp.int32, (N_SYS, CHUNK, CHUNK), 1)
    cc = lax.broadcasted_iota(jnp.int32, (N_SYS, CHUNK, CHUNK), 2)
    rev = lax.broadcasted_iota(jnp.int32, (N_SYS, CHUNK, CHUNK), 0) >= C_HEADS
    ahead = jnp.where(rev, rr - cc, cc - rr)
    incl = ahead <= 0
    strict = ahead < 0

    def prepare_chunk(c, carry):
        rows = pl.ds(pl.multiple_of(c * CHUNK, CHUNK), CHUNK)
        q_blk = q_s[rows, :]
        k_blk = k_s[rows, :]
        v_blk = v_s[rows, :]
        k_blk_t = k_blk.T
        beta_blk = beta_s[rows, :]
        qs, ks, kts, vbs, bcols, gcols, grows, glasts = [], [], [], [], [], [], [], []
        for d in range(N_DIR):
            gam_blk = gam_s[d, rows, :]
            gam_t = gam_blk.T
            last = 0 if d == 1 else CHUNK - 1
            for h in range(C_HEADS):
                cols = slice(h * C_DK, (h + 1) * C_DK)
                lane = d * C_HEADS + h
                bcol = beta_blk[:, BETA_LANE + lane:BETA_LANE + lane + 1]
                gcol = gam_blk[:, DECAY_LANE + lane:DECAY_LANE + lane + 1]
                qs.append(q_blk[:, cols])
                ks.append(k_blk[:, cols])
                kts.append(k_blk_t[cols, :])
                vbs.append(v_blk[:, cols] * bcol)
                bcols.append(bcol)
                gcols.append(gcol)
                grows.append(gam_t[DECAY_LANE + lane:DECAY_LANE + lane + 1, :])
                glasts.append(gcol[last:last + 1, :])
        q8, k8, kt8, vb8 = (jnp.stack(a) for a in (qs, ks, kts, vbs))
        bcol8, gcol8, grow8, glast8 = (jnp.stack(a) for a in (bcols, gcols, grows, glasts))
        dec = jnp.exp(jnp.where(incl, gcol8 - grow8, -jnp.inf))
        kb8 = k8 * bcol8
        m = jnp.where(strict, _bmm_nt(kb8, k8) * dec, 0.0)
        rhs = jnp.concatenate([vb8, kb8 * jnp.exp(gcol8)], axis=-1)
        inv = _unit_tri_inverse(m, rr, cc)
        sol = _bmm(inv, rhs)
        sol = sol + _bmm(inv, rhs - sol - _bmm3(m, sol))
        u_s[c] = sol[:, :, 0:C_DV]
        wq_s[c] = jnp.concatenate([sol[:, :, C_DV:2 * C_DV], q8 * jnp.exp(gcol8)], axis=1).astype(BF16)
        att_s[c] = (_bmm_nt(q8, k8) * dec).astype(BF16)
        kdt_s[c] = (kt8 * jnp.exp(glast8 - grow8)).astype(BF16)
        eg_s[c] = jnp.broadcast_to(jnp.exp(glast8), (N_SYS, 1, LANES))
        return carry

    lax.fori_loop(0, nc, prepare_chunk, 0)

    def scan_chunk(i, state):
        ib = nc - 1 - i

        def both(ref):
            return jnp.concatenate([ref[i, 0:C_HEADS], ref[ib, C_HEADS:N_SYS]], axis=0)

        wq_state = _bmm(both(wq_s), state)
        v_new = both(u_s) - wq_state[:, 0:CHUNK, :]
        o = wq_state[:, CHUNK:2 * CHUNK, :] + _bmm(both(att_s), v_new)
        state = both(eg_s)[:, :, 0:1] * state + _bmm(both(kdt_s), v_new)
        for d in range(N_DIR):
            rows = pl.ds(pl.multiple_of((ib if d == 1 else i) * CHUNK, CHUNK), CHUNK)
            o_s[d, rows, :] = jnp.concatenate([o[d * C_HEADS + h] for h in range(C_HEADS)], axis=-1)
        return state

    if latent:
        state0 = s0_ref[...].reshape(N_SYS, C_DK, C_DV)
    else:
        state0 = jnp.zeros((N_SYS, C_DK, C_DV), F32)
    state = lax.fori_loop(0, nc, scan_chunk, state0)

    o = o_s[0] + o_s[1]
    ms = _dot_rhs_exact(o * o, _group_ones(C_WIDTH, C_DV)) * (1.0 / C_DV)
    gate = zc_ref[:, CONV_CH:CONV_CH + C_WIDTH]
    oc_ref[...] = o * lax.rsqrt(ms + EPS) * dn_ref[0] * _silu(gate)
    if not latent:
        sfin_ref[...] = state.reshape(N_DIR, C_HEADS, C_DK, C_DV)


def _delta(zc, zb, conv_w, al_p, dtb_p, dn_p, layer, latent, state_delta=None):
    seq_len = DEC_SEQ if latent else SEQ
    n_seq = DEC_BATCH if latent else BATCH
    row0 = N_CTX_ROWS // seq_len if latent else 0
    nc = seq_len // CHUNK
    in_specs = [pl.BlockSpec((seq_len, ZC_W), lambda s: (row0 + s, 0)),
                pl.BlockSpec((seq_len, LANES), lambda s: (row0 + s, SMALL_OFF // LANES)),
                pl.BlockSpec((1, CONV_WIDTH, CONV_CH), lambda s: (layer, 0, 0)),
                pl.BlockSpec((1, 1, LANES), lambda s: (layer, 0, 0)),
                pl.BlockSpec((1, 1, LANES), lambda s: (layer, 0, 0)),
                pl.BlockSpec((1, 1, C_WIDTH), lambda s: (layer, 0, 0))]
    args = [zc, zb, conv_w, al_p, dtb_p, dn_p]
    out_specs = [pl.BlockSpec((seq_len, C_WIDTH), lambda s: (s, 0))]
    out_shape = [jax.ShapeDtypeStruct((n_seq * seq_len, C_WIDTH), F32)]
    if latent:
        in_specs.append(pl.BlockSpec((None, None, N_DIR, C_HEADS, C_DK, C_DV), lambda s: (s, layer, 0, 0, 0, 0)))
        args.append(state_delta)
    else:
        out_specs.append(pl.BlockSpec((None, N_DIR, C_HEADS, C_DK, C_DV), lambda s: (s, 0, 0, 0, 0)))
        out_shape.append(jax.ShapeDtypeStruct((n_seq, N_DIR, C_HEADS, C_DK, C_DV), F32))
    return pl.pallas_call(
        functools.partial(_delta_kernel, seq_len=seq_len, latent=latent),
        grid=(n_seq,),
        in_specs=in_specs,
        out_specs=out_specs,
        out_shape=out_shape,
        scratch_shapes=[pltpu.VMEM((seq_len, C_QK_WIDTH), F32),
                        pltpu.VMEM((seq_len, C_QK_WIDTH), F32),
                        pltpu.VMEM((seq_len, C_WIDTH), F32),
                        pltpu.VMEM((seq_len, LANES), F32),
                        pltpu.VMEM((N_DIR, seq_len, LANES), F32),
                        pltpu.VMEM((N_DIR, seq_len, C_WIDTH), F32),
                        pltpu.VMEM((nc, N_SYS, CHUNK, C_DV), F32),
                        pltpu.VMEM((nc, N_SYS, 2 * CHUNK, C_DK), BF16),
                        pltpu.VMEM((nc, N_SYS, CHUNK, CHUNK), BF16),
                        pltpu.VMEM((nc, N_SYS, C_DK, CHUNK), BF16),
                        pltpu.VMEM((nc, N_SYS, 1, LANES), F32)],
        compiler_params=pltpu.CompilerParams(vmem_limit_bytes=VMEM_LIMIT),
        name="delta_latent" if latent else "delta_ctx",
    )(*args)


def _rope_tables():
    rows = DEC_SEQ // GRID_W
    row = jnp.repeat(jnp.arange(rows, dtype=F32), GRID_W)
    col = jnp.tile(jnp.arange(GRID_W, dtype=F32), rows)
    n_freq = HEAD_DIM // 4
    inv_freq = ROPE_THETA ** (-jnp.arange(n_freq, dtype=F32) / n_freq)
    ang_r = row[:, None] * inv_freq
    ang_c = col[:, None] * inv_freq
    ang = jnp.concatenate([ang_r, ang_r, ang_c, ang_c], axis=-1)
    cos, sin = jnp.cos(ang), jnp.sin(ang)
    first = (jnp.arange(HEAD_DIM) % 32) < 16
    sin_a = jnp.where(first, -sin, 0.0)
    sin_b = jnp.where(first, 0.0, sin)
    reps = LANES // HEAD_DIM
    return tuple(jnp.tile(t, (1, reps)) for t in (cos, sin_a, sin_b))


def _block_diag_state(s):
    eye = jnp.eye(B_HEADS, dtype=s.dtype)
    out = jnp.einsum('...hkv,hg->...hkgv', s, eye)
    return out.reshape(s.shape[:-3] + (B_QK_WIDTH, B_WIDTH))


def _diag_blocks(s_bd):
    s5 = s_bd.reshape(s_bd.shape[:-2] + (B_HEADS, B_DK, B_HEADS, B_DV))
    return jnp.stack([s5[..., h, :, h, :] for h in range(B_HEADS)], axis=-3)


def kernel(x_prompt, x_sample, cache_k, cache_v, state_gla, state_delta, c, c_ctx, w_mod, b_mod, norm_gains, w_in, qk_gain, w_gla_gate, b_gla_gate, gla_norm, conv_w, a_log, dt_bias, delta_norm, w_out, w_gate, w_up, w_down):
    x = jnp.concatenate([x_prompt.reshape(N_CTX_ROWS, D_MODEL), x_sample.reshape(N_LAT_ROWS, D_MODEL)], axis=0)
    cond = jnp.concatenate([c_ctx[None, :], c, jnp.zeros((N_COND - 1 - DEC_BATCH, D_MODEL), F32)], axis=0)
    nat_small = A_WIDTH + 2 * A_KV_WIDTH + 2 * B_QK_WIDTH + 2 * B_WIDTH
    nat_c = nat_small + N_DIR * GATE_RANK
    nat_bc = nat_c + ZC_W
    w_in_p = jnp.concatenate(
        [w_in[:, :, 0:nat_small], w_in[:, :, nat_small:nat_c], w_in[:, :, nat_bc:nat_bc + 2 * N_DIR * C_HEADS],
         jnp.zeros((DEPTH, D_MODEL, LANES - N_DIR * GATE_RANK - 2 * N_DIR * C_HEADS), w_in.dtype),
         w_in[:, :, nat_c:nat_bc]], axis=-1).astype(BF16)
    w_out_b = w_out.astype(BF16)
    w_gate_b = w_gate.astype(BF16)
    w_up_b = w_up.astype(BF16)
    w_down_b = w_down.astype(BF16)
    qk_gain2 = jnp.tile(qk_gain, (1, 1, LANES // HEAD_DIM))
    wg_p = jnp.zeros((DEPTH, LANES, N_DIR * B_QK_WIDTH), F32)
    for d in range(N_DIR):
        wg_p = wg_p.at[:, d * GATE_RANK:(d + 1) * GATE_RANK, d * B_QK_WIDTH:(d + 1) * B_QK_WIDTH].set(w_gla_gate[:, d])
    bg_p = b_gla_gate.reshape(DEPTH, 1, N_DIR * B_QK_WIDTH)
    gn_p = jnp.tile(gla_norm, (1, B_HEADS)).reshape(DEPTH, 1, B_WIDTH)
    dn_p = jnp.tile(delta_norm, (1, C_HEADS)).reshape(DEPTH, 1, C_WIDTH)
    al_p = jnp.zeros((DEPTH, 1, LANES), F32).at[:, 0, DECAY_LANE:DECAY_LANE + N_DIR * C_HEADS].set(
        a_log.reshape(DEPTH, N_DIR * C_HEADS))
    dtb_p = jnp.zeros((DEPTH, 1, LANES), F32).at[:, 0, DECAY_LANE:DECAY_LANE + N_DIR * C_HEADS].set(
        dt_bias.reshape(DEPTH, N_DIR * C_HEADS))
    cache_k2 = cache_k.reshape(DEC_BATCH, DEPTH, PAST_LEN, A_KV_WIDTH)
    cache_v2 = cache_v.reshape(DEC_BATCH, DEPTH, PAST_LEN, A_KV_WIDTH)
    s0_gla_bd = _block_diag_state(state_gla.astype(F32))
    s0_delta = state_delta.astype(F32)
    rope = _rope_tables()

    mods = _modulation(cond, w_mod, b_mod).reshape(DEPTH * N_COND, 1, 6 * D_MODEL)

    new_k, new_v, new_gla, new_delta = [], [], [], []
    for l in range(DEPTH):
        za, zb, zc = _in_projection(x, mods, norm_gains, w_in_p, l)
        oa_c, k_l, v_l = _attention(za, qk_gain2, l, False)
        (oa_l,) = _attention(za, qk_gain2, l, True, cache_k2, cache_v2, rope)
        ob_c, sg_l = _gla(zb, wg_p, bg_p, gn_p, l, False)
        (ob_l,) = _gla(zb, wg_p, bg_p, gn_p, l, True, s0_gla_bd)
        oc_c, sd_l = _delta(zc, zb, conv_w, al_p, dtb_p, dn_p, l, False)
        (oc_l,) = _delta(zc, zb, conv_w, al_p, dtb_p, dn_p, l, True, s0_delta)
        oa = jnp.concatenate([oa_c, oa_l], axis=0)
        ob = jnp.concatenate([ob_c, ob_l], axis=0)
        oc = jnp.concatenate([oc_c, oc_l], axis=0)
        x = _out_ffn(x, oa, ob, oc, mods, norm_gains, w_out_b, w_gate_b, w_up_b, w_down_b, l)
        new_k.append(k_l.reshape(BATCH, SEQ, A_KV_HEADS, HEAD_DIM))
        new_v.append(v_l.reshape(BATCH, SEQ, A_KV_HEADS, HEAD_DIM))
        new_gla.append(_diag_blocks(sg_l))
        new_delta.append(sd_l)

    out_dtype = x_prompt.dtype
    y_prompt = x[:N_CTX_ROWS].reshape(BATCH, SEQ, D_MODEL)
    y_sample = x[N_CTX_ROWS:].reshape(DEC_BATCH, DEC_SEQ, D_MODEL)
    return (y_prompt, y_sample,
            jnp.stack(new_k, axis=1), jnp.stack(new_v, axis=1),
            jnp.stack(new_gla, axis=1).astype(out_dtype),
            jnp.stack(new_delta, axis=1).astype(out_dtype))
```

```python
import functools

import jax
import jax.numpy as jnp
import numpy as np
from jax import lax
from jax.experimental import pallas as pl
from jax.experimental.pallas import tpu as pltpu

F32 = jnp.float32
BF16 = jnp.bfloat16

D_MODEL = 1024
BATCH = 16
SEQ = 256
DEPTH = 4
DEC_BATCH = 4
DEC_SEQ = 1024
PAST_LEN = 512
GRID_W = 64
HEAD_DIM = 64
A_HEADS = 8
A_KV_HEADS = 2
A_REP = A_HEADS // A_KV_HEADS
A_WIDTH = A_HEADS * HEAD_DIM
A_KV_WIDTH = A_KV_HEADS * HEAD_DIM
ROPE_THETA = 10000.0
B_HEADS = 4
B_DK = 32
B_DV = 64
B_QK_WIDTH = B_HEADS * B_DK
B_WIDTH = B_HEADS * B_DV
GATE_RANK = 16
GATE_TAU = 16.0
C_HEADS = 4
C_DK = 64
C_DV = 64
C_QK_WIDTH = C_HEADS * C_DK
C_WIDTH = C_HEADS * C_DV
CONV_WIDTH = 5
CONV_CH = 2 * C_QK_WIDTH + C_WIDTH
CHUNK = 64
N_DIR = 2
MIX_WIDTH = A_WIDTH + B_WIDTH + C_WIDTH
D_FF = -(-(8 * D_MODEL) // (3 * 256)) * 256
EPS = 1e-6

LANES = 128
N_CTX_ROWS = BATCH * SEQ
N_LAT_ROWS = DEC_BATCH * DEC_SEQ
N_ROWS = N_CTX_ROWS + N_LAT_ROWS
N_COND = 8

ZA_W = A_WIDTH + 2 * A_KV_WIDTH
ZB_W = 2 * B_QK_WIDTH + 2 * B_WIDTH + LANES
ZC_W = 2 * C_QK_WIDTH + 2 * C_WIDTH
PROJ_PAD = ZA_W + ZB_W + ZC_W
SMALL_OFF = 2 * B_QK_WIDTH + 2 * B_WIDTH
BETA_LANE = N_DIR * GATE_RANK
DECAY_LANE = BETA_LANE + N_DIR * C_HEADS

VMEM_LIMIT = 56 * 1024 * 1024


def _split(x):
    hi = x.astype(BF16)
    lo = (x - hi.astype(F32)).astype(BF16)
    return hi, lo


def _bdot(a, b):
    return jnp.dot(a.astype(BF16), b.astype(BF16), preferred_element_type=F32)


def _bdot_nt(a, b):
    return lax.dot_general(a.astype(BF16), b.astype(BF16), (((1,), (1,)), ((), ())),
                           preferred_element_type=F32)


def _bdot_tn(a, b):
    return lax.dot_general(a.astype(BF16), b.astype(BF16), (((0,), (0,)), ((), ())),
                           preferred_element_type=F32)


def _dot3(a, b):
    ah, al = _split(a)
    bh, bl = _split(b)
    return (jnp.dot(ah, bh, preferred_element_type=F32)
            + jnp.dot(ah, bl, preferred_element_type=F32)
            + jnp.dot(al, bh, preferred_element_type=F32))


def _dot_lhs_exact(m_bf16, x):
    xh, xl = _split(x)
    return (jnp.dot(m_bf16, xh, preferred_element_type=F32)
            + jnp.dot(m_bf16, xl, preferred_element_type=F32))


def _dot_rhs_exact(x, m_bf16):
    xh, xl = _split(x)
    return (jnp.dot(xh, m_bf16, preferred_element_type=F32)
            + jnp.dot(xl, m_bf16, preferred_element_type=F32))


def _group_ones(width, group):
    r = lax.broadcasted_iota(jnp.int32, (width, width), 0) // group
    c = lax.broadcasted_iota(jnp.int32, (width, width), 1) // group
    return (r == c).astype(BF16)


def _sigmoid(x):
    return 1.0 / (1.0 + jnp.exp(-x))


def _silu(x):
    return x * _sigmoid(x)


def _softplus(x):
    return jnp.maximum(x, 0.0) + jnp.log1p(jnp.exp(-jnp.abs(x)))


def _rms(x, g):
    return x * lax.rsqrt(jnp.mean(x * x, axis=-1, keepdims=True) + EPS) * g


def _chunk_tri(t, reverse):
    r = lax.broadcasted_iota(jnp.int32, (t, t), 0)
    c = lax.broadcasted_iota(jnp.int32, (t, t), 1)
    same = (r // CHUNK) == (c // CHUNK)
    tri = (c >= r) if reverse else (c <= r)
    return (same & tri).astype(BF16)


def _mod_row(i, tm):
    start = i * tm
    return jnp.where(start < N_CTX_ROWS, 0, 1 + (start - N_CTX_ROWS) // DEC_SEQ)


def _mod_kernel(cond_ref, w_ref, b_ref, o_ref):
    c = cond_ref[...]
    o_ref[0] = _bdot(_silu(c), w_ref[0]) + b_ref[0]


def _modulation(cond, w_mod, b_mod):
    tn = 1536
    return pl.pallas_call(
        _mod_kernel,
        grid=(DEPTH, 6 * D_MODEL // tn),
        in_specs=[pl.BlockSpec((N_COND, D_MODEL), lambda l, j: (0, 0)),
                  pl.BlockSpec((1, D_MODEL, tn), lambda l, j: (l, 0, j)),
                  pl.BlockSpec((1, 1, tn), lambda l, j: (l, 0, j))],
        out_specs=pl.BlockSpec((1, N_COND, tn), lambda l, j: (l, 0, j)),
        out_shape=jax.ShapeDtypeStruct((DEPTH, N_COND, 6 * D_MODEL), F32),
        compiler_params=pltpu.CompilerParams(vmem_limit_bytes=VMEM_LIMIT),
        name="adaln_mod",
    )(cond, w_mod, b_mod.reshape(DEPTH, 1, 6 * D_MODEL))


DENSE_TM = 512
N_CTX_BLOCKS = N_CTX_ROWS // DENSE_TM


def _ctx_map(i):
    return (jnp.minimum(i, N_CTX_BLOCKS - 1), 0)


def _lat_map(i):
    return (jnp.maximum(i - N_CTX_BLOCKS, 0), 0)


def _inproj_kernel(xc_ref, xl_ref, mod_ref, ng_ref, w_ref, za_ref, zb_ref, zc_ref):
    x = jnp.where(pl.program_id(0) < N_CTX_BLOCKS, xc_ref[...], xl_ref[...])
    mod = mod_ref[0]
    shift = mod[:, 0:D_MODEL]
    scale = mod[:, D_MODEL:2 * D_MODEL]
    h = _rms(x, ng_ref[0, 0:1, :]) * (1.0 + scale) + shift
    z = _bdot(h, w_ref[0])
    za_ref[...] = z[:, 0:ZA_W]
    zb_ref[...] = z[:, ZA_W:ZA_W + ZB_W]
    zc_ref[...] = z[:, ZA_W + ZB_W:PROJ_PAD]


def _in_projection(xc, xl, mods, norm_gains, w_in_p, layer):
    tm = DENSE_TM
    return pl.pallas_call(
        _inproj_kernel,
        grid=(N_ROWS // tm,),
        in_specs=[pl.BlockSpec((tm, D_MODEL), _ctx_map),
                  pl.BlockSpec((tm, D_MODEL), _lat_map),
                  pl.BlockSpec((1, 1, 6 * D_MODEL), lambda i: (layer * N_COND + _mod_row(i, tm), 0, 0)),
                  pl.BlockSpec((1, 4, D_MODEL), lambda i: (layer, 0, 0)),
                  pl.BlockSpec((1, D_MODEL, PROJ_PAD), lambda i: (layer, 0, 0))],
        out_specs=[pl.BlockSpec((tm, ZA_W), lambda i: (i, 0)),
                   pl.BlockSpec((tm, ZB_W), lambda i: (i, 0)),
                   pl.BlockSpec((tm, ZC_W), lambda i: (i, 0))],
        out_shape=[jax.ShapeDtypeStruct((N_ROWS, ZA_W), F32),
                   jax.ShapeDtypeStruct((N_ROWS, ZB_W), F32),
                   jax.ShapeDtypeStruct((N_ROWS, ZC_W), F32)],
        compiler_params=pltpu.CompilerParams(vmem_limit_bytes=VMEM_LIMIT),
        name="in_proj",
    )(xc, xl, mods, norm_gains, w_in_p)


FF_TILE = 256


def _ffn_kernel(xc_ref, xl_ref, oac_ref, oal_ref, obc_ref, obl_ref, occ_ref, ocl_ref, mod_ref, ng_ref,
                wo_ref, wg_ref, wu_ref, wd_ref, yc_ref, yl_ref):
    is_ctx = pl.program_id(0) < N_CTX_BLOCKS
    x = jnp.where(is_ctx, xc_ref[...], xl_ref[...])
    mod = mod_ref[0]
    gate_m = mod[:, 2 * D_MODEL:3 * D_MODEL]
    shift_f = mod[:, 3 * D_MODEL:4 * D_MODEL]
    scale_f = mod[:, 4 * D_MODEL:5 * D_MODEL]
    gate_f = mod[:, 5 * D_MODEL:6 * D_MODEL]
    mix_in = jnp.concatenate([jnp.where(is_ctx, oac_ref[...], oal_ref[...]),
                              jnp.where(is_ctx, obc_ref[...], obl_ref[...]),
                              jnp.where(is_ctx, occ_ref[...], ocl_ref[...])], axis=-1)
    mix = _bdot(mix_in, wo_ref[0])
    x1 = x + gate_m * _rms(mix, ng_ref[0, 1:2, :])
    h = (_rms(x1, ng_ref[0, 2:3, :]) * (1.0 + scale_f) + shift_f).astype(BF16)
    f = jnp.zeros(x.shape, F32)
    for j in range(D_FF // FF_TILE):
        cols = slice(j * FF_TILE, (j + 1) * FF_TILE)
        g = jnp.dot(h, wg_ref[0, :, cols], preferred_element_type=F32)
        u = jnp.dot(h, wu_ref[0, :, cols], preferred_element_type=F32)
        f = f + _bdot(_silu(g) * u, wd_ref[0, cols, :])
    y = x1 + gate_f * _rms(f, ng_ref[0, 3:4, :])

    @pl.when(is_ctx)
    def _():
        yc_ref[...] = y

    @pl.when(jnp.logical_not(is_ctx))
    def _():
        yl_ref[...] = y


def _out_ffn(xc, xl, mixer_outs, mods, norm_gains, w_out_b, w_gate_b, w_up_b, w_down_b, layer):
    tm = DENSE_TM
    resident = dict(pipeline_mode=pl.Buffered(1))
    pair_specs = []
    for width in (D_MODEL, A_WIDTH, B_WIDTH, C_WIDTH):
        pair_specs += [pl.BlockSpec((tm, width), _ctx_map), pl.BlockSpec((tm, width), _lat_map)]
    return pl.pallas_call(
        _ffn_kernel,
        grid=(N_ROWS // tm,),
        in_specs=pair_specs + [
                  pl.BlockSpec((1, 1, 6 * D_MODEL), lambda i: (layer * N_COND + _mod_row(i, tm), 0, 0)),
                  pl.BlockSpec((1, 4, D_MODEL), lambda i: (layer, 0, 0)),
                  pl.BlockSpec((1, MIX_WIDTH, D_MODEL), lambda i: (layer, 0, 0), **resident),
                  pl.BlockSpec((1, D_MODEL, D_FF), lambda i: (layer, 0, 0), **resident),
                  pl.BlockSpec((1, D_MODEL, D_FF), lambda i: (layer, 0, 0), **resident),
                  pl.BlockSpec((1, D_FF, D_MODEL), lambda i: (layer, 0, 0), **resident)],
        out_specs=[pl.BlockSpec((tm, D_MODEL), _ctx_map), pl.BlockSpec((tm, D_MODEL), _lat_map)],
        out_shape=[jax.ShapeDtypeStruct((N_CTX_ROWS, D_MODEL), F32),
                   jax.ShapeDtypeStruct((N_LAT_ROWS, D_MODEL), F32)],
        compiler_params=pltpu.CompilerParams(
            dimension_semantics=("arbitrary",), vmem_limit_bytes=VMEM_LIMIT),
        name="out_ffn",
    )(xc, xl, *mixer_outs, mods, norm_gains, w_out_b, w_gate_b, w_up_b, w_down_b)


ATT_TQ = 256


def _head_norm(x, gain_row, ones_g):
    ss = _dot_rhs_exact(x * x, ones_g)
    return x * lax.rsqrt(ss * (1.0 / HEAD_DIM) + EPS) * gain_row


def _rope(x, cos, sin_a, sin_b):
    return x * cos + pltpu.roll(x, LANES - 16, 1) * sin_a + pltpu.roll(x, 16, 1) * sin_b


def _attn_kernel(*refs, seq_len, latent):
    if latent:
        (q_ref, kv_ref, ck_ref, cv_ref, qkg_ref, cos_ref, sa_ref, sb_ref, o_ref, k_s, v_s) = refs
    else:
        (q_ref, kv_ref, qkg_ref, o_ref, kn_ref, vn_ref, k_s, v_s) = refs
    j = pl.program_id(1)
    ones_g = _group_ones(LANES, HEAD_DIM)
    gq = qkg_ref[0, 0:1, :]
    gk = qkg_ref[0, 1:2, :]
    past = PAST_LEN if latent else 0

    @pl.when(j == 0)
    def _():
        ka = kv_ref[:, 0:A_KV_WIDTH]
        va = kv_ref[:, A_KV_WIDTH:2 * A_KV_WIDTH]
        kn = _head_norm(ka, gk, ones_g)
        if latent:
            kr = _rope(kn, cos_ref[...], sa_ref[...], sb_ref[...])
            ck = ck_ref[...]
            cv = cv_ref[...]
            for g in range(A_KV_HEADS):
                cols = slice(g * HEAD_DIM, (g + 1) * HEAD_DIM)
                k_s[g, 0:past, :] = ck[:, cols].astype(BF16)
                v_s[g, 0:past, :] = cv[:, cols].astype(BF16)
                k_s[g, past:past + seq_len, :] = kr[:, cols].astype(BF16)
                v_s[g, past:past + seq_len, :] = va[:, cols].astype(BF16)
        else:
            kn_ref[...] = kn
            vn_ref[...] = va
            for g in range(A_KV_HEADS):
                cols = slice(g * HEAD_DIM, (g + 1) * HEAD_DIM)
                k_s[g] = kn[:, cols].astype(BF16)
                v_s[g] = va[:, cols].astype(BF16)

    if latent:
        rows = pl.ds(pl.multiple_of(j * ATT_TQ, ATT_TQ), ATT_TQ)
        cos = cos_ref[rows, :]
        sin_a = sa_ref[rows, :]
        sin_b = sb_ref[rows, :]
    outs = []
    for t in range(A_WIDTH // LANES):
        qt = _head_norm(q_ref[:, t * LANES:(t + 1) * LANES], gq, ones_g)
        if latent:
            qt = _rope(qt, cos, sin_a, sin_b)
        qt = (qt * (HEAD_DIM ** -0.5)).astype(BF16)
        for hh in range(LANES // HEAD_DIM):
            h = t * (LANES // HEAD_DIM) + hh
            g = h // A_REP
            qh = qt[:, hh * HEAD_DIM:(hh + 1) * HEAD_DIM]
            s = lax.dot_general(qh, k_s[g], (((1,), (1,)), ((), ())), preferred_element_type=F32)
            m = jnp.max(s, axis=-1, keepdims=True)
            p = jnp.exp(s - m)
            l = jnp.sum(p, axis=-1, keepdims=True)
            o = jnp.dot(p.astype(BF16), v_s[g], preferred_element_type=F32)
            outs.append(o / l)
    o_ref[...] = jnp.concatenate(outs, axis=-1)


def _attention(za, qk_gain2, layer, latent, cache_k=None, cache_v=None, rope=None):
    seq_len = DEC_SEQ if latent else SEQ
    n_seq = DEC_BATCH if latent else BATCH
    row0 = N_CTX_ROWS // seq_len if latent else 0
    nq = seq_len // ATT_TQ
    row0q = N_CTX_ROWS // ATT_TQ if latent else 0
    s_len = seq_len + (PAST_LEN if latent else 0)
    in_specs = [pl.BlockSpec((ATT_TQ, A_WIDTH), lambda s, j: (row0q + s * nq + j, 0)),
                pl.BlockSpec((seq_len, 2 * A_KV_WIDTH), lambda s, j: (row0 + s, 2))]
    args = [za, za]
    if latent:
        in_specs += [pl.BlockSpec((None, None, PAST_LEN, A_KV_WIDTH), lambda s, j: (s, layer, 0, 0)),
                     pl.BlockSpec((None, None, PAST_LEN, A_KV_WIDTH), lambda s, j: (s, layer, 0, 0))]
        args += [cache_k, cache_v]
    in_specs.append(pl.BlockSpec((1, 2, LANES), lambda s, j: (layer, 0, 0)))
    args.append(qk_gain2)
    if latent:
        in_specs += [pl.BlockSpec((seq_len, LANES), lambda s, j: (0, 0))] * 3
        args += list(rope)
    out_specs = [pl.BlockSpec((ATT_TQ, A_WIDTH), lambda s, j: (s * nq + j, 0))]
    out_shape = [jax.ShapeDtypeStruct((n_seq * seq_len, A_WIDTH), F32)]
    if not latent:
        out_specs += [pl.BlockSpec((None, seq_len, A_KV_WIDTH), lambda s, j: (s, 0, 0))] * 2
        out_shape += [jax.ShapeDtypeStruct((n_seq, seq_len, A_KV_WIDTH), F32)] * 2
    return pl.pallas_call(
        functools.partial(_attn_kernel, seq_len=seq_len, latent=latent),
        grid=(n_seq, nq),
        in_specs=in_specs,
        out_specs=out_specs,
        out_shape=out_shape,
        scratch_shapes=[pltpu.VMEM((A_KV_HEADS, s_len, HEAD_DIM), BF16),
                        pltpu.VMEM((A_KV_HEADS, s_len, HEAD_DIM), BF16)],
        compiler_params=pltpu.CompilerParams(
            dimension_semantics=("arbitrary", "arbitrary"), vmem_limit_bytes=VMEM_LIMIT),
        name="attn_latent" if latent else "attn_ctx",
    )(*args)


def _gla_kernel(*refs, seq_len, latent):
    if latent:
        zb_ref, wg_ref, bg_ref, gn_ref, s0_ref, ob_ref = refs
    else:
        zb_ref, wg_ref, bg_ref, gn_ref, ob_ref, sfin_ref = refs
    nc = seq_len // CHUNK
    q = zb_ref[:, 0:B_QK_WIDTH] * (B_DK ** -0.5)
    k = zb_ref[:, B_QK_WIDTH:2 * B_QK_WIDTH]
    v = zb_ref[:, 2 * B_QK_WIDTH:2 * B_QK_WIDTH + B_WIDTH]
    small = zb_ref[:, SMALL_OFF:SMALL_OFF + LANES]
    pre = _dot3(small, wg_ref[0]) + bg_ref[0]
    glog = (jnp.minimum(pre, 0.0) - jnp.log1p(jnp.exp(-jnp.abs(pre)))) * (1.0 / GATE_TAU)

    kr = lax.broadcasted_iota(jnp.int32, (B_QK_WIDTH, B_WIDTH), 0) // B_DK
    vc = lax.broadcasted_iota(jnp.int32, (B_QK_WIDTH, B_WIDTH), 1) // B_DV
    bd_kv = kr == vc
    sr = lax.broadcasted_iota(jnp.int32, (B_WIDTH, B_WIDTH), 0) // CHUNK
    sc = lax.broadcasted_iota(jnp.int32, (B_WIDTH, B_WIDTH), 1) // B_DV
    bd_sv = sr == sc
    tt = lax.broadcasted_iota(jnp.int32, (CHUNK, B_WIDTH), 0)
    ss = lax.broadcasted_iota(jnp.int32, (CHUNK, B_WIDTH), 1) % CHUNK

    q3 = q.reshape(nc, CHUNK, B_QK_WIDTH)
    k3 = k.reshape(nc, CHUNK, B_QK_WIDTH)
    o_dirs = []
    for d in range(N_DIR):
        g = glog[:, d * B_QK_WIDTH:(d + 1) * B_QK_WIDTH]
        b = _dot_lhs_exact(_chunk_tri(seq_len, d == 1), g)
        b3 = b.reshape(nc, CHUNK, B_QK_WIDTH)
        mid = b3[:, CHUNK // 2:CHUNK // 2 + 1, :]
        last = b3[:, 0:1, :] if d == 1 else b3[:, CHUNK - 1:CHUNK, :]
        q_in = q3 * jnp.exp(b3 - mid)
        k_in = k3 * jnp.exp(mid - b3)
        q_st = q3 * jnp.exp(b3)
        k_st = k3 * jnp.exp(last - b3)
        causal = (ss >= tt) if d == 1 else (ss <= tt)
        state = s0_ref[d] if latent else jnp.zeros((B_QK_WIDTH, B_WIDTH), F32)
        o_chunks = [None] * nc
        order = range(nc - 1, -1, -1) if d == 1 else range(nc)
        for c in order:
            rows = slice(c * CHUNK, (c + 1) * CHUNK)
            vc_ = v[rows, :]
            v_bd = jnp.where(bd_sv, jnp.concatenate([vc_] * B_HEADS, axis=0), 0.0)
            k_in_t = k_in[c].T
            k_bd = jnp.where(bd_kv, jnp.concatenate([k_in_t] * B_HEADS, axis=1), 0.0)
            att = jnp.where(causal, _bdot(q_in[c], k_bd), 0.0)
            o_chunks[c] = _bdot(att, v_bd) + _bdot(q_st[c], state)
            decay = jnp.exp(b3[c].T[:, (0 if d == 1 else CHUNK - 1):(1 if d == 1 else CHUNK)])
            upd = jnp.where(bd_kv, _bdot_tn(k_st[c], vc_), 0.0)
            state = decay * state + upd
        o_dirs.append(jnp.concatenate(o_chunks, axis=0))
        if not latent:
            sfin_ref[d] = state
    o = o_dirs[0] + o_dirs[1]
    ms = _dot_rhs_exact(o * o, _group_ones(B_WIDTH, B_DV)) * (1.0 / B_DV)
    r = zb_ref[:, 2 * B_QK_WIDTH + B_WIDTH:2 * B_QK_WIDTH + 2 * B_WIDTH]
    ob_ref[...] = o * lax.rsqrt(ms + EPS) * gn_ref[0] * _silu(r)


def _gla(zb, wg_p, bg_p, gn_p, layer, latent, s0_bd=None):
    seq_len = DEC_SEQ if latent else SEQ
    n_seq = DEC_BATCH if latent else BATCH
    row0 = N_CTX_ROWS // seq_len if latent else 0
    in_specs = [pl.BlockSpec((seq_len, ZB_W), lambda s: (row0 + s, 0)),
                pl.BlockSpec((1, LANES, N_DIR * B_QK_WIDTH), lambda s: (layer, 0, 0)),
                pl.BlockSpec((1, 1, N_DIR * B_QK_WIDTH), lambda s: (layer, 0, 0)),
                pl.BlockSpec((1, 1, B_WIDTH), lambda s: (layer, 0, 0))]
    args = [zb, wg_p, bg_p, gn_p]
    out_specs = [pl.BlockSpec((seq_len, B_WIDTH), lambda s: (s, 0))]
    out_shape = [jax.ShapeDtypeStruct((n_seq * seq_len, B_WIDTH), F32)]
    if latent:
        in_specs.append(pl.BlockSpec((None, None, N_DIR, B_QK_WIDTH, B_WIDTH), lambda s: (s, layer, 0, 0, 0)))
        args.append(s0_bd)
    else:
        out_specs.append(pl.BlockSpec((None, N_DIR, B_QK_WIDTH, B_WIDTH), lambda s: (s, 0, 0, 0)))
        out_shape.append(jax.ShapeDtypeStruct((n_seq, N_DIR, B_QK_WIDTH, B_WIDTH), F32))
    return pl.pallas_call(
        functools.partial(_gla_kernel, seq_len=seq_len, latent=latent),
        grid=(n_seq,),
        in_specs=in_specs,
        out_specs=out_specs,
        out_shape=out_shape,
        compiler_params=pltpu.CompilerParams(vmem_limit_bytes=VMEM_LIMIT),
        name="gla_latent" if latent else "gla_ctx",
    )(*args)


TRI_BASE = 8
N_SYS = N_DIR * C_HEADS
PREP_CHUNKS = 4


def _bmm(a, b):
    return jnp.einsum('bij,bjk->bik', a.astype(BF16), b.astype(BF16), preferred_element_type=F32)


def _bmm_nt(a, b):
    return jnp.einsum('bik,bjk->bij', a.astype(BF16), b.astype(BF16), preferred_element_type=F32)


def _bmm3(a, b):
    ah, al = _split(a)
    bh, bl = _split(b)
    mm = functools.partial(jnp.einsum, 'bij,bjk->bik', preferred_element_type=F32)
    return mm(ah, bh) + mm(ah, bl) + mm(al, bh)


def _unit_tri_inverse(m, rr, cc):
    pw = jnp.where((rr // TRI_BASE) == (cc // TRI_BASE), -m, 0.0)
    inv = (rr == cc).astype(F32) + pw
    span = 2
    while span < TRI_BASE:
        pw = _bmm(pw, pw)
        inv = inv + _bmm(inv, pw)
        span *= 2
    size = 2 * TRI_BASE
    while size <= CHUNK:
        off = ((rr // size) == (cc // size)) & ((rr // (size // 2)) != (cc // (size // 2)))
        inv = inv - _bmm(inv, _bmm(jnp.where(off, m, 0.0), inv))
        size *= 2
    return inv


def _delta_kernel(*refs, seq_len, latent):
    if latent:
        (zc_ref, sm_ref, cw_ref, al_ref, dtb_ref, dn_ref, s0_ref, oc_ref,
         q_s, k_s, v_s, beta_s, gam_s, o_s, u_s, wq_s, att_s, kdt_s, eg_s) = refs
    else:
        (zc_ref, sm_ref, cw_ref, al_ref, dtb_ref, dn_ref, oc_ref, sfin_ref,
         q_s, k_s, v_s, beta_s, gam_s, o_s, u_s, wq_s, att_s, kdt_s, eg_s) = refs
    nc = seq_len // CHUNK

    x = zc_ref[:, 0:CONV_CH]
    t_idx = lax.broadcasted_iota(jnp.int32, (seq_len, 1), 0)
    y = x * cw_ref[0, CONV_WIDTH // 2:CONV_WIDTH // 2 + 1, :]
    for tap in range(CONV_WIDTH):
        off = tap - CONV_WIDTH // 2
        if off == 0:
            continue
        shifted = pltpu.roll(x, (-off) % seq_len, 0)
        valid = (t_idx + off >= 0) & (t_idx + off < seq_len)
        y = y + jnp.where(valid, shifted, 0.0) * cw_ref[0, tap:tap + 1, :]
    y = _silu(y)
    ones_g = _group_ones(C_QK_WIDTH, C_DK)
    qc = y[:, 0:C_QK_WIDTH]
    kc = y[:, C_QK_WIDTH:2 * C_QK_WIDTH]
    q_s[...] = qc * lax.rsqrt(_dot_rhs_exact(qc * qc, ones_g) + EPS) * (C_DK ** -0.5)
    k_s[...] = kc * lax.rsqrt(_dot_rhs_exact(kc * kc, ones_g) + EPS)
    v_s[...] = y[:, 2 * C_QK_WIDTH:CONV_CH]

    small = sm_ref[...]
    beta_s[...] = _sigmoid(small)
    glog = -jnp.exp(al_ref[0]) * _softplus(small + dtb_ref[0])
    for d in range(N_DIR):
        gam_s[d] = _dot_lhs_exact(_chunk_tri(seq_len, d == 1), glog)

    nb = PREP_CHUNKS * N_SYS
    rr = lax.broadcasted_iota(jnp.int32, (nb, CHUNK, CHUNK), 1)
    cc = lax.broadcasted_iota(jnp.int32, (nb, CHUNK, CHUNK), 2)
    rev = (lax.broadcasted_iota(jnp.int32, (nb, CHUNK, CHUNK), 0) % N_SYS) >= C_HEADS
    ahead = jnp.where(rev, rr - cc, cc - rr)
    incl = ahead <= 0
    strict = ahead < 0

    def prepare_chunks(step, carry):
        qs, ks, kts, vbs, bcols, gcols, grows, glasts = [], [], [], [], [], [], [], []
        for cj in range(PREP_CHUNKS):
            c = step * PREP_CHUNKS + cj
            rows = pl.ds(pl.multiple_of(c * CHUNK, CHUNK), CHUNK)
            q_blk = q_s[rows, :]
            k_blk = k_s[rows, :]
            v_blk = v_s[rows, :]
            k_blk_t = k_blk.T
            beta_blk = beta_s[rows, :]
            for d in range(N_DIR):
                gam_blk = gam_s[d, rows, :]
                gam_t = gam_blk.T
                last = 0 if d == 1 else CHUNK - 1
                for h in range(C_HEADS):
                    cols = slice(h * C_DK, (h + 1) * C_DK)
                    lane = d * C_HEADS + h
                    bcol = beta_blk[:, BETA_LANE + lane:BETA_LANE + lane + 1]
                    gcol = gam_blk[:, DECAY_LANE + lane:DECAY_LANE + lane + 1]
                    qs.append(q_blk[:, cols])
                    ks.append(k_blk[:, cols])
                    kts.append(k_blk_t[cols, :])
                    vbs.append(v_blk[:, cols] * bcol)
                    bcols.append(bcol)
                    gcols.append(gcol)
                    grows.append(gam_t[DECAY_LANE + lane:DECAY_LANE + lane + 1, :])
                    glasts.append(gcol[last:last + 1, :])
        q8, k8, kt8, vb8 = (jnp.stack(a) for a in (qs, ks, kts, vbs))
        bcol8, gcol8, grow8, glast8 = (jnp.stack(a) for a in (bcols, gcols, grows, glasts))
        dec = jnp.exp(jnp.where(incl, gcol8 - grow8, -jnp.inf))
        kb8 = k8 * bcol8
        m = jnp.where(strict, _bmm_nt(kb8, k8) * dec, 0.0)
        rhs = jnp.concatenate([vb8, kb8 * jnp.exp(gcol8)], axis=-1)
        inv = _unit_tri_inverse(m, rr, cc)
        sol = _bmm(inv, rhs)
        sol = sol + _bmm(inv, rhs - sol - _bmm3(m, sol))
        wq = jnp.concatenate([sol[:, :, C_DV:2 * C_DV], q8 * jnp.exp(gcol8)], axis=1).astype(BF16)
        att = (_bmm_nt(q8, k8) * dec).astype(BF16)
        kdt = (kt8 * jnp.exp(glast8 - grow8)).astype(BF16)
        eg = jnp.broadcast_to(jnp.exp(glast8), (nb, 1, LANES))
        for cj in range(PREP_CHUNKS):
            c = step * PREP_CHUNKS + cj
            sys = slice(cj * N_SYS, (cj + 1) * N_SYS)
            u_s[c] = sol[sys, :, 0:C_DV]
            wq_s[c] = wq[sys]
            att_s[c] = att[sys]
            kdt_s[c] = kdt[sys]
            eg_s[c] = eg[sys]
        return carry

    lax.fori_loop(0, nc // PREP_CHUNKS, prepare_chunks, 0)

    def scan_chunk(i, state):
        ib = nc - 1 - i

        def both(ref):
            return jnp.concatenate([ref[i, 0:C_HEADS], ref[ib, C_HEADS:N_SYS]], axis=0)

        wq_state = _bmm(both(wq_s), state)
        v_new = both(u_s) - wq_state[:, 0:CHUNK, :]
        o = wq_state[:, CHUNK:2 * CHUNK, :] + _bmm(both(att_s), v_new)
        state = both(eg_s)[:, :, 0:1] * state + _bmm(both(kdt_s), v_new)
        for d in range(N_DIR):
            rows = pl.ds(pl.multiple_of((ib if d == 1 else i) * CHUNK, CHUNK), CHUNK)
            o_s[d, rows, :] = jnp.concatenate([o[d * C_HEADS + h] for h in range(C_HEADS)], axis=-1)
        return state

    if latent:
        state0 = s0_ref[...].reshape(N_SYS, C_DK, C_DV)
    else:
        state0 = jnp.zeros((N_SYS, C_DK, C_DV), F32)
    state = lax.fori_loop(0, nc, scan_chunk, state0)

    o = o_s[0] + o_s[1]
    ms = _dot_rhs_exact(o * o, _group_ones(C_WIDTH, C_DV)) * (1.0 / C_DV)
    gate = zc_ref[:, CONV_CH:CONV_CH + C_WIDTH]
    oc_ref[...] = o * lax.rsqrt(ms + EPS) * dn_ref[0] * _silu(gate)
    if not latent:
        sfin_ref[...] = state.reshape(N_DIR, C_HEADS, C_DK, C_DV)


def _delta(zc, zb, conv_w, al_p, dtb_p, dn_p, layer, latent, state_delta=None):
    seq_len = DEC_SEQ if latent else SEQ
    n_seq = DEC_BATCH if latent else BATCH
    row0 = N_CTX_ROWS // seq_len if latent else 0
    nc = seq_len // CHUNK
    in_specs = [pl.BlockSpec((seq_len, ZC_W), lambda s: (row0 + s, 0)),
                pl.BlockSpec((seq_len, LANES), lambda s: (row0 + s, SMALL_OFF // LANES)),
                pl.BlockSpec((1, CONV_WIDTH, CONV_CH), lambda s: (layer, 0, 0)),
                pl.BlockSpec((1, 1, LANES), lambda s: (layer, 0, 0)),
                pl.BlockSpec((1, 1, LANES), lambda s: (layer, 0, 0)),
                pl.BlockSpec((1, 1, C_WIDTH), lambda s: (layer, 0, 0))]
    args = [zc, zb, conv_w, al_p, dtb_p, dn_p]
    out_specs = [pl.BlockSpec((seq_len, C_WIDTH), lambda s: (s, 0))]
    out_shape = [jax.ShapeDtypeStruct((n_seq * seq_len, C_WIDTH), F32)]
    if latent:
        in_specs.append(pl.BlockSpec((None, None, N_DIR, C_HEADS, C_DK, C_DV), lambda s: (s, layer, 0, 0, 0, 0)))
        args.append(state_delta)
    else:
        out_specs.append(pl.BlockSpec((None, N_DIR, C_HEADS, C_DK, C_DV), lambda s: (s, 0, 0, 0, 0)))
        out_shape.append(jax.ShapeDtypeStruct((n_seq, N_DIR, C_HEADS, C_DK, C_DV), F32))
    return pl.pallas_call(
        functools.partial(_delta_kernel, seq_len=seq_len, latent=latent),
        grid=(n_seq,),
        in_specs=in_specs,
        out_specs=out_specs,
        out_shape=out_shape,
        scratch_shapes=[pltpu.VMEM((seq_len, C_QK_WIDTH), F32),
                        pltpu.VMEM((seq_len, C_QK_WIDTH), F32),
                        pltpu.VMEM((seq_len, C_WIDTH), F32),
                        pltpu.VMEM((seq_len, LANES), F32),
                        pltpu.VMEM((N_DIR, seq_len, LANES), F32),
                        pltpu.VMEM((N_DIR, seq_len, C_WIDTH), F32),
                        pltpu.VMEM((nc, N_SYS, CHUNK, C_DV), F32),
                        pltpu.VMEM((nc, N_SYS, 2 * CHUNK, C_DK), BF16),
                        pltpu.VMEM((nc, N_SYS, CHUNK, CHUNK), BF16),
                        pltpu.VMEM((nc, N_SYS, C_DK, CHUNK), BF16),
                        pltpu.VMEM((nc, N_SYS, 1, LANES), F32)],
        compiler_params=pltpu.CompilerParams(vmem_limit_bytes=VMEM_LIMIT),
        name="delta_latent" if latent else "delta_ctx",
    )(*args)


def _rope_tables():
    rows = DEC_SEQ // GRID_W
    row = jnp.repeat(jnp.arange(rows, dtype=F32), GRID_W)
    col = jnp.tile(jnp.arange(GRID_W, dtype=F32), rows)
    n_freq = HEAD_DIM // 4
    inv_freq = ROPE_THETA ** (-jnp.arange(n_freq, dtype=F32) / n_freq)
    ang_r = row[:, None] * inv_freq
    ang_c = col[:, None] * inv_freq
    ang = jnp.concatenate([ang_r, ang_r, ang_c, ang_c], axis=-1)
    cos, sin = jnp.cos(ang), jnp.sin(ang)
    first = (jnp.arange(HEAD_DIM) % 32) < 16
    sin_a = jnp.where(first, -sin, 0.0)
    sin_b = jnp.where(first, 0.0, sin)
    reps = LANES // HEAD_DIM
    return tuple(jnp.tile(t, (1, reps)) for t in (cos, sin_a, sin_b))


def _block_diag_state(s):
    eye = jnp.eye(B_HEADS, dtype=s.dtype)
    out = jnp.einsum('...hkv,hg->...hkgv', s, eye)
    return out.reshape(s.shape[:-3] + (B_QK_WIDTH, B_WIDTH))


def _diag_blocks(s_bd):
    s5 = s_bd.reshape(s_bd.shape[:-2] + (B_HEADS, B_DK, B_HEADS, B_DV))
    return jnp.stack([s5[..., h, :, h, :] for h in range(B_HEADS)], axis=-3)


def kernel(x_prompt, x_sample, cache_k, cache_v, state_gla, state_delta, c, c_ctx, w_mod, b_mod, norm_gains, w_in, qk_gain, w_gla_gate, b_gla_gate, gla_norm, conv_w, a_log, dt_bias, delta_norm, w_out, w_gate, w_up, w_down):
    xc = x_prompt.reshape(N_CTX_ROWS, D_MODEL)
    xl = x_sample.reshape(N_LAT_ROWS, D_MODEL)
    cond = jnp.concatenate([c_ctx[None, :], c, jnp.zeros((N_COND - 1 - DEC_BATCH, D_MODEL), F32)], axis=0)
    nat_small = A_WIDTH + 2 * A_KV_WIDTH + 2 * B_QK_WIDTH + 2 * B_WIDTH
    nat_c = nat_small + N_DIR * GATE_RANK
    nat_bc = nat_c + ZC_W
    w_in_p = jnp.concatenate(
        [w_in[:, :, 0:nat_small], w_in[:, :, nat_small:nat_c], w_in[:, :, nat_bc:nat_bc + 2 * N_DIR * C_HEADS],
         jnp.zeros((DEPTH, D_MODEL, LANES - N_DIR * GATE_RANK - 2 * N_DIR * C_HEADS), w_in.dtype),
         w_in[:, :, nat_c:nat_bc]], axis=-1).astype(BF16)
    w_out_b = w_out.astype(BF16)
    w_gate_b = w_gate.astype(BF16)
    w_up_b = w_up.astype(BF16)
    w_down_b = w_down.astype(BF16)
    qk_gain2 = jnp.tile(qk_gain, (1, 1, LANES // HEAD_DIM))
    wg_p = jnp.zeros((DEPTH, LANES, N_DIR * B_QK_WIDTH), F32)
    for d in range(N_DIR):
        wg_p = wg_p.at[:, d * GATE_RANK:(d + 1) * GATE_RANK, d * B_QK_WIDTH:(d + 1) * B_QK_WIDTH].set(w_gla_gate[:, d])
    bg_p = b_gla_gate.reshape(DEPTH, 1, N_DIR * B_QK_WIDTH)
    gn_p = jnp.tile(gla_norm, (1, B_HEADS)).reshape(DEPTH, 1, B_WIDTH)
    dn_p = jnp.tile(delta_norm, (1, C_HEADS)).reshape(DEPTH, 1, C_WIDTH)
    al_p = jnp.zeros((DEPTH, 1, LANES), F32).at[:, 0, DECAY_LANE:DECAY_LANE + N_DIR * C_HEADS].set(
        a_log.reshape(DEPTH, N_DIR * C_HEADS))
    dtb_p = jnp.zeros((DEPTH, 1, LANES), F32).at[:, 0, DECAY_LANE:DECAY_LANE + N_DIR * C_HEADS].set(
        dt_bias.reshape(DEPTH, N_DIR * C_HEADS))
    cache_k2 = cache_k.reshape(DEC_BATCH, DEPTH, PAST_LEN, A_KV_WIDTH)
    cache_v2 = cache_v.reshape(DEC_BATCH, DEPTH, PAST_LEN, A_KV_WIDTH)
    s0_gla_bd = _block_diag_state(state_gla.astype(F32))
    s0_delta = state_delta.astype(F32)
    rope = _rope_tables()

    mods = _modulation(cond, w_mod, b_mod).reshape(DEPTH * N_COND, 1, 6 * D_MODEL)

    new_k, new_v, new_gla, new_delta = [], [], [], []
    for l in range(DEPTH):
        za, zb, zc = _in_projection(xc, xl, mods, norm_gains, w_in_p, l)
        oa_c, k_l, v_l = _attention(za, qk_gain2, l, False)
        (oa_l,) = _attention(za, qk_gain2, l, True, cache_k2, cache_v2, rope)
        ob_c, sg_l = _gla(zb, wg_p, bg_p, gn_p, l, False)
        (ob_l,) = _gla(zb, wg_p, bg_p, gn_p, l, True, s0_gla_bd)
        oc_c, sd_l = _delta(zc, zb, conv_w, al_p, dtb_p, dn_p, l, False)
        (oc_l,) = _delta(zc, zb, conv_w, al_p, dtb_p, dn_p, l, True, s0_delta)
        xc, xl = _out_ffn(xc, xl, (oa_c, oa_l, ob_c, ob_l, oc_c, oc_l), mods, norm_gains,
                          w_out_b, w_gate_b, w_up_b, w_down_b, l)
        new_k.append(k_l.reshape(BATCH, SEQ, A_KV_HEADS, HEAD_DIM))
        new_v.append(v_l.reshape(BATCH, SEQ, A_KV_HEADS, HEAD_DIM))
        new_gla.append(_diag_blocks(sg_l))
        new_delta.append(sd_l)

    out_dtype = x_prompt.dtype
    y_prompt = xc.reshape(BATCH, SEQ, D_MODEL)
    y_sample = xl.reshape(DEC_BATCH, DEC_SEQ, D_MODEL)
    return (y_prompt, y_sample,
            jnp.stack(new_k, axis=1), jnp.stack(new_v, axis=1),
            jnp.stack(new_gla, axis=1).astype(out_dtype),
            jnp.stack(new_delta, axis=1).astype(out_dtype))
```

```python
import functools

import jax
import jax.numpy as jnp
import numpy as np
from jax import lax
from jax.experimental import pallas as pl
from jax.experimental.pallas import tpu as pltpu

F32 = jnp.float32
BF16 = jnp.bfloat16

D_MODEL = 1024
BATCH = 16
SEQ = 256
DEPTH = 4
DEC_BATCH = 4
DEC_SEQ = 1024
PAST_LEN = 512
GRID_W = 64
HEAD_DIM = 64
A_HEADS = 8
A_KV_HEADS = 2
A_REP = A_HEADS // A_KV_HEADS
A_WIDTH = A_HEADS * HEAD_DIM
A_KV_WIDTH = A_KV_HEADS * HEAD_DIM
ROPE_THETA = 10000.0
B_HEADS = 4
B_DK = 32
B_DV = 64
B_QK_WIDTH = B_HEADS * B_DK
B_WIDTH = B_HEADS * B_DV
GATE_RANK = 16
GATE_TAU = 16.0
C_HEADS = 4
C_DK = 64
C_DV = 64
C_QK_WIDTH = C_HEADS * C_DK
C_WIDTH = C_HEADS * C_DV
CONV_WIDTH = 5
CONV_CH = 2 * C_QK_WIDTH + C_WIDTH
CHUNK = 64
N_DIR = 2
MIX_WIDTH = A_WIDTH + B_WIDTH + C_WIDTH
D_FF = -(-(8 * D_MODEL) // (3 * 256)) * 256
EPS = 1e-6

LANES = 128
N_CTX_ROWS = BATCH * SEQ
N_LAT_ROWS = DEC_BATCH * DEC_SEQ
N_ROWS = N_CTX_ROWS + N_LAT_ROWS
N_COND = 8

ZA_W = A_WIDTH + 2 * A_KV_WIDTH
ZB_W = 2 * B_QK_WIDTH + 2 * B_WIDTH + LANES
ZC_W = 2 * C_QK_WIDTH + 2 * C_WIDTH
PROJ_PAD = ZA_W + ZB_W + ZC_W
SMALL_OFF = 2 * B_QK_WIDTH + 2 * B_WIDTH
BETA_LANE = N_DIR * GATE_RANK
DECAY_LANE = BETA_LANE + N_DIR * C_HEADS

VMEM_LIMIT = 56 * 1024 * 1024


def _split(x):
    hi = x.astype(BF16)
    lo = (x - hi.astype(F32)).astype(BF16)
    return hi, lo


def _bdot(a, b):
    return jnp.dot(a.astype(BF16), b.astype(BF16), preferred_element_type=F32)


def _bdot_nt(a, b):
    return lax.dot_general(a.astype(BF16), b.astype(BF16), (((1,), (1,)), ((), ())),
                           preferred_element_type=F32)


def _bdot_tn(a, b):
    return lax.dot_general(a.astype(BF16), b.astype(BF16), (((0,), (0,)), ((), ())),
                           preferred_element_type=F32)


def _dot3(a, b):
    ah, al = _split(a)
    bh, bl = _split(b)
    return (jnp.dot(ah, bh, preferred_element_type=F32)
            + jnp.dot(ah, bl, preferred_element_type=F32)
            + jnp.dot(al, bh, preferred_element_type=F32))


def _dot_lhs_exact(m_bf16, x):
    xh, xl = _split(x)
    return (jnp.dot(m_bf16, xh, preferred_element_type=F32)
            + jnp.dot(m_bf16, xl, preferred_element_type=F32))


def _dot_rhs_exact(x, m_bf16):
    xh, xl = _split(x)
    return (jnp.dot(xh, m_bf16, preferred_element_type=F32)
            + jnp.dot(xl, m_bf16, preferred_element_type=F32))


def _group_sumsq(x, ones_g):
    return jnp.dot((x * x).astype(BF16), ones_g, preferred_element_type=F32)


def _group_ones(width, group):
    r = lax.broadcasted_iota(jnp.int32, (width, width), 0) // group
    c = lax.broadcasted_iota(jnp.int32, (width, width), 1) // group
    return (r == c).astype(BF16)


def _sigmoid(x):
    return 1.0 / (1.0 + jnp.exp(-x))


def _silu(x):
    return x * _sigmoid(x)


def _softplus(x):
    return jnp.maximum(x, 0.0) + jnp.log1p(jnp.exp(-jnp.abs(x)))


def _rms(x, g):
    return x * lax.rsqrt(jnp.mean(x * x, axis=-1, keepdims=True) + EPS) * g


def _chunk_tri(t, reverse):
    r = lax.broadcasted_iota(jnp.int32, (t, t), 0)
    c = lax.broadcasted_iota(jnp.int32, (t, t), 1)
    same = (r // CHUNK) == (c // CHUNK)
    tri = (c >= r) if reverse else (c <= r)
    return (same & tri).astype(BF16)


def _mod_row(i, tm):
    start = i * tm
    return jnp.where(start < N_CTX_ROWS, 0, 1 + (start - N_CTX_ROWS) // DEC_SEQ)


def _mod_kernel(cond_ref, w_ref, b_ref, o_ref):
    c = cond_ref[...]
    o_ref[0] = _bdot(_silu(c), w_ref[0]) + b_ref[0]


def _modulation(cond, w_mod, b_mod):
    tn = 1536
    return pl.pallas_call(
        _mod_kernel,
        grid=(DEPTH, 6 * D_MODEL // tn),
        in_specs=[pl.BlockSpec((N_COND, D_MODEL), lambda l, j: (0, 0)),
                  pl.BlockSpec((1, D_MODEL, tn), lambda l, j: (l, 0, j)),
                  pl.BlockSpec((1, 1, tn), lambda l, j: (l, 0, j))],
        out_specs=pl.BlockSpec((1, N_COND, tn), lambda l, j: (l, 0, j)),
        out_shape=jax.ShapeDtypeStruct((DEPTH, N_COND, 6 * D_MODEL), F32),
        compiler_params=pltpu.CompilerParams(vmem_limit_bytes=VMEM_LIMIT),
        name="adaln_mod",
    )(cond, w_mod, b_mod.reshape(DEPTH, 1, 6 * D_MODEL))


DENSE_TM = 512
N_CTX_BLOCKS = N_CTX_ROWS // DENSE_TM


def _ctx_map(i):
    return (jnp.minimum(i, N_CTX_BLOCKS - 1), 0)


def _lat_map(i):
    return (jnp.maximum(i - N_CTX_BLOCKS, 0), 0)


def _inproj_kernel(xc_ref, xl_ref, mod_ref, ng_ref, w_ref, za_ref, zb_ref, zc_ref):
    x = jnp.where(pl.program_id(0) < N_CTX_BLOCKS, xc_ref[...], xl_ref[...])
    mod = mod_ref[0]
    shift = mod[:, 0:D_MODEL]
    scale = mod[:, D_MODEL:2 * D_MODEL]
    h = _rms(x, ng_ref[0, 0:1, :]) * (1.0 + scale) + shift
    z = _bdot(h, w_ref[0])
    za_ref[...] = z[:, 0:ZA_W]
    zb_ref[...] = z[:, ZA_W:ZA_W + ZB_W]
    zc_ref[...] = z[:, ZA_W + ZB_W:PROJ_PAD]


def _in_projection(xc, xl, mods, norm_gains, w_in_p, layer):
    tm = DENSE_TM
    return pl.pallas_call(
        _inproj_kernel,
        grid=(N_ROWS // tm,),
        in_specs=[pl.BlockSpec((tm, D_MODEL), _ctx_map),
                  pl.BlockSpec((tm, D_MODEL), _lat_map),
                  pl.BlockSpec((1, 1, 6 * D_MODEL), lambda i: (layer * N_COND + _mod_row(i, tm), 0, 0)),
                  pl.BlockSpec((1, 4, D_MODEL), lambda i: (layer, 0, 0)),
                  pl.BlockSpec((1, D_MODEL, PROJ_PAD), lambda i: (layer, 0, 0))],
        out_specs=[pl.BlockSpec((tm, ZA_W), lambda i: (i, 0)),
                   pl.BlockSpec((tm, ZB_W), lambda i: (i, 0)),
                   pl.BlockSpec((tm, ZC_W), lambda i: (i, 0))],
        out_shape=[jax.ShapeDtypeStruct((N_ROWS, ZA_W), F32),
                   jax.ShapeDtypeStruct((N_ROWS, ZB_W), F32),
                   jax.ShapeDtypeStruct((N_ROWS, ZC_W), F32)],
        compiler_params=pltpu.CompilerParams(vmem_limit_bytes=VMEM_LIMIT),
        name="in_proj",
    )(xc, xl, mods, norm_gains, w_in_p)


FF_TILE = 256


def _ffn_kernel(xc_ref, xl_ref, oac_ref, oal_ref, obc_ref, obl_ref, occ_ref, ocl_ref, mod_ref, ng_ref,
                wo_ref, wg_ref, wu_ref, wd_ref, yc_ref, yl_ref):
    is_ctx = pl.program_id(0) < N_CTX_BLOCKS
    x = jnp.where(is_ctx, xc_ref[...], xl_ref[...])
    mod = mod_ref[0]
    gate_m = mod[:, 2 * D_MODEL:3 * D_MODEL]
    shift_f = mod[:, 3 * D_MODEL:4 * D_MODEL]
    scale_f = mod[:, 4 * D_MODEL:5 * D_MODEL]
    gate_f = mod[:, 5 * D_MODEL:6 * D_MODEL]
    mix_in = jnp.concatenate([jnp.where(is_ctx, oac_ref[...], oal_ref[...]),
                              jnp.where(is_ctx, obc_ref[...], obl_ref[...]),
                              jnp.where(is_ctx, occ_ref[...], ocl_ref[...])], axis=-1)
    mix = _bdot(mix_in, wo_ref[0])
    x1 = x + gate_m * _rms(mix, ng_ref[0, 1:2, :])
    h = (_rms(x1, ng_ref[0, 2:3, :]) * (1.0 + scale_f) + shift_f).astype(BF16)
    f = jnp.zeros(x.shape, F32)
    for j in range(D_FF // FF_TILE):
        cols = slice(j * FF_TILE, (j + 1) * FF_TILE)
        g = jnp.dot(h, wg_ref[0, :, cols], preferred_element_type=F32)
        u = jnp.dot(h, wu_ref[0, :, cols], preferred_element_type=F32)
        f = f + _bdot(_silu(g) * u, wd_ref[0, cols, :])
    y = x1 + gate_f * _rms(f, ng_ref[0, 3:4, :])

    @pl.when(is_ctx)
    def _():
        yc_ref[...] = y

    @pl.when(jnp.logical_not(is_ctx))
    def _():
        yl_ref[...] = y


def _out_ffn(xc, xl, mixer_outs, mods, norm_gains, w_out_b, w_gate_b, w_up_b, w_down_b, layer):
    tm = DENSE_TM
    resident = dict(pipeline_mode=pl.Buffered(1))
    pair_specs = []
    for width in (D_MODEL, A_WIDTH, B_WIDTH, C_WIDTH):
        pair_specs += [pl.BlockSpec((tm, width), _ctx_map), pl.BlockSpec((tm, width), _lat_map)]
    return pl.pallas_call(
        _ffn_kernel,
        grid=(N_ROWS // tm,),
        in_specs=pair_specs + [
                  pl.BlockSpec((1, 1, 6 * D_MODEL), lambda i: (layer * N_COND + _mod_row(i, tm), 0, 0)),
                  pl.BlockSpec((1, 4, D_MODEL), lambda i: (layer, 0, 0)),
                  pl.BlockSpec((1, MIX_WIDTH, D_MODEL), lambda i: (layer, 0, 0), **resident),
                  pl.BlockSpec((1, D_MODEL, D_FF), lambda i: (layer, 0, 0), **resident),
                  pl.BlockSpec((1, D_MODEL, D_FF), lambda i: (layer, 0, 0), **resident),
                  pl.BlockSpec((1, D_FF, D_MODEL), lambda i: (layer, 0, 0), **resident)],
        out_specs=[pl.BlockSpec((tm, D_MODEL), _ctx_map), pl.BlockSpec((tm, D_MODEL), _lat_map)],
        out_shape=[jax.ShapeDtypeStruct((N_CTX_ROWS, D_MODEL), F32),
                   jax.ShapeDtypeStruct((N_LAT_ROWS, D_MODEL), F32)],
        compiler_params=pltpu.CompilerParams(
            dimension_semantics=("arbitrary",), vmem_limit_bytes=VMEM_LIMIT),
        name="out_ffn",
    )(xc, xl, *mixer_outs, mods, norm_gains, w_out_b, w_gate_b, w_up_b, w_down_b)


ATT_TQ = 256


def _head_norm(x, gain_row, ones_g):
    ss = _group_sumsq(x, ones_g)
    return x * lax.rsqrt(ss * (1.0 / HEAD_DIM) + EPS) * gain_row


def _rope(x, cos, sin_a, sin_b):
    return x * cos + pltpu.roll(x, LANES - 16, 1) * sin_a + pltpu.roll(x, 16, 1) * sin_b


def _attn_kernel(*refs, seq_len, latent):
    if latent:
        (q_ref, kv_ref, ck_ref, cv_ref, qkg_ref, cos_ref, sa_ref, sb_ref, o_ref, k_s, v_s) = refs
    else:
        (q_ref, kv_ref, qkg_ref, o_ref, kn_ref, vn_ref, k_s, v_s) = refs
    j = pl.program_id(1)
    ones_g = _group_ones(LANES, HEAD_DIM)
    gq = qkg_ref[0, 0:1, :]
    gk = qkg_ref[0, 1:2, :]
    past = PAST_LEN if latent else 0

    @pl.when(j == 0)
    def _():
        ka = kv_ref[:, 0:A_KV_WIDTH]
        va = kv_ref[:, A_KV_WIDTH:2 * A_KV_WIDTH]
        kn = _head_norm(ka, gk, ones_g)
        if latent:
            kr = _rope(kn, cos_ref[...], sa_ref[...], sb_ref[...])
            ck = ck_ref[...]
            cv = cv_ref[...]
            for g in range(A_KV_HEADS):
                cols = slice(g * HEAD_DIM, (g + 1) * HEAD_DIM)
                k_s[g, 0:past, :] = ck[:, cols].astype(BF16)
                v_s[g, 0:past, :] = cv[:, cols].astype(BF16)
                k_s[g, past:past + seq_len, :] = kr[:, cols].astype(BF16)
                v_s[g, past:past + seq_len, :] = va[:, cols].astype(BF16)
        else:
            kn_ref[...] = kn
            vn_ref[...] = va
            for g in range(A_KV_HEADS):
                cols = slice(g * HEAD_DIM, (g + 1) * HEAD_DIM)
                k_s[g] = kn[:, cols].astype(BF16)
                v_s[g] = va[:, cols].astype(BF16)

    if latent:
        rows = pl.ds(pl.multiple_of(j * ATT_TQ, ATT_TQ), ATT_TQ)
        cos = cos_ref[rows, :]
        sin_a = sa_ref[rows, :]
        sin_b = sb_ref[rows, :]
    outs = []
    for t in range(A_WIDTH // LANES):
        qt = _head_norm(q_ref[:, t * LANES:(t + 1) * LANES], gq, ones_g)
        if latent:
            qt = _rope(qt, cos, sin_a, sin_b)
        qt = (qt * (HEAD_DIM ** -0.5)).astype(BF16)
        for hh in range(LANES // HEAD_DIM):
            h = t * (LANES // HEAD_DIM) + hh
            g = h // A_REP
            qh = qt[:, hh * HEAD_DIM:(hh + 1) * HEAD_DIM]
            s = lax.dot_general(qh, k_s[g], (((1,), (1,)), ((), ())), preferred_element_type=F32)
            m = jnp.max(s, axis=-1, keepdims=True)
            p = jnp.exp(s - m)
            l = jnp.sum(p, axis=-1, keepdims=True)
            o = jnp.dot(p.astype(BF16), v_s[g], preferred_element_type=F32)
            outs.append(o / l)
    o_ref[...] = jnp.concatenate(outs, axis=-1)


def _attention(za, qk_gain2, layer, latent, cache_k=None, cache_v=None, rope=None):
    seq_len = DEC_SEQ if latent else SEQ
    n_seq = DEC_BATCH if latent else BATCH
    row0 = N_CTX_ROWS // seq_len if latent else 0
    nq = seq_len // ATT_TQ
    row0q = N_CTX_ROWS // ATT_TQ if latent else 0
    s_len = seq_len + (PAST_LEN if latent else 0)
    in_specs = [pl.BlockSpec((ATT_TQ, A_WIDTH), lambda s, j: (row0q + s * nq + j, 0)),
                pl.BlockSpec((seq_len, 2 * A_KV_WIDTH), lambda s, j: (row0 + s, 2))]
    args = [za, za]
    if latent:
        in_specs += [pl.BlockSpec((None, None, PAST_LEN, A_KV_WIDTH), lambda s, j: (s, layer, 0, 0)),
                     pl.BlockSpec((None, None, PAST_LEN, A_KV_WIDTH), lambda s, j: (s, layer, 0, 0))]
        args += [cache_k, cache_v]
    in_specs.append(pl.BlockSpec((1, 2, LANES), lambda s, j: (layer, 0, 0)))
    args.append(qk_gain2)
    if latent:
        in_specs += [pl.BlockSpec((seq_len, LANES), lambda s, j: (0, 0))] * 3
        args += list(rope)
    out_specs = [pl.BlockSpec((ATT_TQ, A_WIDTH), lambda s, j: (s * nq + j, 0))]
    out_shape = [jax.ShapeDtypeStruct((n_seq * seq_len, A_WIDTH), F32)]
    if not latent:
        out_specs += [pl.BlockSpec((None, seq_len, A_KV_WIDTH), lambda s, j: (s, 0, 0))] * 2
        out_shape += [jax.ShapeDtypeStruct((n_seq, seq_len, A_KV_WIDTH), F32)] * 2
    return pl.pallas_call(
        functools.partial(_attn_kernel, seq_len=seq_len, latent=latent),
        grid=(n_seq, nq),
        in_specs=in_specs,
        out_specs=out_specs,
        out_shape=out_shape,
        scratch_shapes=[pltpu.VMEM((A_KV_HEADS, s_len, HEAD_DIM), BF16),
                        pltpu.VMEM((A_KV_HEADS, s_len, HEAD_DIM), BF16)],
        compiler_params=pltpu.CompilerParams(
            dimension_semantics=("arbitrary", "arbitrary"), vmem_limit_bytes=VMEM_LIMIT),
        name="attn_latent" if latent else "attn_ctx",
    )(*args)


def _gla_kernel(*refs, seq_len, latent):
    if latent:
        zb_ref, wg_ref, bg_ref, gn_ref, s0_ref, ob_ref = refs
    else:
        zb_ref, wg_ref, bg_ref, gn_ref, ob_ref, sfin_ref = refs
    nc = seq_len // CHUNK
    q = zb_ref[:, 0:B_QK_WIDTH] * (B_DK ** -0.5)
    k = zb_ref[:, B_QK_WIDTH:2 * B_QK_WIDTH]
    v = zb_ref[:, 2 * B_QK_WIDTH:2 * B_QK_WIDTH + B_WIDTH]
    small = zb_ref[:, SMALL_OFF:SMALL_OFF + LANES]
    pre = _dot3(small, wg_ref[0]) + bg_ref[0]
    glog = (jnp.minimum(pre, 0.0) - jnp.log1p(jnp.exp(-jnp.abs(pre)))) * (1.0 / GATE_TAU)

    kr = lax.broadcasted_iota(jnp.int32, (B_QK_WIDTH, B_WIDTH), 0) // B_DK
    vc = lax.broadcasted_iota(jnp.int32, (B_QK_WIDTH, B_WIDTH), 1) // B_DV
    bd_kv = kr == vc
    sr = lax.broadcasted_iota(jnp.int32, (B_WIDTH, B_WIDTH), 0) // CHUNK
    sc = lax.broadcasted_iota(jnp.int32, (B_WIDTH, B_WIDTH), 1) // B_DV
    bd_sv = sr == sc
    tt = lax.broadcasted_iota(jnp.int32, (CHUNK, B_WIDTH), 0)
    ss = lax.broadcasted_iota(jnp.int32, (CHUNK, B_WIDTH), 1) % CHUNK

    q3 = q.reshape(nc, CHUNK, B_QK_WIDTH)
    k3 = k.reshape(nc, CHUNK, B_QK_WIDTH)
    intra, upds, decays, q_sts = [], [], [], []
    for d in range(N_DIR):
        g = glog[:, d * B_QK_WIDTH:(d + 1) * B_QK_WIDTH]
        b = _dot_lhs_exact(_chunk_tri(seq_len, d == 1), g)
        b3 = b.reshape(nc, CHUNK, B_QK_WIDTH)
        mid = b3[:, CHUNK // 2:CHUNK // 2 + 1, :]
        last = b3[:, 0:1, :] if d == 1 else b3[:, CHUNK - 1:CHUNK, :]
        q_in = q3 * jnp.exp(b3 - mid)
        k_in = k3 * jnp.exp(mid - b3)
        q_sts.append(q3 * jnp.exp(b3))
        k_st = k3 * jnp.exp(last - b3)
        decay_t = jnp.exp(jnp.broadcast_to(last, (nc, 8, B_QK_WIDTH)).reshape(nc * 8, B_QK_WIDTH)).T
        causal = (ss >= tt) if d == 1 else (ss <= tt)
        v_chunks = [v[c * CHUNK:(c + 1) * CHUNK, :] for c in range(nc)]
        atts = [_bdot(q_in[c], jnp.where(bd_kv, jnp.concatenate([k_in[c].T] * B_HEADS, axis=1), 0.0))
                for c in range(nc)]
        upds.append([jnp.where(bd_kv, _bdot_tn(k_st[c], v_chunks[c]), 0.0) for c in range(nc)])
        intra.append([_bdot(jnp.where(causal, atts[c], 0.0),
                            jnp.where(bd_sv, jnp.concatenate([v_chunks[c]] * B_HEADS, axis=0), 0.0))
                      for c in range(nc)])
        decays.append([decay_t[:, 8 * c:8 * c + 1] for c in range(nc)])
    o_dirs = []
    for d in range(N_DIR):
        state = s0_ref[d] if latent else jnp.zeros((B_QK_WIDTH, B_WIDTH), F32)
        o_chunks = [None] * nc
        for c in (range(nc - 1, -1, -1) if d == 1 else range(nc)):
            o_chunks[c] = intra[d][c] + _bdot(q_sts[d][c], state)
            state = decays[d][c] * state + upds[d][c]
        o_dirs.append(jnp.concatenate(o_chunks, axis=0))
        if not latent:
            sfin_ref[d] = state
    o = o_dirs[0] + o_dirs[1]
    ms = _group_sumsq(o, _group_ones(B_WIDTH, B_DV)) * (1.0 / B_DV)
    r = zb_ref[:, 2 * B_QK_WIDTH + B_WIDTH:2 * B_QK_WIDTH + 2 * B_WIDTH]
    ob_ref[...] = o * lax.rsqrt(ms + EPS) * gn_ref[0] * _silu(r)


def _gla(zb, wg_p, bg_p, gn_p, layer, latent, s0_bd=None):
    seq_len = DEC_SEQ if latent else SEQ
    n_seq = DEC_BATCH if latent else BATCH
    row0 = N_CTX_ROWS // seq_len if latent else 0
    in_specs = [pl.BlockSpec((seq_len, ZB_W), lambda s: (row0 + s, 0)),
                pl.BlockSpec((1, LANES, N_DIR * B_QK_WIDTH), lambda s: (layer, 0, 0)),
                pl.BlockSpec((1, 1, N_DIR * B_QK_WIDTH), lambda s: (layer, 0, 0)),
                pl.BlockSpec((1, 1, B_WIDTH), lambda s: (layer, 0, 0))]
    args = [zb, wg_p, bg_p, gn_p]
    out_specs = [pl.BlockSpec((seq_len, B_WIDTH), lambda s: (s, 0))]
    out_shape = [jax.ShapeDtypeStruct((n_seq * seq_len, B_WIDTH), F32)]
    if latent:
        in_specs.append(pl.BlockSpec((None, None, N_DIR, B_QK_WIDTH, B_WIDTH), lambda s: (s, layer, 0, 0, 0)))
        args.append(s0_bd)
    else:
        out_specs.append(pl.BlockSpec((None, N_DIR, B_QK_WIDTH, B_WIDTH), lambda s: (s, 0, 0, 0)))
        out_shape.append(jax.ShapeDtypeStruct((n_seq, N_DIR, B_QK_WIDTH, B_WIDTH), F32))
    return pl.pallas_call(
        functools.partial(_gla_kernel, seq_len=seq_len, latent=latent),
        grid=(n_seq,),
        in_specs=in_specs,
        out_specs=out_specs,
        out_shape=out_shape,
        compiler_params=pltpu.CompilerParams(vmem_limit_bytes=VMEM_LIMIT),
        name="gla_latent" if latent else "gla_ctx",
    )(*args)


TRI_BASE = 8
C_PAIRS = C_HEADS // 2
N_SYS = N_DIR * C_PAIRS
PREP_CHUNKS = 4
CUM_ROWS = 256
CONV_PAD = 8


def _bd(y, mask):
    return jnp.where(mask, jnp.concatenate([y, y], axis=0), 0.0).astype(BF16)


def _pair_tri_inverse(ms, bd_mask, tt, ss):
    pws = [jnp.where((tt // TRI_BASE) == (ss // TRI_BASE), -m, 0.0) for m in ms]
    invs = [(tt == ss).astype(F32) + pw for pw in pws]
    span = 2
    while span < TRI_BASE:
        pws = [_bdot(pw, _bd(pw, bd_mask)) for pw in pws]
        invs = [inv + _bdot(inv, _bd(pw, bd_mask)) for inv, pw in zip(invs, pws)]
        span *= 2
    size = 2 * TRI_BASE
    while size <= CHUNK:
        off = ((tt // size) == (ss // size)) & ((tt // (size // 2)) != (ss // (size // 2)))
        cxs = [_bdot(jnp.where(off, m, 0.0), _bd(inv, bd_mask)) for m, inv in zip(ms, invs)]
        invs = [inv - _bdot(inv, _bd(cx, bd_mask)) for inv, cx in zip(invs, cxs)]
        size *= 2
    return invs


def _delta_kernel(*refs, seq_len, latent):
    if latent:
        (zc_ref, sm_ref, cw_ref, al_ref, dtb_ref, dn_ref, s0_ref, oc_ref,
         xp_s, q_s, k_s, v_s, bx_s, gx_s, gr_s, o_s, u_s, wq_s, ak_s, eg_s) = refs
    else:
        (zc_ref, sm_ref, cw_ref, al_ref, dtb_ref, dn_ref, oc_ref, sfin_ref,
         xp_s, q_s, k_s, v_s, bx_s, gx_s, gr_s, o_s, u_s, wq_s, ak_s, eg_s) = refs
    nc = seq_len // CHUNK

    xp_s[0:CONV_PAD, :] = jnp.zeros((CONV_PAD, CONV_CH), F32)
    xp_s[CONV_PAD + seq_len:2 * CONV_PAD + seq_len, :] = jnp.zeros((CONV_PAD, CONV_CH), F32)
    xp_s[CONV_PAD:CONV_PAD + seq_len, :] = zc_ref[:, 0:CONV_CH]
    y = None
    for tap in range(CONV_WIDTH):
        start = CONV_PAD + tap - CONV_WIDTH // 2
        term = xp_s[start:start + seq_len, :] * cw_ref[0, tap:tap + 1, :]
        y = term if y is None else y + term
    y = _silu(y)
    ones_g = _group_ones(C_QK_WIDTH, C_DK)
    qc = y[:, 0:C_QK_WIDTH]
    kc = y[:, C_QK_WIDTH:2 * C_QK_WIDTH]
    q_s[...] = qc * lax.rsqrt(_group_sumsq(qc, ones_g) + EPS) * (C_DK ** -0.5)
    k_s[...] = kc * lax.rsqrt(_group_sumsq(kc, ones_g) + EPS)
    v_s[...] = y[:, 2 * C_QK_WIDTH:CONV_CH]

    small = sm_ref[...]
    beta = _sigmoid(small)
    glog = -jnp.exp(al_ref[0]) * _softplus(small + dtb_ref[0])
    lane_r = lax.broadcasted_iota(jnp.int32, (LANES, C_WIDTH), 0)
    head_c = lax.broadcasted_iota(jnp.int32, (LANES, C_WIDTH), 1) // C_DV
    cr = lax.broadcasted_iota(jnp.int32, (CUM_ROWS, CUM_ROWS), 0)
    cl = lax.broadcasted_iota(jnp.int32, (CUM_ROWS, CUM_ROWS), 1)
    same_chunk = ((cr // CHUNK) == (cl // CHUNK)).astype(BF16)
    on_diag = (lax.broadcasted_iota(jnp.int32, (CUM_ROWS, C_WIDTH), 0) % CHUNK
               == lax.broadcasted_iota(jnp.int32, (CUM_ROWS, C_WIDTH), 1) % C_DV)
    for d in range(N_DIR):
        bx_s[d] = _dot_rhs_exact(beta, (lane_r == BETA_LANE + d * C_HEADS + head_c).astype(BF16))
        pick_gam = (lane_r == DECAY_LANE + d * C_HEADS + head_c).astype(BF16)
        tri = _chunk_tri(CUM_ROWS, d == 1)
        for blk in range(seq_len // CUM_ROWS):
            r = slice(blk * CUM_ROWS, (blk + 1) * CUM_ROWS)
            g_x = _dot_rhs_exact(_dot_lhs_exact(tri, glog[r, :]), pick_gam)
            gx_s[d, r, :] = g_x
            gr_s[d, r, :] = _dot_lhs_exact(same_chunk, jnp.where(on_diag, g_x, 0.0))

    tt = lax.broadcasted_iota(jnp.int32, (CHUNK, LANES), 0)
    ss = lax.broadcasted_iota(jnp.int32, (CHUNK, LANES), 1) % CHUNK
    bd1 = (lax.broadcasted_iota(jnp.int32, (LANES, LANES), 0) // CHUNK
           == lax.broadcasted_iota(jnp.int32, (LANES, LANES), 1) // CHUNK)
    bd2 = jnp.concatenate([bd1, bd1], axis=1)

    def prepare_chunks(step, carry):
        ids, ms, rhss, qgs, aks, egs = [], [], [], [], [], []
        for cj in range(PREP_CHUNKS):
            c = step * PREP_CHUNKS + cj
            rows = pl.ds(pl.multiple_of(c * CHUNK, CHUNK), CHUNK)
            k_t = k_s[rows, :].T
            for d in range(N_DIR):
                ahead = (tt - ss) if d == 1 else (ss - tt)
                last = 0 if d == 1 else CHUNK - 1
                for p in range(C_PAIRS):
                    lanes = slice(p * LANES, (p + 1) * LANES)
                    qp = q_s[rows, lanes]
                    kp = k_s[rows, lanes]
                    bx = bx_s[d, rows, lanes]
                    gx = gx_s[d, rows, lanes]
                    gr = gr_s[d, rows, lanes]
                    dec = jnp.exp(jnp.where(ahead <= 0, gx - gr, -jnp.inf))
                    kb = kp * bx
                    ma = lax.dot_general(jnp.concatenate([kb, qp], axis=0).astype(BF16), _bd(kp, bd1),
                                         (((1,), (1,)), ((), ())), preferred_element_type=F32)
                    e_gx = jnp.exp(gx)
                    k_t_pair = jnp.concatenate([k_t[p * LANES:p * LANES + C_DK, :],
                                                k_t[p * LANES + C_DK:(p + 1) * LANES, :]], axis=1)
                    g_last = gx[last:last + 1, :]
                    ids.append((c, d * C_PAIRS + p))
                    ms.append(jnp.where(ahead < 0, ma[0:CHUNK] * dec, 0.0))
                    rhss.append(jnp.concatenate([v_s[rows, lanes] * bx, kb * e_gx], axis=1))
                    qgs.append(qp * e_gx)
                    aks.append(jnp.concatenate([ma[CHUNK:2 * CHUNK] * dec, k_t_pair * jnp.exp(g_last - gr)],
                                               axis=0).astype(BF16))
                    egs.append(jnp.exp(g_last))
        invs = _pair_tri_inverse(ms, bd1, tt, ss)
        sols = [_bdot(inv, _bd(rhs, bd2)) for inv, rhs in zip(invs, rhss)]
        m_sols = []
        for m, sol in zip(ms, sols):
            sh, sl = _split(sol)
            mh, ml = _split(m)
            shb, slb = _bd(sh, bd2), _bd(sl, bd2)
            m_sols.append(jnp.dot(mh, shb, preferred_element_type=F32)
                          + jnp.dot(mh, slb, preferred_element_type=F32)
                          + jnp.dot(ml, shb, preferred_element_type=F32))
        sols = [sol + _bdot(inv, _bd(rhs - sol - m_sol, bd2))
                for inv, rhs, sol, m_sol in zip(invs, rhss, sols, m_sols)]
        for (c, sy), sol, qg, ak, eg in zip(ids, sols, qgs, aks, egs):
            u_s[c, sy] = sol[:, 0:LANES]
            wq_s[c, sy] = jnp.concatenate([sol[:, LANES:2 * LANES], qg], axis=0).astype(BF16)
            ak_s[c, sy] = ak
            eg_s[c, sy] = eg
        return carry

    lax.fori_loop(0, nc // PREP_CHUNKS, prepare_chunks, 0)

    def scan_chunk(i, state):
        ids = [(d, (nc - 1 - i) if d == 1 else i, p) for d in range(N_DIR) for p in range(C_PAIRS)]
        wq_states = [jnp.dot(wq_s[c, d * C_PAIRS + p], _bd(state[d * C_PAIRS + p], bd1),
                             preferred_element_type=F32) for d, c, p in ids]
        v_news = [u_s[c, d * C_PAIRS + p] - wqs[0:CHUNK] for (d, c, p), wqs in zip(ids, wq_states)]
        aks = [jnp.dot(ak_s[c, d * C_PAIRS + p], _bd(v_new, bd1), preferred_element_type=F32)
               for (d, c, p), v_new in zip(ids, v_news)]
        new_state = [eg_s[c, d * C_PAIRS + p] * state[d * C_PAIRS + p] + ak[CHUNK:2 * CHUNK]
                     for (d, c, p), ak in zip(ids, aks)]
        for (d, c, p), wqs, ak in zip(ids, wq_states, aks):
            o_s[d, pl.ds(pl.multiple_of(c * CHUNK, CHUNK), CHUNK), p * LANES:(p + 1) * LANES] = (
                wqs[CHUNK:2 * CHUNK] + ak[0:CHUNK])
        return jnp.stack(new_state)

    if latent:
        state0 = s0_ref[...]
    else:
        state0 = jnp.zeros((N_SYS, C_DK, LANES), F32)
    state = lax.fori_loop(0, nc, scan_chunk, state0)

    o = o_s[0] + o_s[1]
    ms = _group_sumsq(o, _group_ones(C_WIDTH, C_DV)) * (1.0 / C_DV)
    gate = zc_ref[:, CONV_CH:CONV_CH + C_WIDTH]
    oc_ref[...] = o * lax.rsqrt(ms + EPS) * dn_ref[0] * _silu(gate)
    if not latent:
        sfin_ref[...] = state


def _delta(zc, zb, conv_w, al_p, dtb_p, dn_p, layer, latent, state_delta=None):
    seq_len = DEC_SEQ if latent else SEQ
    n_seq = DEC_BATCH if latent else BATCH
    row0 = N_CTX_ROWS // seq_len if latent else 0
    nc = seq_len // CHUNK
    in_specs = [pl.BlockSpec((seq_len, ZC_W), lambda s: (row0 + s, 0)),
                pl.BlockSpec((seq_len, LANES), lambda s: (row0 + s, SMALL_OFF // LANES)),
                pl.BlockSpec((1, CONV_WIDTH, CONV_CH), lambda s: (layer, 0, 0)),
                pl.BlockSpec((1, 1, LANES), lambda s: (layer, 0, 0)),
                pl.BlockSpec((1, 1, LANES), lambda s: (layer, 0, 0)),
                pl.BlockSpec((1, 1, C_WIDTH), lambda s: (layer, 0, 0))]
    args = [zc, zb, conv_w, al_p, dtb_p, dn_p]
    out_specs = [pl.BlockSpec((seq_len, C_WIDTH), lambda s: (s, 0))]
    out_shape = [jax.ShapeDtypeStruct((n_seq * seq_len, C_WIDTH), F32)]
    if latent:
        in_specs.append(pl.BlockSpec((None, None, N_SYS, C_DK, LANES), lambda s: (s, layer, 0, 0, 0)))
        args.append(state_delta)
    else:
        out_specs.append(pl.BlockSpec((None, N_SYS, C_DK, LANES), lambda s: (s, 0, 0, 0)))
        out_shape.append(jax.ShapeDtypeStruct((n_seq, N_SYS, C_DK, LANES), F32))
    return pl.pallas_call(
        functools.partial(_delta_kernel, seq_len=seq_len, latent=latent),
        grid=(n_seq,),
        in_specs=in_specs,
        out_specs=out_specs,
        out_shape=out_shape,
        scratch_shapes=[pltpu.VMEM((seq_len + 2 * CONV_PAD, CONV_CH), F32),
                        pltpu.VMEM((seq_len, C_QK_WIDTH), F32),
                        pltpu.VMEM((seq_len, C_QK_WIDTH), F32),
                        pltpu.VMEM((seq_len, C_WIDTH), F32),
                        pltpu.VMEM((N_DIR, seq_len, C_WIDTH), F32),
                        pltpu.VMEM((N_DIR, seq_len, C_WIDTH), F32),
                        pltpu.VMEM((N_DIR, seq_len, C_WIDTH), F32),
                        pltpu.VMEM((N_DIR, seq_len, C_WIDTH), F32),
                        pltpu.VMEM((nc, N_SYS, CHUNK, LANES), F32),
                        pltpu.VMEM((nc, N_SYS, 2 * CHUNK, LANES), BF16),
                        pltpu.VMEM((nc, N_SYS, 2 * CHUNK, LANES), BF16),
                        pltpu.VMEM((nc, N_SYS, 1, LANES), F32)],
        compiler_params=pltpu.CompilerParams(vmem_limit_bytes=VMEM_LIMIT),
        name="delta_latent" if latent else "delta_ctx",
    )(*args)


def _rope_tables():
    rows = DEC_SEQ // GRID_W
    row = jnp.repeat(jnp.arange(rows, dtype=F32), GRID_W)
    col = jnp.tile(jnp.arange(GRID_W, dtype=F32), rows)
    n_freq = HEAD_DIM // 4
    inv_freq = ROPE_THETA ** (-jnp.arange(n_freq, dtype=F32) / n_freq)
    ang_r = row[:, None] * inv_freq
    ang_c = col[:, None] * inv_freq
    ang = jnp.concatenate([ang_r, ang_r, ang_c, ang_c], axis=-1)
    cos, sin = jnp.cos(ang), jnp.sin(ang)
    first = (jnp.arange(HEAD_DIM) % 32) < 16
    sin_a = jnp.where(first, -sin, 0.0)
    sin_b = jnp.where(first, 0.0, sin)
    reps = LANES // HEAD_DIM
    return tuple(jnp.tile(t, (1, reps)) for t in (cos, sin_a, sin_b))


def _block_diag_state(s):
    eye = jnp.eye(B_HEADS, dtype=s.dtype)
    out = jnp.einsum('...hkv,hg->...hkgv', s, eye)
    return out.reshape(s.shape[:-3] + (B_QK_WIDTH, B_WIDTH))


def _pair_state(s):
    lead = s.shape[:-4]
    s = s.reshape(lead + (N_DIR, C_PAIRS, 2, C_DK, C_DV))
    s = jnp.moveaxis(s, -3, -2)
    return s.reshape(lead + (N_SYS, C_DK, 2 * C_DV))


def _unpair_state(s):
    lead = s.shape[:-3]
    s = s.reshape(lead + (N_DIR, C_PAIRS, C_DK, 2, C_DV))
    s = jnp.moveaxis(s, -2, -3)
    return s.reshape(lead + (N_DIR, C_HEADS, C_DK, C_DV))


def _diag_blocks(s_bd):
    s5 = s_bd.reshape(s_bd.shape[:-2] + (B_HEADS, B_DK, B_HEADS, B_DV))
    return jnp.stack([s5[..., h, :, h, :] for h in range(B_HEADS)], axis=-3)


def kernel(x_prompt, x_sample, cache_k, cache_v, state_gla, state_delta, c, c_ctx, w_mod, b_mod, norm_gains, w_in, qk_gain, w_gla_gate, b_gla_gate, gla_norm, conv_w, a_log, dt_bias, delta_norm, w_out, w_gate, w_up, w_down):
    xc = x_prompt.reshape(N_CTX_ROWS, D_MODEL)
    xl = x_sample.reshape(N_LAT_ROWS, D_MODEL)
    cond = jnp.concatenate([c_ctx[None, :], c, jnp.zeros((N_COND - 1 - DEC_BATCH, D_MODEL), F32)], axis=0)
    nat_small = A_WIDTH + 2 * A_KV_WIDTH + 2 * B_QK_WIDTH + 2 * B_WIDTH
    nat_c = nat_small + N_DIR * GATE_RANK
    nat_bc = nat_c + ZC_W
    w_in_p = jnp.concatenate(
        [w_in[:, :, 0:nat_small], w_in[:, :, nat_small:nat_c], w_in[:, :, nat_bc:nat_bc + 2 * N_DIR * C_HEADS],
         jnp.zeros((DEPTH, D_MODEL, LANES - N_DIR * GATE_RANK - 2 * N_DIR * C_HEADS), w_in.dtype),
         w_in[:, :, nat_c:nat_bc]], axis=-1).astype(BF16)
    w_out_b = w_out.astype(BF16)
    w_gate_b = w_gate.astype(BF16)
    w_up_b = w_up.astype(BF16)
    w_down_b = w_down.astype(BF16)
    qk_gain2 = jnp.tile(qk_gain, (1, 1, LANES // HEAD_DIM))
    wg_p = jnp.zeros((DEPTH, LANES, N_DIR * B_QK_WIDTH), F32)
    for d in range(N_DIR):
        wg_p = wg_p.at[:, d * GATE_RANK:(d + 1) * GATE_RANK, d * B_QK_WIDTH:(d + 1) * B_QK_WIDTH].set(w_gla_gate[:, d])
    bg_p = b_gla_gate.reshape(DEPTH, 1, N_DIR * B_QK_WIDTH)
    gn_p = jnp.tile(gla_norm, (1, B_HEADS)).reshape(DEPTH, 1, B_WIDTH)
    dn_p = jnp.tile(delta_norm, (1, C_HEADS)).reshape(DEPTH, 1, C_WIDTH)
    al_p = jnp.zeros((DEPTH, 1, LANES), F32).at[:, 0, DECAY_LANE:DECAY_LANE + N_DIR * C_HEADS].set(
        a_log.reshape(DEPTH, N_DIR * C_HEADS))
    dtb_p = jnp.zeros((DEPTH, 1, LANES), F32).at[:, 0, DECAY_LANE:DECAY_LANE + N_DIR * C_HEADS].set(
        dt_bias.reshape(DEPTH, N_DIR * C_HEADS))
    cache_k2 = cache_k.reshape(DEC_BATCH, DEPTH, PAST_LEN, A_KV_WIDTH)
    cache_v2 = cache_v.reshape(DEC_BATCH, DEPTH, PAST_LEN, A_KV_WIDTH)
    s0_gla_bd = _block_diag_state(state_gla.astype(F32))
    s0_delta = _pair_state(state_delta.astype(F32))
    rope = _rope_tables()

    mods = _modulation(cond, w_mod, b_mod).reshape(DEPTH * N_COND, 1, 6 * D_MODEL)

    new_k, new_v, new_gla, new_delta = [], [], [], []
    for l in range(DEPTH):
        za, zb, zc = _in_projection(xc, xl, mods, norm_gains, w_in_p, l)
        oa_c, k_l, v_l = _attention(za, qk_gain2, l, False)
        (oa_l,) = _attention(za, qk_gain2, l, True, cache_k2, cache_v2, rope)
        ob_c, sg_l = _gla(zb, wg_p, bg_p, gn_p, l, False)
        (ob_l,) = _gla(zb, wg_p, bg_p, gn_p, l, True, s0_gla_bd)
        oc_c, sd_l = _delta(zc, zb, conv_w, al_p, dtb_p, dn_p, l, False)
        (oc_l,) = _delta(zc, zb, conv_w, al_p, dtb_p, dn_p, l, True, s0_delta)
        xc, xl = _out_ffn(xc, xl, (oa_c, oa_l, ob_c, ob_l, oc_c, oc_l), mods, norm_gains,
                          w_out_b, w_gate_b, w_up_b, w_down_b, l)
        new_k.append(k_l.reshape(BATCH, SEQ, A_KV_HEADS, HEAD_DIM))
        new_v.append(v_l.reshape(BATCH, SEQ, A_KV_HEADS, HEAD_DIM))
        new_gla.append(_diag_blocks(sg_l))
        new_delta.append(_unpair_state(sd_l))

    out_dtype = x_prompt.dtype
    y_prompt = xc.reshape(BATCH, SEQ, D_MODEL)
    y_sample = xl.reshape(DEC_BATCH, DEC_SEQ, D_MODEL)
    return (y_prompt, y_sample,
            jnp.stack(new_k, axis=1), jnp.stack(new_v, axis=1),
            jnp.stack(new_gla, axis=1).astype(out_dtype),
            jnp.stack(new_delta, axis=1).astype(out_dtype))
```

```python
import functools

import jax
import jax.numpy as jnp
import numpy as np
from jax import lax
from jax.experimental import pallas as pl
from jax.experimental.pallas import tpu as pltpu

F32 = jnp.float32
BF16 = jnp.bfloat16

D_MODEL = 1024
BATCH = 16
SEQ = 256
DEPTH = 4
DEC_BATCH = 4
DEC_SEQ = 1024
PAST_LEN = 512
GRID_W = 64
HEAD_DIM = 64
A_HEADS = 8
A_KV_HEADS = 2
A_REP = A_HEADS // A_KV_HEADS
A_WIDTH = A_HEADS * HEAD_DIM
A_KV_WIDTH = A_KV_HEADS * HEAD_DIM
ROPE_THETA = 10000.0
B_HEADS = 4
B_DK = 32
B_DV = 64
B_QK_WIDTH = B_HEADS * B_DK
B_WIDTH = B_HEADS * B_DV
GATE_RANK = 16
GATE_TAU = 16.0
C_HEADS = 4
C_DK = 64
C_DV = 64
C_QK_WIDTH = C_HEADS * C_DK
C_WIDTH = C_HEADS * C_DV
CONV_WIDTH = 5
CONV_CH = 2 * C_QK_WIDTH + C_WIDTH
CHUNK = 64
N_DIR = 2
MIX_WIDTH = A_WIDTH + B_WIDTH + C_WIDTH
D_FF = -(-(8 * D_MODEL) // (3 * 256)) * 256
EPS = 1e-6

LANES = 128
N_CTX_ROWS = BATCH * SEQ
N_LAT_ROWS = DEC_BATCH * DEC_SEQ
N_ROWS = N_CTX_ROWS + N_LAT_ROWS
N_COND = 8

ZA_W = A_WIDTH + 2 * A_KV_WIDTH
ZB_W = 2 * B_QK_WIDTH + 2 * B_WIDTH + LANES
ZC_W = 2 * C_QK_WIDTH + 2 * C_WIDTH
PROJ_PAD = ZA_W + ZB_W + ZC_W
SMALL_OFF = 2 * B_QK_WIDTH + 2 * B_WIDTH
NAT_SMALL = A_WIDTH + 2 * A_KV_WIDTH + 2 * B_QK_WIDTH + 2 * B_WIDTH
NAT_C = NAT_SMALL + N_DIR * GATE_RANK
NAT_BC = NAT_C + ZC_W
BETA_LANE = N_DIR * GATE_RANK
DECAY_LANE = BETA_LANE + N_DIR * C_HEADS

VMEM_LIMIT = 56 * 1024 * 1024


def _split(x):
    hi = x.astype(BF16)
    lo = (x - hi.astype(F32)).astype(BF16)
    return hi, lo


def _bdot(a, b):
    return jnp.dot(a.astype(BF16), b.astype(BF16), preferred_element_type=F32)


def _bdot_nt(a, b):
    return lax.dot_general(a.astype(BF16), b.astype(BF16), (((1,), (1,)), ((), ())),
                           preferred_element_type=F32)


def _bdot_tn(a, b):
    return lax.dot_general(a.astype(BF16), b.astype(BF16), (((0,), (0,)), ((), ())),
                           preferred_element_type=F32)


def _dot3(a, b):
    ah, al = _split(a)
    bh, bl = _split(b)
    return (jnp.dot(ah, bh, preferred_element_type=F32)
            + jnp.dot(ah, bl, preferred_element_type=F32)
            + jnp.dot(al, bh, preferred_element_type=F32))


def _dot_lhs_exact(m_bf16, x):
    xh, xl = _split(x)
    return (jnp.dot(m_bf16, xh, preferred_element_type=F32)
            + jnp.dot(m_bf16, xl, preferred_element_type=F32))


def _dot_rhs_exact(x, m_bf16):
    xh, xl = _split(x)
    return (jnp.dot(xh, m_bf16, preferred_element_type=F32)
            + jnp.dot(xl, m_bf16, preferred_element_type=F32))


def _group_sumsq(x, ones_g):
    return jnp.dot((x * x).astype(BF16), ones_g, preferred_element_type=F32)


def _group_ones(width, group):
    r = lax.broadcasted_iota(jnp.int32, (width, width), 0) // group
    c = lax.broadcasted_iota(jnp.int32, (width, width), 1) // group
    return (r == c).astype(BF16)


def _sigmoid(x):
    return 1.0 / (1.0 + jnp.exp(-x))


def _silu(x):
    return x * _sigmoid(x)


def _softplus(x):
    return jnp.maximum(x, 0.0) + jnp.log1p(jnp.exp(-jnp.abs(x)))


def _rms(x, g):
    return x * lax.rsqrt(jnp.mean(x * x, axis=-1, keepdims=True) + EPS) * g


def _chunk_tri(t, reverse):
    r = lax.broadcasted_iota(jnp.int32, (t, t), 0)
    c = lax.broadcasted_iota(jnp.int32, (t, t), 1)
    same = (r // CHUNK) == (c // CHUNK)
    tri = (c >= r) if reverse else (c <= r)
    return (same & tri).astype(BF16)


def _mod_row(i, tm):
    start = i * tm
    return jnp.where(start < N_CTX_ROWS, 0, 1 + (start - N_CTX_ROWS) // DEC_SEQ)


def _mod_kernel(cond_ref, w_ref, b_ref, o_ref):
    c = cond_ref[...]
    o_ref[0] = _bdot(_silu(c), w_ref[0]) + b_ref[0]


def _modulation(cond, w_mod, b_mod):
    tn = 1536
    return pl.pallas_call(
        _mod_kernel,
        grid=(DEPTH, 6 * D_MODEL // tn),
        in_specs=[pl.BlockSpec((N_COND, D_MODEL), lambda l, j: (0, 0)),
                  pl.BlockSpec((1, D_MODEL, tn), lambda l, j: (l, 0, j)),
                  pl.BlockSpec((1, 1, tn), lambda l, j: (l, 0, j))],
        out_specs=pl.BlockSpec((1, N_COND, tn), lambda l, j: (l, 0, j)),
        out_shape=jax.ShapeDtypeStruct((DEPTH, N_COND, 6 * D_MODEL), F32),
        compiler_params=pltpu.CompilerParams(vmem_limit_bytes=VMEM_LIMIT),
        name="adaln_mod",
    )(cond, w_mod, b_mod.reshape(DEPTH, 1, 6 * D_MODEL))


DENSE_TM = 512
N_CTX_BLOCKS = N_CTX_ROWS // DENSE_TM


def _ctx_map(i):
    return (jnp.minimum(i, N_CTX_BLOCKS - 1), 0)


def _lat_map(i):
    return (jnp.maximum(i - N_CTX_BLOCKS, 0), 0)


def _inproj_kernel(xc_ref, xl_ref, mod_ref, ng_ref, w_ref, za_ref, zb_ref, zc_ref, wp_s):
    @pl.when(pl.program_id(0) == 0)
    def _():
        n_gate = N_DIR * GATE_RANK
        n_bc = 2 * N_DIR * C_HEADS
        wp_s[:, 0:NAT_SMALL] = w_ref[0, :, 0:NAT_SMALL].astype(BF16)
        wp_s[:, NAT_SMALL:NAT_SMALL + n_gate] = w_ref[0, :, NAT_SMALL:NAT_C].astype(BF16)
        wp_s[:, NAT_SMALL + n_gate:NAT_SMALL + n_gate + n_bc] = w_ref[0, :, NAT_BC:NAT_BC + n_bc].astype(BF16)
        wp_s[:, NAT_SMALL + n_gate + n_bc:NAT_SMALL + LANES] = jnp.zeros(
            (D_MODEL, LANES - n_gate - n_bc), BF16)
        wp_s[:, NAT_SMALL + LANES:PROJ_PAD] = w_ref[0, :, NAT_C:NAT_BC].astype(BF16)

    x = jnp.where(pl.program_id(0) < N_CTX_BLOCKS, xc_ref[...], xl_ref[...])
    mod = mod_ref[0]
    shift = mod[:, 0:D_MODEL]
    scale = mod[:, D_MODEL:2 * D_MODEL]
    h = _rms(x, ng_ref[0, 0:1, :]) * (1.0 + scale) + shift
    z = jnp.dot(h.astype(BF16), wp_s[...], preferred_element_type=F32)
    za_ref[...] = z[:, 0:ZA_W]
    zb_ref[...] = z[:, ZA_W:ZA_W + ZB_W]
    zc_ref[...] = z[:, ZA_W + ZB_W:PROJ_PAD]


def _in_projection(xc, xl, mods, norm_gains, w_in, layer):
    tm = DENSE_TM
    proj_width = w_in.shape[-1]
    return pl.pallas_call(
        _inproj_kernel,
        grid=(N_ROWS // tm,),
        in_specs=[pl.BlockSpec((tm, D_MODEL), _ctx_map),
                  pl.BlockSpec((tm, D_MODEL), _lat_map),
                  pl.BlockSpec((1, 1, 6 * D_MODEL), lambda i: (layer * N_COND + _mod_row(i, tm), 0, 0)),
                  pl.BlockSpec((1, 4, D_MODEL), lambda i: (layer, 0, 0)),
                  pl.BlockSpec((1, D_MODEL, proj_width), lambda i: (layer, 0, 0),
                               pipeline_mode=pl.Buffered(1))],
        out_specs=[pl.BlockSpec((tm, ZA_W), lambda i: (i, 0)),
                   pl.BlockSpec((tm, ZB_W), lambda i: (i, 0)),
                   pl.BlockSpec((tm, ZC_W), lambda i: (i, 0))],
        out_shape=[jax.ShapeDtypeStruct((N_ROWS, ZA_W), F32),
                   jax.ShapeDtypeStruct((N_ROWS, ZB_W), F32),
                   jax.ShapeDtypeStruct((N_ROWS, ZC_W), F32)],
        scratch_shapes=[pltpu.VMEM((D_MODEL, PROJ_PAD), BF16)],
        compiler_params=pltpu.CompilerParams(
            dimension_semantics=("arbitrary",), vmem_limit_bytes=VMEM_LIMIT),
        name="in_proj",
    )(xc, xl, mods, norm_gains, w_in)


FF_TILE = 256


def _ffn_kernel(xc_ref, xl_ref, oac_ref, oal_ref, obc_ref, obl_ref, occ_ref, ocl_ref, mod_ref, ng_ref,
                wo_ref, wg_ref, wu_ref, wd_ref, yc_ref, yl_ref):
    is_ctx = pl.program_id(0) < N_CTX_BLOCKS
    x = jnp.where(is_ctx, xc_ref[...], xl_ref[...])
    mod = mod_ref[0]
    gate_m = mod[:, 2 * D_MODEL:3 * D_MODEL]
    shift_f = mod[:, 3 * D_MODEL:4 * D_MODEL]
    scale_f = mod[:, 4 * D_MODEL:5 * D_MODEL]
    gate_f = mod[:, 5 * D_MODEL:6 * D_MODEL]
    mix_in = jnp.concatenate([jnp.where(is_ctx, oac_ref[...], oal_ref[...]),
                              jnp.where(is_ctx, obc_ref[...], obl_ref[...]),
                              jnp.where(is_ctx, occ_ref[...], ocl_ref[...])], axis=-1)
    mix = _bdot(mix_in, wo_ref[0])
    x1 = x + gate_m * _rms(mix, ng_ref[0, 1:2, :])
    h = (_rms(x1, ng_ref[0, 2:3, :]) * (1.0 + scale_f) + shift_f).astype(BF16)
    f = jnp.zeros(x.shape, F32)
    for j in range(D_FF // FF_TILE):
        cols = slice(j * FF_TILE, (j + 1) * FF_TILE)
        g = jnp.dot(h, wg_ref[0, :, cols], preferred_element_type=F32)
        u = jnp.dot(h, wu_ref[0, :, cols], preferred_element_type=F32)
        f = f + _bdot(_silu(g) * u, wd_ref[0, cols, :])
    y = x1 + gate_f * _rms(f, ng_ref[0, 3:4, :])

    @pl.when(is_ctx)
    def _():
        yc_ref[...] = y

    @pl.when(jnp.logical_not(is_ctx))
    def _():
        yl_ref[...] = y


def _out_ffn(xc, xl, mixer_outs, mods, norm_gains, w_out_b, w_gate_b, w_up_b, w_down_b, layer):
    tm = DENSE_TM
    resident = dict(pipeline_mode=pl.Buffered(1))
    pair_specs = []
    for width in (D_MODEL, A_WIDTH, B_WIDTH, C_WIDTH):
        pair_specs += [pl.BlockSpec((tm, width), _ctx_map), pl.BlockSpec((tm, width), _lat_map)]
    return pl.pallas_call(
        _ffn_kernel,
        grid=(N_ROWS // tm,),
        in_specs=pair_specs + [
                  pl.BlockSpec((1, 1, 6 * D_MODEL), lambda i: (layer * N_COND + _mod_row(i, tm), 0, 0)),
                  pl.BlockSpec((1, 4, D_MODEL), lambda i: (layer, 0, 0)),
                  pl.BlockSpec((1, MIX_WIDTH, D_MODEL), lambda i: (layer, 0, 0), **resident),
                  pl.BlockSpec((1, D_MODEL, D_FF), lambda i: (layer, 0, 0), **resident),
                  pl.BlockSpec((1, D_MODEL, D_FF), lambda i: (layer, 0, 0), **resident),
                  pl.BlockSpec((1, D_FF, D_MODEL), lambda i: (layer, 0, 0), **resident)],
        out_specs=[pl.BlockSpec((tm, D_MODEL), _ctx_map), pl.BlockSpec((tm, D_MODEL), _lat_map)],
        out_shape=[jax.ShapeDtypeStruct((N_CTX_ROWS, D_MODEL), F32),
                   jax.ShapeDtypeStruct((N_LAT_ROWS, D_MODEL), F32)],
        compiler_params=pltpu.CompilerParams(
            dimension_semantics=("arbitrary",), vmem_limit_bytes=VMEM_LIMIT),
        name="out_ffn",
    )(xc, xl, *mixer_outs, mods, norm_gains, w_out_b, w_gate_b, w_up_b, w_down_b)


ATT_TQ = 256
ATT_LOOKAHEAD = 1


def _head_norm(x, gain_row, ones_g):
    ss = _group_sumsq(x, ones_g)
    return x * lax.rsqrt(ss * (1.0 / HEAD_DIM) + EPS) * gain_row


def _rope(x, cos, sin_a, sin_b):
    return x * cos + pltpu.roll(x, LANES - 16, 1) * sin_a + pltpu.roll(x, 16, 1) * sin_b


def _attn_kernel(*refs, seq_len, latent):
    if latent:
        (q_ref, kv_ref, ck_ref, cv_ref, qkg_ref, cos_ref, sa_ref, sb_ref, o_ref, k_s, v_s) = refs
    else:
        (q_ref, kv_ref, qkg_ref, o_ref, kn_ref, vn_ref, k_s, v_s) = refs
    j = pl.program_id(1)
    ones_g = _group_ones(LANES, HEAD_DIM)
    gq = qkg_ref[0, 0:1, :]
    gk = qkg_ref[0, 1:2, :]
    past = PAST_LEN if latent else 0

    @pl.when(j == 0)
    def _():
        ka = kv_ref[:, 0:A_KV_WIDTH]
        va = kv_ref[:, A_KV_WIDTH:2 * A_KV_WIDTH]
        kn = _head_norm(ka, gk, ones_g)
        if latent:
            kr = _rope(kn, cos_ref[...], sa_ref[...], sb_ref[...])
            ck = ck_ref[...]
            cv = cv_ref[...]
            for g in range(A_KV_HEADS):
                cols = slice(g * HEAD_DIM, (g + 1) * HEAD_DIM)
                k_s[g, 0:past, :] = ck[:, cols].astype(BF16)
                v_s[g, 0:past, :] = cv[:, cols].astype(BF16)
                k_s[g, past:past + seq_len, :] = kr[:, cols].astype(BF16)
                v_s[g, past:past + seq_len, :] = va[:, cols].astype(BF16)
        else:
            kn_ref[...] = kn
            vn_ref[...] = va
            for g in range(A_KV_HEADS):
                cols = slice(g * HEAD_DIM, (g + 1) * HEAD_DIM)
                k_s[g] = kn[:, cols].astype(BF16)
                v_s[g] = va[:, cols].astype(BF16)

    if latent:
        rows = pl.ds(pl.multiple_of(j * ATT_TQ, ATT_TQ), ATT_TQ)
        cos = cos_ref[rows, :]
        sin_a = sa_ref[rows, :]
        sin_b = sb_ref[rows, :]
    q_heads = []
    for t in range(A_WIDTH // LANES):
        qt = _head_norm(q_ref[:, t * LANES:(t + 1) * LANES], gq, ones_g)
        if latent:
            qt = _rope(qt, cos, sin_a, sin_b)
        qt = (qt * (HEAD_DIM ** -0.5)).astype(BF16)
        q_heads += [qt[:, hh * HEAD_DIM:(hh + 1) * HEAD_DIM] for hh in range(LANES // HEAD_DIM)]

    def scores(h):
        return lax.dot_general(q_heads[h], k_s[h // A_REP], (((1,), (1,)), ((), ())),
                               preferred_element_type=F32)

    outs = []
    pending = [scores(h) for h in range(ATT_LOOKAHEAD)]
    for h in range(A_HEADS):
        s = pending.pop(0)
        if h + ATT_LOOKAHEAD < A_HEADS:
            pending.append(scores(h + ATT_LOOKAHEAD))
        m = jnp.max(s, axis=-1, keepdims=True)
        p = jnp.exp(s - m)
        l = jnp.sum(p, axis=-1, keepdims=True)
        o = jnp.dot(p.astype(BF16), v_s[h // A_REP], preferred_element_type=F32)
        outs.append(o / l)
    o_ref[...] = jnp.concatenate(outs, axis=-1)


def _attention(za, qk_gain2, layer, latent, cache_k=None, cache_v=None, rope=None):
    seq_len = DEC_SEQ if latent else SEQ
    n_seq = DEC_BATCH if latent else BATCH
    row0 = N_CTX_ROWS // seq_len if latent else 0
    nq = seq_len // ATT_TQ
    row0q = N_CTX_ROWS // ATT_TQ if latent else 0
    s_len = seq_len + (PAST_LEN if latent else 0)
    in_specs = [pl.BlockSpec((ATT_TQ, A_WIDTH), lambda s, j: (row0q + s * nq + j, 0)),
                pl.BlockSpec((seq_len, 2 * A_KV_WIDTH), lambda s, j: (row0 + s, 2))]
    args = [za, za]
    if latent:
        in_specs += [pl.BlockSpec((None, None, PAST_LEN, A_KV_WIDTH), lambda s, j: (s, layer, 0, 0)),
                     pl.BlockSpec((None, None, PAST_LEN, A_KV_WIDTH), lambda s, j: (s, layer, 0, 0))]
        args += [cache_k, cache_v]
    in_specs.append(pl.BlockSpec((1, 2, LANES), lambda s, j: (layer, 0, 0)))
    args.append(qk_gain2)
    if latent:
        in_specs += [pl.BlockSpec((seq_len, LANES), lambda s, j: (0, 0))] * 3
        args += list(rope)
    out_specs = [pl.BlockSpec((ATT_TQ, A_WIDTH), lambda s, j: (s * nq + j, 0))]
    out_shape = [jax.ShapeDtypeStruct((n_seq * seq_len, A_WIDTH), F32)]
    if not latent:
        out_specs += [pl.BlockSpec((None, seq_len, A_KV_WIDTH), lambda s, j: (s, 0, 0))] * 2
        out_shape += [jax.ShapeDtypeStruct((n_seq, seq_len, A_KV_WIDTH), F32)] * 2
    return pl.pallas_call(
        functools.partial(_attn_kernel, seq_len=seq_len, latent=latent),
        grid=(n_seq, nq),
        in_specs=in_specs,
        out_specs=out_specs,
        out_shape=out_shape,
        scratch_shapes=[pltpu.VMEM((A_KV_HEADS, s_len, HEAD_DIM), BF16),
                        pltpu.VMEM((A_KV_HEADS, s_len, HEAD_DIM), BF16)],
        compiler_params=pltpu.CompilerParams(
            dimension_semantics=("arbitrary", "arbitrary"), vmem_limit_bytes=VMEM_LIMIT),
        name="attn_latent" if latent else "attn_ctx",
    )(*args)


def _gla_kernel(*refs, seq_len, latent):
    if latent:
        zb_ref, wg_ref, bg_ref, gn_ref, s0_ref, ob_ref = refs
    else:
        zb_ref, wg_ref, bg_ref, gn_ref, ob_ref, sfin_ref = refs
    nc = seq_len // CHUNK
    q = zb_ref[:, 0:B_QK_WIDTH] * (B_DK ** -0.5)
    k = zb_ref[:, B_QK_WIDTH:2 * B_QK_WIDTH]
    v = zb_ref[:, 2 * B_QK_WIDTH:2 * B_QK_WIDTH + B_WIDTH]
    small = zb_ref[:, SMALL_OFF:SMALL_OFF + LANES]
    pre = _dot3(small, wg_ref[0]) + bg_ref[0]
    glog = (jnp.minimum(pre, 0.0) - jnp.log1p(jnp.exp(-jnp.abs(pre)))) * (1.0 / GATE_TAU)

    kr = lax.broadcasted_iota(jnp.int32, (B_QK_WIDTH, B_WIDTH), 0) // B_DK
    vc = lax.broadcasted_iota(jnp.int32, (B_QK_WIDTH, B_WIDTH), 1) // B_DV
    bd_kv = kr == vc
    sr = lax.broadcasted_iota(jnp.int32, (B_WIDTH, B_WIDTH), 0) // CHUNK
    sc = lax.broadcasted_iota(jnp.int32, (B_WIDTH, B_WIDTH), 1) // B_DV
    bd_sv = sr == sc
    tt = lax.broadcasted_iota(jnp.int32, (CHUNK, B_WIDTH), 0)
    ss = lax.broadcasted_iota(jnp.int32, (CHUNK, B_WIDTH), 1) % CHUNK

    q3 = q.reshape(nc, CHUNK, B_QK_WIDTH)
    k3 = k.reshape(nc, CHUNK, B_QK_WIDTH)
    intra, upds, decays, q_sts = [], [], [], []
    for d in range(N_DIR):
        g = glog[:, d * B_QK_WIDTH:(d + 1) * B_QK_WIDTH]
        b = _dot_lhs_exact(_chunk_tri(seq_len, d == 1), g)
        b3 = b.reshape(nc, CHUNK, B_QK_WIDTH)
        mid = b3[:, CHUNK // 2:CHUNK // 2 + 1, :]
        last = b3[:, 0:1, :] if d == 1 else b3[:, CHUNK - 1:CHUNK, :]
        q_in = q3 * jnp.exp(b3 - mid)
        k_in = k3 * jnp.exp(mid - b3)
        q_sts.append(q3 * jnp.exp(b3))
        k_st = k3 * jnp.exp(last - b3)
        decay_t = jnp.exp(jnp.broadcast_to(last, (nc, 8, B_QK_WIDTH)).reshape(nc * 8, B_QK_WIDTH)).T
        causal = (ss >= tt) if d == 1 else (ss <= tt)
        v_chunks = [v[c * CHUNK:(c + 1) * CHUNK, :] for c in range(nc)]
        atts = [_bdot(q_in[c], jnp.where(bd_kv, jnp.concatenate([k_in[c].T] * B_HEADS, axis=1), 0.0))
                for c in range(nc)]
        upds.append([jnp.where(bd_kv, _bdot_tn(k_st[c], v_chunks[c]), 0.0) for c in range(nc)])
        intra.append([_bdot(jnp.where(causal, atts[c], 0.0),
                            jnp.where(bd_sv, jnp.concatenate([v_chunks[c]] * B_HEADS, axis=0), 0.0))
                      for c in range(nc)])
        decays.append([decay_t[:, 8 * c:8 * c + 1] for c in range(nc)])
    o_dirs = []
    for d in range(N_DIR):
        state = s0_ref[d] if latent else jnp.zeros((B_QK_WIDTH, B_WIDTH), F32)
        o_chunks = [None] * nc
        for c in (range(nc - 1, -1, -1) if d == 1 else range(nc)):
            o_chunks[c] = intra[d][c] + _bdot(q_sts[d][c], state)
            state = decays[d][c] * state + upds[d][c]
        o_dirs.append(jnp.concatenate(o_chunks, axis=0))
        if not latent:
            sfin_ref[d] = state
    o = o_dirs[0] + o_dirs[1]
    ms = _group_sumsq(o, _group_ones(B_WIDTH, B_DV)) * (1.0 / B_DV)
    r = zb_ref[:, 2 * B_QK_WIDTH + B_WIDTH:2 * B_QK_WIDTH + 2 * B_WIDTH]
    ob_ref[...] = o * lax.rsqrt(ms + EPS) * gn_ref[0] * _silu(r)


def _gla(zb, wg_p, bg_p, gn_p, layer, latent, s0_bd=None):
    seq_len = DEC_SEQ if latent else SEQ
    n_seq = DEC_BATCH if latent else BATCH
    row0 = N_CTX_ROWS // seq_len if latent else 0
    in_specs = [pl.BlockSpec((seq_len, ZB_W), lambda s: (row0 + s, 0)),
                pl.BlockSpec((1, LANES, N_DIR * B_QK_WIDTH), lambda s: (layer, 0, 0)),
                pl.BlockSpec((1, 1, N_DIR * B_QK_WIDTH), lambda s: (layer, 0, 0)),
                pl.BlockSpec((1, 1, B_WIDTH), lambda s: (layer, 0, 0))]
    args = [zb, wg_p, bg_p, gn_p]
    out_specs = [pl.BlockSpec((seq_len, B_WIDTH), lambda s: (s, 0))]
    out_shape = [jax.ShapeDtypeStruct((n_seq * seq_len, B_WIDTH), F32)]
    if latent:
        in_specs.append(pl.BlockSpec((None, None, N_DIR, B_QK_WIDTH, B_WIDTH), lambda s: (s, layer, 0, 0, 0)))
        args.append(s0_bd)
    else:
        out_specs.append(pl.BlockSpec((None, N_DIR, B_QK_WIDTH, B_WIDTH), lambda s: (s, 0, 0, 0)))
        out_shape.append(jax.ShapeDtypeStruct((n_seq, N_DIR, B_QK_WIDTH, B_WIDTH), F32))
    return pl.pallas_call(
        functools.partial(_gla_kernel, seq_len=seq_len, latent=latent),
        grid=(n_seq,),
        in_specs=in_specs,
        out_specs=out_specs,
        out_shape=out_shape,
        compiler_params=pltpu.CompilerParams(vmem_limit_bytes=VMEM_LIMIT),
        name="gla_latent" if latent else "gla_ctx",
    )(*args)


TRI_BASE = 8
C_PAIRS = C_HEADS // 2
N_SYS = N_DIR * C_PAIRS
PREP_CHUNKS = 4
CUM_ROWS = 256
CONV_PAD = 8


def _bd(y, mask):
    return jnp.where(mask, jnp.concatenate([y, y], axis=0), 0.0).astype(BF16)


def _pair_tri_inverse(ms, bd_mask, tt, ss):
    pws = [jnp.where((tt // TRI_BASE) == (ss // TRI_BASE), -m, 0.0) for m in ms]
    invs = [(tt == ss).astype(F32) + pw for pw in pws]
    span = 2
    while span < TRI_BASE:
        pws = [_bdot(pw, _bd(pw, bd_mask)) for pw in pws]
        invs = [inv + _bdot(inv, _bd(pw, bd_mask)) for inv, pw in zip(invs, pws)]
        span *= 2
    size = 2 * TRI_BASE
    while size <= CHUNK:
        off = ((tt // size) == (ss // size)) & ((tt // (size // 2)) != (ss // (size // 2)))
        cxs = [_bdot(jnp.where(off, m, 0.0), _bd(inv, bd_mask)) for m, inv in zip(ms, invs)]
        invs = [inv - _bdot(inv, _bd(cx, bd_mask)) for inv, cx in zip(invs, cxs)]
        size *= 2
    return invs


def _delta_kernel(*refs, seq_len, latent):
    if latent:
        (zc_ref, sm_ref, cw_ref, al_ref, dtb_ref, dn_ref, s0_ref, oc_ref,
         xp_s, yp_s, q_s, k_s, v_s, bx_s, gx_s, gr_s, o_s, u_s, wq_s, ak_s, eg_s) = refs
    else:
        (zc_ref, sm_ref, cw_ref, al_ref, dtb_ref, dn_ref, oc_ref, sfin_ref,
         xp_s, yp_s, q_s, k_s, v_s, bx_s, gx_s, gr_s, o_s, u_s, wq_s, ak_s, eg_s) = refs
    nc = seq_len // CHUNK

    seg = seq_len // 8 + 1
    sub = lax.broadcasted_iota(jnp.int32, (8, LANES), 0)
    half = CONV_WIDTH // 2
    for j in range(CONV_CH // LANES):
        cols = slice(j * LANES, (j + 1) * LANES)
        xp_s[j, 0:seq_len, :] = zc_ref[:, cols]
        xp_s[j, seq_len:8 * seg, :] = jnp.zeros((8 * seg - seq_len, LANES), F32)
        taps = [cw_ref[0, tap:tap + 1, cols] for tap in range(CONV_WIDTH)]
        xr = [xp_s[j, pl.ds(a, 8, stride=seg), :] for a in range(seg)]
        nxt = [jnp.where(sub < 7, pltpu.roll(xr[a], 7, 0), 0.0) for a in range(half)]
        prv = [jnp.where(sub > 0, pltpu.roll(xr[seg - half + a], 1, 0), 0.0) for a in range(half)]
        window = prv + xr + nxt
        for a in range(seg):
            acc = window[a] * taps[0]
            for tap in range(1, CONV_WIDTH):
                acc = acc + window[a + tap] * taps[tap]
            yp_s[j, pl.ds(a, 8, stride=seg), :] = acc
    y = _silu(jnp.concatenate([yp_s[j, 0:seq_len, :] for j in range(CONV_CH // LANES)], axis=1))
    ones_g = _group_ones(C_QK_WIDTH, C_DK)
    qc = y[:, 0:C_QK_WIDTH]
    kc = y[:, C_QK_WIDTH:2 * C_QK_WIDTH]
    q_s[...] = qc * lax.rsqrt(_group_sumsq(qc, ones_g) + EPS) * (C_DK ** -0.5)
    k_s[...] = kc * lax.rsqrt(_group_sumsq(kc, ones_g) + EPS)
    v_s[...] = y[:, 2 * C_QK_WIDTH:CONV_CH]

    small = sm_ref[...]
    beta = _sigmoid(small)
    glog = -jnp.exp(al_ref[0]) * _softplus(small + dtb_ref[0])
    lane_r = lax.broadcasted_iota(jnp.int32, (LANES, C_WIDTH), 0)
    head_c = lax.broadcasted_iota(jnp.int32, (LANES, C_WIDTH), 1) // C_DV
    cr = lax.broadcasted_iota(jnp.int32, (CUM_ROWS, CUM_ROWS), 0)
    cl = lax.broadcasted_iota(jnp.int32, (CUM_ROWS, CUM_ROWS), 1)
    same_chunk = ((cr // CHUNK) == (cl // CHUNK)).astype(BF16)
    on_diag = (lax.broadcasted_iota(jnp.int32, (CUM_ROWS, C_WIDTH), 0) % CHUNK
               == lax.broadcasted_iota(jnp.int32, (CUM_ROWS, C_WIDTH), 1) % C_DV)
    for d in range(N_DIR):
        bx_s[d] = _dot_rhs_exact(beta, (lane_r == BETA_LANE + d * C_HEADS + head_c).astype(BF16))
        pick_gam = (lane_r == DECAY_LANE + d * C_HEADS + head_c).astype(BF16)
        tri = _chunk_tri(CUM_ROWS, d == 1)
        for blk in range(seq_len // CUM_ROWS):
            r = slice(blk * CUM_ROWS, (blk + 1) * CUM_ROWS)
            g_x = _dot_rhs_exact(_dot_lhs_exact(tri, glog[r, :]), pick_gam)
            gx_s[d, r, :] = g_x
            gr_s[d, r, :] = _dot_lhs_exact(same_chunk, jnp.where(on_diag, g_x, 0.0))

    tt = lax.broadcasted_iota(jnp.int32, (CHUNK, LANES), 0)
    ss = lax.broadcasted_iota(jnp.int32, (CHUNK, LANES), 1) % CHUNK
    bd1 = (lax.broadcasted_iota(jnp.int32, (LANES, LANES), 0) // CHUNK
           == lax.broadcasted_iota(jnp.int32, (LANES, LANES), 1) // CHUNK)
    bd2 = jnp.concatenate([bd1, bd1], axis=1)

    prep = min(PREP_CHUNKS, nc)

    def prepare_chunks(step, carry):
        ids, ms, rhss, qgs, aks, egs = [], [], [], [], [], []
        for cj in range(prep):
            c = step * prep + cj
            rows = pl.ds(pl.multiple_of(c * CHUNK, CHUNK), CHUNK)
            k_t = k_s[rows, :].T
            for d in range(N_DIR):
                ahead = (tt - ss) if d == 1 else (ss - tt)
                last = 0 if d == 1 else CHUNK - 1
                for p in range(C_PAIRS):
                    lanes = slice(p * LANES, (p + 1) * LANES)
                    qp = q_s[rows, lanes]
                    kp = k_s[rows, lanes]
                    bx = bx_s[d, rows, lanes]
                    gx = gx_s[d, rows, lanes]
                    gr = gr_s[d, rows, lanes]
                    dec = jnp.exp(jnp.where(ahead <= 0, gx - gr, -jnp.inf))
                    kb = kp * bx
                    ma = lax.dot_general(jnp.concatenate([kb, qp], axis=0).astype(BF16), _bd(kp, bd1),
                                         (((1,), (1,)), ((), ())), preferred_element_type=F32)
                    e_gx = jnp.exp(gx)
                    k_t_pair = jnp.concatenate([k_t[p * LANES:p * LANES + C_DK, :],
                                                k_t[p * LANES + C_DK:(p + 1) * LANES, :]], axis=1)
                    g_last = gx[last:last + 1, :]
                    ids.append((c, d * C_PAIRS + p))
                    ms.append(jnp.where(ahead < 0, ma[0:CHUNK] * dec, 0.0))
                    rhss.append(jnp.concatenate([v_s[rows, lanes] * bx, kb * e_gx], axis=1))
                    qgs.append(qp * e_gx)
                    aks.append(jnp.concatenate([ma[CHUNK:2 * CHUNK] * dec, k_t_pair * jnp.exp(g_last - gr)],
                                               axis=0).astype(BF16))
                    egs.append(jnp.exp(g_last))
        invs = _pair_tri_inverse(ms, bd1, tt, ss)
        sols = [_bdot(inv, _bd(rhs, bd2)) for inv, rhs in zip(invs, rhss)]
        m_sols = []
        for m, sol in zip(ms, sols):
            sh, sl = _split(sol)
            mh, ml = _split(m)
            shb, slb = _bd(sh, bd2), _bd(sl, bd2)
            m_sols.append(jnp.dot(mh, shb, preferred_element_type=F32)
                          + jnp.dot(mh, slb, preferred_element_type=F32)
                          + jnp.dot(ml, shb, preferred_element_type=F32))
        sols = [sol + _bdot(inv, _bd(rhs - sol - m_sol, bd2))
                for inv, rhs, sol, m_sol in zip(invs, rhss, sols, m_sols)]
        for (c, sy), sol, qg, ak, eg in zip(ids, sols, qgs, aks, egs):
            u_s[c, sy] = sol[:, 0:LANES]
            wq_s[c, sy] = jnp.concatenate([sol[:, LANES:2 * LANES], qg], axis=0).astype(BF16)
            ak_s[c, sy] = ak
            eg_s[c, sy] = eg
        return carry

    lax.fori_loop(0, nc // prep, prepare_chunks, 0)

    def scan_chunk(i, state):
        ids = [(d, (nc - 1 - i) if d == 1 else i, p) for d in range(N_DIR) for p in range(C_PAIRS)]
        wq_states = [jnp.dot(wq_s[c, d * C_PAIRS + p], _bd(state[d * C_PAIRS + p], bd1),
                             preferred_element_type=F32) for d, c, p in ids]
        v_news = [u_s[c, d * C_PAIRS + p] - wqs[0:CHUNK] for (d, c, p), wqs in zip(ids, wq_states)]
        aks = [jnp.dot(ak_s[c, d * C_PAIRS + p], _bd(v_new, bd1), preferred_element_type=F32)
               for (d, c, p), v_new in zip(ids, v_news)]
        new_state = [eg_s[c, d * C_PAIRS + p] * state[d * C_PAIRS + p] + ak[CHUNK:2 * CHUNK]
                     for (d, c, p), ak in zip(ids, aks)]
        for (d, c, p), wqs, ak in zip(ids, wq_states, aks):
            o_s[d, pl.ds(pl.multiple_of(c * CHUNK, CHUNK), CHUNK), p * LANES:(p + 1) * LANES] = (
                wqs[CHUNK:2 * CHUNK] + ak[0:CHUNK])
        return jnp.stack(new_state)

    if latent:
        state0 = s0_ref[...]
    else:
        state0 = jnp.zeros((N_SYS, C_DK, LANES), F32)
    state = lax.fori_loop(0, nc, scan_chunk, state0)

    o = o_s[0] + o_s[1]
    ms = _group_sumsq(o, _group_ones(C_WIDTH, C_DV)) * (1.0 / C_DV)
    gate = zc_ref[:, CONV_CH:CONV_CH + C_WIDTH]
    oc_ref[...] = o * lax.rsqrt(ms + EPS) * dn_ref[0] * _silu(gate)
    if not latent:
        sfin_ref[...] = state


def _delta(zc, zb, conv_w, al_p, dtb_p, dn_p, layer, latent, state_delta=None):
    seq_len = DEC_SEQ if latent else SEQ
    n_seq = DEC_BATCH if latent else BATCH
    row0 = N_CTX_ROWS // seq_len if latent else 0
    nc = seq_len // CHUNK
    in_specs = [pl.BlockSpec((seq_len, ZC_W), lambda s: (row0 + s, 0)),
                pl.BlockSpec((seq_len, LANES), lambda s: (row0 + s, SMALL_OFF // LANES)),
                pl.BlockSpec((1, CONV_WIDTH, CONV_CH), lambda s: (layer, 0, 0)),
                pl.BlockSpec((1, 1, LANES), lambda s: (layer, 0, 0)),
                pl.BlockSpec((1, 1, LANES), lambda s: (layer, 0, 0)),
                pl.BlockSpec((1, 1, C_WIDTH), lambda s: (layer, 0, 0))]
    args = [zc, zb, conv_w, al_p, dtb_p, dn_p]
    out_specs = [pl.BlockSpec((seq_len, C_WIDTH), lambda s: (s, 0))]
    out_shape = [jax.ShapeDtypeStruct((n_seq * seq_len, C_WIDTH), F32)]
    if latent:
        in_specs.append(pl.BlockSpec((None, None, N_SYS, C_DK, LANES), lambda s: (s, layer, 0, 0, 0)))
        args.append(state_delta)
    else:
        out_specs.append(pl.BlockSpec((None, N_SYS, C_DK, LANES), lambda s: (s, 0, 0, 0)))
        out_shape.append(jax.ShapeDtypeStruct((n_seq, N_SYS, C_DK, LANES), F32))
    return pl.pallas_call(
        functools.partial(_delta_kernel, seq_len=seq_len, latent=latent),
        grid=(n_seq,),
        in_specs=in_specs,
        out_specs=out_specs,
        out_shape=out_shape,
        scratch_shapes=[pltpu.VMEM((CONV_CH // LANES, seq_len + CONV_PAD, LANES), F32),
                        pltpu.VMEM((CONV_CH // LANES, seq_len + CONV_PAD, LANES), F32),
                        pltpu.VMEM((seq_len, C_QK_WIDTH), F32),
                        pltpu.VMEM((seq_len, C_QK_WIDTH), F32),
                        pltpu.VMEM((seq_len, C_WIDTH), F32),
                        pltpu.VMEM((N_DIR, seq_len, C_WIDTH), F32),
                        pltpu.VMEM((N_DIR, seq_len, C_WIDTH), F32),
                        pltpu.VMEM((N_DIR, seq_len, C_WIDTH), F32),
                        pltpu.VMEM((N_DIR, seq_len, C_WIDTH), F32),
                        pltpu.VMEM((nc, N_SYS, CHUNK, LANES), F32),
                        pltpu.VMEM((nc, N_SYS, 2 * CHUNK, LANES), BF16),
                        pltpu.VMEM((nc, N_SYS, 2 * CHUNK, LANES), BF16),
                        pltpu.VMEM((nc, N_SYS, 1, LANES), F32)],
        compiler_params=pltpu.CompilerParams(vmem_limit_bytes=VMEM_LIMIT),
        name="delta_latent" if latent else "delta_ctx",
    )(*args)


def _rope_tables():
    rows = DEC_SEQ // GRID_W
    row = jnp.repeat(jnp.arange(rows, dtype=F32), GRID_W)
    col = jnp.tile(jnp.arange(GRID_W, dtype=F32), rows)
    n_freq = HEAD_DIM // 4
    inv_freq = ROPE_THETA ** (-jnp.arange(n_freq, dtype=F32) / n_freq)
    ang_r = row[:, None] * inv_freq
    ang_c = col[:, None] * inv_freq
    ang = jnp.concatenate([ang_r, ang_r, ang_c, ang_c], axis=-1)
    cos, sin = jnp.cos(ang), jnp.sin(ang)
    first = (jnp.arange(HEAD_DIM) % 32) < 16
    sin_a = jnp.where(first, -sin, 0.0)
    sin_b = jnp.where(first, 0.0, sin)
    reps = LANES // HEAD_DIM
    return tuple(jnp.tile(t, (1, reps)) for t in (cos, sin_a, sin_b))


def _block_diag_state(s):
    eye = jnp.eye(B_HEADS, dtype=s.dtype)
    out = jnp.einsum('...hkv,hg->...hkgv', s, eye)
    return out.reshape(s.shape[:-3] + (B_QK_WIDTH, B_WIDTH))


def _pair_state(s):
    lead = s.shape[:-4]
    s = s.reshape(lead + (N_DIR, C_PAIRS, 2, C_DK, C_DV))
    s = jnp.moveaxis(s, -3, -2)
    return s.reshape(lead + (N_SYS, C_DK, 2 * C_DV))


def _unpair_state(s):
    lead = s.shape[:-3]
    s = s.reshape(lead + (N_DIR, C_PAIRS, C_DK, 2, C_DV))
    s = jnp.moveaxis(s, -2, -3)
    return s.reshape(lead + (N_DIR, C_HEADS, C_DK, C_DV))


def _diag_blocks(s_bd):
    s5 = s_bd.reshape(s_bd.shape[:-2] + (B_HEADS, B_DK, B_HEADS, B_DV))
    return jnp.stack([s5[..., h, :, h, :] for h in range(B_HEADS)], axis=-3)


def kernel(x_prompt, x_sample, cache_k, cache_v, state_gla, state_delta, c, c_ctx, w_mod, b_mod, norm_gains, w_in, qk_gain, w_gla_gate, b_gla_gate, gla_norm, conv_w, a_log, dt_bias, delta_norm, w_out, w_gate, w_up, w_down):
    xc = x_prompt.reshape(N_CTX_ROWS, D_MODEL)
    xl = x_sample.reshape(N_LAT_ROWS, D_MODEL)
    cond = jnp.concatenate([c_ctx[None, :], c, jnp.zeros((N_COND - 1 - DEC_BATCH, D_MODEL), F32)], axis=0)
    w_out_b = w_out.astype(BF16)
    w_gate_b = w_gate.astype(BF16)
    w_up_b = w_up.astype(BF16)
    w_down_b = w_down.astype(BF16)
    qk_gain2 = jnp.tile(qk_gain, (1, 1, LANES // HEAD_DIM))
    wg_p = jnp.zeros((DEPTH, LANES, N_DIR * B_QK_WIDTH), F32)
    for d in range(N_DIR):
        wg_p = wg_p.at[:, d * GATE_RANK:(d + 1) * GATE_RANK, d * B_QK_WIDTH:(d + 1) * B_QK_WIDTH].set(w_gla_gate[:, d])
    bg_p = b_gla_gate.reshape(DEPTH, 1, N_DIR * B_QK_WIDTH)
    gn_p = jnp.tile(gla_norm, (1, B_HEADS)).reshape(DEPTH, 1, B_WIDTH)
    dn_p = jnp.tile(delta_norm, (1, C_HEADS)).reshape(DEPTH, 1, C_WIDTH)
    al_p = jnp.zeros((DEPTH, 1, LANES), F32).at[:, 0, DECAY_LANE:DECAY_LANE + N_DIR * C_HEADS].set(
        a_log.reshape(DEPTH, N_DIR * C_HEADS))
    dtb_p = jnp.zeros((DEPTH, 1, LANES), F32).at[:, 0, DECAY_LANE:DECAY_LANE + N_DIR * C_HEADS].set(
        dt_bias.reshape(DEPTH, N_DIR * C_HEADS))
    cache_k2 = cache_k.reshape(DEC_BATCH, DEPTH, PAST_LEN, A_KV_WIDTH)
    cache_v2 = cache_v.reshape(DEC_BATCH, DEPTH, PAST_LEN, A_KV_WIDTH)
    s0_gla_bd = _block_diag_state(state_gla.astype(F32))
    s0_delta = _pair_state(state_delta.astype(F32))
    rope = _rope_tables()

    mods = _modulation(cond, w_mod, b_mod).reshape(DEPTH * N_COND, 1, 6 * D_MODEL)

    new_k, new_v, new_gla, new_delta = [], [], [], []
    for l in range(DEPTH):
        za, zb, zc = _in_projection(xc, xl, mods, norm_gains, w_in, l)
        oa_c, k_l, v_l = _attention(za, qk_gain2, l, False)
        (oa_l,) = _attention(za, qk_gain2, l, True, cache_k2, cache_v2, rope)
        ob_c, sg_l = _gla(zb, wg_p, bg_p, gn_p, l, False)
        (ob_l,) = _gla(zb, wg_p, bg_p, gn_p, l, True, s0_gla_bd)
        oc_c, sd_l = _delta(zc, zb, conv_w, al_p, dtb_p, dn_p, l, False)
        (oc_l,) = _delta(zc, zb, conv_w, al_p, dtb_p, dn_p, l, True, s0_delta)
        xc, xl = _out_ffn(xc, xl, (oa_c, oa_l, ob_c, ob_l, oc_c, oc_l), mods, norm_gains,
                          w_out_b, w_gate_b, w_up_b, w_down_b, l)
        new_k.append(k_l.reshape(BATCH, SEQ, A_KV_HEADS, HEAD_DIM))
        new_v.append(v_l.reshape(BATCH, SEQ, A_KV_HEADS, HEAD_DIM))
        new_gla.append(_diag_blocks(sg_l))
        new_delta.append(_unpair_state(sd_l))

    out_dtype = x_prompt.dtype
    y_prompt = xc.reshape(BATCH, SEQ, D_MODEL)
    y_sample = xl.reshape(DEC_BATCH, DEC_SEQ, D_MODEL)
    return (y_prompt, y_sample,
            jnp.stack(new_k, axis=1), jnp.stack(new_v, axis=1),
            jnp.stack(new_gla, axis=1).astype(out_dtype),
            jnp.stack(new_delta, axis=1).astype(out_dtype))
```

```python
import functools

import jax
import jax.numpy as jnp
import numpy as np
from jax import lax
from jax.experimental import pallas as pl
from jax.experimental.pallas import tpu as pltpu

F32 = jnp.float32
BF16 = jnp.bfloat16

D_MODEL = 1024
BATCH = 16
SEQ = 256
DEPTH = 4
DEC_BATCH = 4
DEC_SEQ = 1024
PAST_LEN = 512
GRID_W = 64
HEAD_DIM = 64
A_HEADS = 8
A_KV_HEADS = 2
A_REP = A_HEADS // A_KV_HEADS
A_WIDTH = A_HEADS * HEAD_DIM
A_KV_WIDTH = A_KV_HEADS * HEAD_DIM
ROPE_THETA = 10000.0
B_HEADS = 4
B_DK = 32
B_DV = 64
B_QK_WIDTH = B_HEADS * B_DK
B_WIDTH = B_HEADS * B_DV
GATE_RANK = 16
GATE_TAU = 16.0
C_HEADS = 4
C_DK = 64
C_DV = 64
C_QK_WIDTH = C_HEADS * C_DK
C_WIDTH = C_HEADS * C_DV
CONV_WIDTH = 5
CONV_CH = 2 * C_QK_WIDTH + C_WIDTH
CHUNK = 64
N_DIR = 2
MIX_WIDTH = A_WIDTH + B_WIDTH + C_WIDTH
D_FF = -(-(8 * D_MODEL) // (3 * 256)) * 256
EPS = 1e-6

LANES = 128
N_CTX_ROWS = BATCH * SEQ
N_LAT_ROWS = DEC_BATCH * DEC_SEQ
N_ROWS = N_CTX_ROWS + N_LAT_ROWS
N_COND = 8

ZA_W = A_WIDTH + 2 * A_KV_WIDTH
ZB_W = 2 * B_QK_WIDTH + 2 * B_WIDTH + LANES
ZC_W = 2 * C_QK_WIDTH + 2 * C_WIDTH
PROJ_PAD = ZA_W + ZB_W + ZC_W
SMALL_OFF = 2 * B_QK_WIDTH + 2 * B_WIDTH
NAT_SMALL = A_WIDTH + 2 * A_KV_WIDTH + 2 * B_QK_WIDTH + 2 * B_WIDTH
NAT_C = NAT_SMALL + N_DIR * GATE_RANK
NAT_BC = NAT_C + ZC_W
BETA_LANE = N_DIR * GATE_RANK
DECAY_LANE = BETA_LANE + N_DIR * C_HEADS

VMEM_LIMIT = 56 * 1024 * 1024


def _split(x):
    hi = x.astype(BF16)
    lo = (x - hi.astype(F32)).astype(BF16)
    return hi, lo


def _bdot(a, b):
    return jnp.dot(a.astype(BF16), b.astype(BF16), preferred_element_type=F32)


def _bdot_nt(a, b):
    return lax.dot_general(a.astype(BF16), b.astype(BF16), (((1,), (1,)), ((), ())),
                           preferred_element_type=F32)


def _bdot_tn(a, b):
    return lax.dot_general(a.astype(BF16), b.astype(BF16), (((0,), (0,)), ((), ())),
                           preferred_element_type=F32)


def _dot3(a, b):
    ah, al = _split(a)
    bh, bl = _split(b)
    return (jnp.dot(ah, bh, preferred_element_type=F32)
            + jnp.dot(ah, bl, preferred_element_type=F32)
            + jnp.dot(al, bh, preferred_element_type=F32))


def _dot_lhs_exact(m_bf16, x):
    xh, xl = _split(x)
    return (jnp.dot(m_bf16, xh, preferred_element_type=F32)
            + jnp.dot(m_bf16, xl, preferred_element_type=F32))


def _dot_rhs_exact(x, m_bf16):
    xh, xl = _split(x)
    return (jnp.dot(xh, m_bf16, preferred_element_type=F32)
            + jnp.dot(xl, m_bf16, preferred_element_type=F32))


def _group_sumsq(x, ones_g):
    return jnp.dot((x * x).astype(BF16), ones_g, preferred_element_type=F32)


def _group_ones(width, group):
    r = lax.broadcasted_iota(jnp.int32, (width, width), 0) // group
    c = lax.broadcasted_iota(jnp.int32, (width, width), 1) // group
    return (r == c).astype(BF16)


def _sigmoid(x):
    return 1.0 / (1.0 + jnp.exp(-x))


def _silu(x):
    return x * _sigmoid(x)


def _softplus(x):
    return jnp.maximum(x, 0.0) + jnp.log1p(jnp.exp(-jnp.abs(x)))


def _rms(x, g):
    return x * lax.rsqrt(jnp.mean(x * x, axis=-1, keepdims=True) + EPS) * g


def _chunk_tri(t, reverse):
    r = lax.broadcasted_iota(jnp.int32, (t, t), 0)
    c = lax.broadcasted_iota(jnp.int32, (t, t), 1)
    same = (r // CHUNK) == (c // CHUNK)
    tri = (c >= r) if reverse else (c <= r)
    return (same & tri).astype(BF16)


def _mod_row(i, tm):
    start = i * tm
    return jnp.where(start < N_CTX_ROWS, 0, 1 + (start - N_CTX_ROWS) // DEC_SEQ)


def _mod_kernel(cond_ref, w_ref, b_ref, o_ref):
    c = cond_ref[...]
    o_ref[0] = _bdot(_silu(c), w_ref[0]) + b_ref[0]


def _modulation(cond, w_mod, b_mod):
    tn = 1536
    return pl.pallas_call(
        _mod_kernel,
        grid=(DEPTH, 6 * D_MODEL // tn),
        in_specs=[pl.BlockSpec((N_COND, D_MODEL), lambda l, j: (0, 0)),
                  pl.BlockSpec((1, D_MODEL, tn), lambda l, j: (l, 0, j)),
                  pl.BlockSpec((1, 1, tn), lambda l, j: (l, 0, j))],
        out_specs=pl.BlockSpec((1, N_COND, tn), lambda l, j: (l, 0, j)),
        out_shape=jax.ShapeDtypeStruct((DEPTH, N_COND, 6 * D_MODEL), F32),
        compiler_params=pltpu.CompilerParams(vmem_limit_bytes=VMEM_LIMIT),
        name="adaln_mod",
    )(cond, w_mod, b_mod.reshape(DEPTH, 1, 6 * D_MODEL))


DENSE_TM = 512
N_CTX_BLOCKS = N_CTX_ROWS // DENSE_TM


def _ctx_map(i):
    return (jnp.minimum(i, N_CTX_BLOCKS - 1), 0)


def _lat_map(i):
    return (jnp.maximum(i - N_CTX_BLOCKS, 0), 0)


def _inproj_kernel(xc_ref, xl_ref, mod_ref, ng_ref, w_ref, za_ref, zb_ref, zc_ref, wp_s):
    @pl.when(pl.program_id(0) == 0)
    def _():
        n_gate = N_DIR * GATE_RANK
        n_bc = 2 * N_DIR * C_HEADS
        wp_s[:, 0:NAT_SMALL] = w_ref[0, :, 0:NAT_SMALL].astype(BF16)
        wp_s[:, NAT_SMALL:NAT_SMALL + n_gate] = w_ref[0, :, NAT_SMALL:NAT_C].astype(BF16)
        wp_s[:, NAT_SMALL + n_gate:NAT_SMALL + n_gate + n_bc] = w_ref[0, :, NAT_BC:NAT_BC + n_bc].astype(BF16)
        wp_s[:, NAT_SMALL + n_gate + n_bc:NAT_SMALL + LANES] = jnp.zeros(
            (D_MODEL, LANES - n_gate - n_bc), BF16)
        wp_s[:, NAT_SMALL + LANES:PROJ_PAD] = w_ref[0, :, NAT_C:NAT_BC].astype(BF16)

    x = jnp.where(pl.program_id(0) < N_CTX_BLOCKS, xc_ref[...], xl_ref[...])
    mod = mod_ref[0]
    shift = mod[:, 0:D_MODEL]
    scale = mod[:, D_MODEL:2 * D_MODEL]
    h = _rms(x, ng_ref[0, 0:1, :]) * (1.0 + scale) + shift
    z = jnp.dot(h.astype(BF16), wp_s[...], preferred_element_type=F32)
    za_ref[...] = z[:, 0:ZA_W]
    zb_ref[...] = z[:, ZA_W:ZA_W + ZB_W]
    zc_ref[...] = z[:, ZA_W + ZB_W:PROJ_PAD]


def _in_projection(xc, xl, mods, norm_gains, w_in, layer):
    tm = DENSE_TM
    proj_width = w_in.shape[-1]
    return pl.pallas_call(
        _inproj_kernel,
        grid=(N_ROWS // tm,),
        in_specs=[pl.BlockSpec((tm, D_MODEL), _ctx_map),
                  pl.BlockSpec((tm, D_MODEL), _lat_map),
                  pl.BlockSpec((1, 1, 6 * D_MODEL), lambda i: (layer * N_COND + _mod_row(i, tm), 0, 0)),
                  pl.BlockSpec((1, 4, D_MODEL), lambda i: (layer, 0, 0)),
                  pl.BlockSpec((1, D_MODEL, proj_width), lambda i: (layer, 0, 0),
                               pipeline_mode=pl.Buffered(1))],
        out_specs=[pl.BlockSpec((tm, ZA_W), lambda i: (i, 0)),
                   pl.BlockSpec((tm, ZB_W), lambda i: (i, 0)),
                   pl.BlockSpec((tm, ZC_W), lambda i: (i, 0))],
        out_shape=[jax.ShapeDtypeStruct((N_ROWS, ZA_W), F32),
                   jax.ShapeDtypeStruct((N_ROWS, ZB_W), F32),
                   jax.ShapeDtypeStruct((N_ROWS, ZC_W), F32)],
        scratch_shapes=[pltpu.VMEM((D_MODEL, PROJ_PAD), BF16)],
        compiler_params=pltpu.CompilerParams(
            dimension_semantics=("arbitrary",), vmem_limit_bytes=VMEM_LIMIT),
        name="in_proj",
    )(xc, xl, mods, norm_gains, w_in)


FF_TILE = 256


def _ffn_kernel(xc_ref, xl_ref, oac_ref, oal_ref, obc_ref, obl_ref, occ_ref, ocl_ref, mod_ref, ng_ref,
                wo_ref, wg_ref, wu_ref, wd_ref, yc_ref, yl_ref):
    is_ctx = pl.program_id(0) < N_CTX_BLOCKS
    x = jnp.where(is_ctx, xc_ref[...], xl_ref[...])
    mod = mod_ref[0]
    gate_m = mod[:, 2 * D_MODEL:3 * D_MODEL]
    shift_f = mod[:, 3 * D_MODEL:4 * D_MODEL]
    scale_f = mod[:, 4 * D_MODEL:5 * D_MODEL]
    gate_f = mod[:, 5 * D_MODEL:6 * D_MODEL]
    mix_in = jnp.concatenate([jnp.where(is_ctx, oac_ref[...], oal_ref[...]),
                              jnp.where(is_ctx, obc_ref[...], obl_ref[...]),
                              jnp.where(is_ctx, occ_ref[...], ocl_ref[...])], axis=-1)
    mix = _bdot(mix_in, wo_ref[0])
    x1 = x + gate_m * _rms(mix, ng_ref[0, 1:2, :])
    h = (_rms(x1, ng_ref[0, 2:3, :]) * (1.0 + scale_f) + shift_f).astype(BF16)
    f = jnp.zeros(x.shape, F32)
    for j in range(D_FF // FF_TILE):
        cols = slice(j * FF_TILE, (j + 1) * FF_TILE)
        g = jnp.dot(h, wg_ref[0, :, cols], preferred_element_type=F32)
        u = jnp.dot(h, wu_ref[0, :, cols], preferred_element_type=F32)
        f = f + _bdot(_silu(g) * u, wd_ref[0, cols, :])
    y = x1 + gate_f * _rms(f, ng_ref[0, 3:4, :])

    @pl.when(is_ctx)
    def _():
        yc_ref[...] = y

    @pl.when(jnp.logical_not(is_ctx))
    def _():
        yl_ref[...] = y


def _out_ffn(xc, xl, mixer_outs, mods, norm_gains, w_out_b, w_gate_b, w_up_b, w_down_b, layer):
    tm = DENSE_TM
    resident = dict(pipeline_mode=pl.Buffered(1))
    pair_specs = []
    for width in (D_MODEL, A_WIDTH, B_WIDTH, C_WIDTH):
        pair_specs += [pl.BlockSpec((tm, width), _ctx_map), pl.BlockSpec((tm, width), _lat_map)]
    return pl.pallas_call(
        _ffn_kernel,
        grid=(N_ROWS // tm,),
        in_specs=pair_specs + [
                  pl.BlockSpec((1, 1, 6 * D_MODEL), lambda i: (layer * N_COND + _mod_row(i, tm), 0, 0)),
                  pl.BlockSpec((1, 4, D_MODEL), lambda i: (layer, 0, 0)),
                  pl.BlockSpec((1, MIX_WIDTH, D_MODEL), lambda i: (layer, 0, 0), **resident),
                  pl.BlockSpec((1, D_MODEL, D_FF), lambda i: (layer, 0, 0), **resident),
                  pl.BlockSpec((1, D_MODEL, D_FF), lambda i: (layer, 0, 0), **resident),
                  pl.BlockSpec((1, D_FF, D_MODEL), lambda i: (layer, 0, 0), **resident)],
        out_specs=[pl.BlockSpec((tm, D_MODEL), _ctx_map), pl.BlockSpec((tm, D_MODEL), _lat_map)],
        out_shape=[jax.ShapeDtypeStruct((N_CTX_ROWS, D_MODEL), F32),
                   jax.ShapeDtypeStruct((N_LAT_ROWS, D_MODEL), F32)],
        compiler_params=pltpu.CompilerParams(
            dimension_semantics=("arbitrary",), vmem_limit_bytes=VMEM_LIMIT),
        name="out_ffn",
    )(xc, xl, *mixer_outs, mods, norm_gains, w_out_b, w_gate_b, w_up_b, w_down_b)


ATT_TQ = 512
ATT_LOOKAHEAD = 1


def _head_norm(x, gain_row, ones_g):
    ss = _group_sumsq(x, ones_g)
    return x * lax.rsqrt(ss * (1.0 / HEAD_DIM) + EPS) * gain_row


def _rope(x, cos, sin_a, sin_b):
    return x * cos + pltpu.roll(x, LANES - 16, 1) * sin_a + pltpu.roll(x, 16, 1) * sin_b


def _attn_kernel(*refs, seq_len, tq, latent):
    if latent:
        (q_ref, kv_ref, ck_ref, cv_ref, qkg_ref, cos_ref, sa_ref, sb_ref, o_ref, k_s, v_s) = refs
    else:
        (q_ref, kv_ref, qkg_ref, o_ref, kn_ref, vn_ref, k_s, v_s) = refs
    j = pl.program_id(1)
    ones_g = _group_ones(LANES, HEAD_DIM)
    gq = qkg_ref[0, 0:1, :]
    gk = qkg_ref[0, 1:2, :]
    past = PAST_LEN if latent else 0

    @pl.when(j == 0)
    def _():
        ka = kv_ref[:, 0:A_KV_WIDTH]
        va = kv_ref[:, A_KV_WIDTH:2 * A_KV_WIDTH]
        kn = _head_norm(ka, gk, ones_g)
        if latent:
            kr = _rope(kn, cos_ref[...], sa_ref[...], sb_ref[...])
            ck = ck_ref[...]
            cv = cv_ref[...]
            for g in range(A_KV_HEADS):
                cols = slice(g * HEAD_DIM, (g + 1) * HEAD_DIM)
                k_s[g, 0:past, :] = ck[:, cols].astype(BF16)
                v_s[g, 0:past, :] = cv[:, cols].astype(BF16)
                k_s[g, past:past + seq_len, :] = kr[:, cols].astype(BF16)
                v_s[g, past:past + seq_len, :] = va[:, cols].astype(BF16)
        else:
            kn_ref[...] = kn
            vn_ref[...] = va
            for g in range(A_KV_HEADS):
                cols = slice(g * HEAD_DIM, (g + 1) * HEAD_DIM)
                k_s[g] = kn[:, cols].astype(BF16)
                v_s[g] = va[:, cols].astype(BF16)

    if latent:
        rows = pl.ds(pl.multiple_of(j * tq, tq), tq)
        cos = cos_ref[rows, :]
        sin_a = sa_ref[rows, :]
        sin_b = sb_ref[rows, :]
    q_heads = []
    for t in range(A_WIDTH // LANES):
        qt = _head_norm(q_ref[:, t * LANES:(t + 1) * LANES], gq, ones_g)
        if latent:
            qt = _rope(qt, cos, sin_a, sin_b)
        qt = (qt * (HEAD_DIM ** -0.5)).astype(BF16)
        q_heads += [qt[:, hh * HEAD_DIM:(hh + 1) * HEAD_DIM] for hh in range(LANES // HEAD_DIM)]

    def scores(h):
        return lax.dot_general(q_heads[h], k_s[h // A_REP], (((1,), (1,)), ((), ())),
                               preferred_element_type=F32)

    outs = []
    pending = [scores(h) for h in range(ATT_LOOKAHEAD)]
    for h in range(A_HEADS):
        s = pending.pop(0)
        if h + ATT_LOOKAHEAD < A_HEADS:
            pending.append(scores(h + ATT_LOOKAHEAD))
        m = jnp.max(s, axis=-1, keepdims=True)
        p = jnp.exp(s - m)
        l = jnp.sum(p, axis=-1, keepdims=True)
        o = jnp.dot(p.astype(BF16), v_s[h // A_REP], preferred_element_type=F32)
        outs.append(o / l)
    o_ref[...] = jnp.concatenate(outs, axis=-1)


def _attention(za, qk_gain2, layer, latent, cache_k=None, cache_v=None, rope=None):
    seq_len = DEC_SEQ if latent else SEQ
    n_seq = DEC_BATCH if latent else BATCH
    row0 = N_CTX_ROWS // seq_len if latent else 0
    tq = min(ATT_TQ, seq_len)
    nq = seq_len // tq
    row0q = N_CTX_ROWS // tq if latent else 0
    s_len = seq_len + (PAST_LEN if latent else 0)
    in_specs = [pl.BlockSpec((tq, A_WIDTH), lambda s, j: (row0q + s * nq + j, 0)),
                pl.BlockSpec((seq_len, 2 * A_KV_WIDTH), lambda s, j: (row0 + s, 2))]
    args = [za, za]
    if latent:
        in_specs += [pl.BlockSpec((None, None, PAST_LEN, A_KV_WIDTH), lambda s, j: (s, layer, 0, 0)),
                     pl.BlockSpec((None, None, PAST_LEN, A_KV_WIDTH), lambda s, j: (s, layer, 0, 0))]
        args += [cache_k, cache_v]
    in_specs.append(pl.BlockSpec((1, 2, LANES), lambda s, j: (layer, 0, 0)))
    args.append(qk_gain2)
    if latent:
        in_specs += [pl.BlockSpec((seq_len, LANES), lambda s, j: (0, 0))] * 3
        args += list(rope)
    out_specs = [pl.BlockSpec((tq, A_WIDTH), lambda s, j: (s * nq + j, 0))]
    out_shape = [jax.ShapeDtypeStruct((n_seq * seq_len, A_WIDTH), F32)]
    if not latent:
        out_specs += [pl.BlockSpec((None, seq_len, A_KV_WIDTH), lambda s, j: (s, 0, 0))] * 2
        out_shape += [jax.ShapeDtypeStruct((n_seq, seq_len, A_KV_WIDTH), F32)] * 2
    return pl.pallas_call(
        functools.partial(_attn_kernel, seq_len=seq_len, tq=tq, latent=latent),
        grid=(n_seq, nq),
        in_specs=in_specs,
        out_specs=out_specs,
        out_shape=out_shape,
        scratch_shapes=[pltpu.VMEM((A_KV_HEADS, s_len, HEAD_DIM), BF16),
                        pltpu.VMEM((A_KV_HEADS, s_len, HEAD_DIM), BF16)],
        compiler_params=pltpu.CompilerParams(
            dimension_semantics=("arbitrary", "arbitrary"), vmem_limit_bytes=VMEM_LIMIT),
        name="attn_latent" if latent else "attn_ctx",
    )(*args)


def _gla_kernel(*refs, seq_len, n_sub, latent):
    if latent:
        zb_ref, wg_ref, bg_ref, gn_ref, s0_ref, ob_ref = refs
    else:
        zb_ref, wg_ref, bg_ref, gn_ref, ob_ref, sfin_ref = refs
    nc = seq_len // CHUNK
    q = zb_ref[:, 0:B_QK_WIDTH] * (B_DK ** -0.5)
    k = zb_ref[:, B_QK_WIDTH:2 * B_QK_WIDTH]
    v = zb_ref[:, 2 * B_QK_WIDTH:2 * B_QK_WIDTH + B_WIDTH]
    small = zb_ref[:, SMALL_OFF:SMALL_OFF + LANES]
    pre = _dot3(small, wg_ref[0]) + bg_ref[0]
    glog = (jnp.minimum(pre, 0.0) - jnp.log1p(jnp.exp(-jnp.abs(pre)))) * (1.0 / GATE_TAU)

    kr = lax.broadcasted_iota(jnp.int32, (B_QK_WIDTH, B_WIDTH), 0) // B_DK
    vc = lax.broadcasted_iota(jnp.int32, (B_QK_WIDTH, B_WIDTH), 1) // B_DV
    bd_kv = kr == vc
    sr = lax.broadcasted_iota(jnp.int32, (B_WIDTH, B_WIDTH), 0) // CHUNK
    sc = lax.broadcasted_iota(jnp.int32, (B_WIDTH, B_WIDTH), 1) // B_DV
    bd_sv = sr == sc
    tt = lax.broadcasted_iota(jnp.int32, (CHUNK, B_WIDTH), 0)
    ss = lax.broadcasted_iota(jnp.int32, (CHUNK, B_WIDTH), 1) % CHUNK

    q3 = q.reshape(nc, CHUNK, B_QK_WIDTH)
    k3 = k.reshape(nc, CHUNK, B_QK_WIDTH)
    intra, upds, decays, q_sts = [], [], [], []
    for d in range(N_DIR):
        g = glog[:, d * B_QK_WIDTH:(d + 1) * B_QK_WIDTH]
        tri = _chunk_tri(CUM_ROWS, d == 1)
        b = jnp.concatenate([_dot_lhs_exact(tri, g[r * CUM_ROWS:(r + 1) * CUM_ROWS, :])
                             for r in range(seq_len // CUM_ROWS)], axis=0)
        b3 = b.reshape(nc, CHUNK, B_QK_WIDTH)
        mid = b3[:, CHUNK // 2:CHUNK // 2 + 1, :]
        last = b3[:, 0:1, :] if d == 1 else b3[:, CHUNK - 1:CHUNK, :]
        q_in = q3 * jnp.exp(b3 - mid)
        k_in = k3 * jnp.exp(mid - b3)
        q_sts.append(q3 * jnp.exp(b3))
        k_st = k3 * jnp.exp(last - b3)
        decay_t = jnp.exp(jnp.broadcast_to(last, (nc, 8, B_QK_WIDTH)).reshape(nc * 8, B_QK_WIDTH)).T
        causal = (ss >= tt) if d == 1 else (ss <= tt)
        v_chunks = [v[c * CHUNK:(c + 1) * CHUNK, :] for c in range(nc)]
        atts = [_bdot(q_in[c], jnp.where(bd_kv, jnp.concatenate([k_in[c].T] * B_HEADS, axis=1), 0.0))
                for c in range(nc)]
        upds.append([jnp.where(bd_kv, _bdot_tn(k_st[c], v_chunks[c]), 0.0) for c in range(nc)])
        intra.append([_bdot(jnp.where(causal, atts[c], 0.0),
                            jnp.where(bd_sv, jnp.concatenate([v_chunks[c]] * B_HEADS, axis=0), 0.0))
                      for c in range(nc)])
        decays.append([decay_t[:, 8 * c:8 * c + 1] for c in range(nc)])
    o_dirs = []
    per_seq = nc // n_sub
    for d in range(N_DIR):
        o_chunks = [None] * nc
        for sq in range(n_sub):
            state = s0_ref[d] if latent else jnp.zeros((B_QK_WIDTH, B_WIDTH), F32)
            chunks = range(sq * per_seq, (sq + 1) * per_seq)
            for c in (reversed(chunks) if d == 1 else chunks):
                o_chunks[c] = intra[d][c] + _bdot(q_sts[d][c], state)
                state = decays[d][c] * state + upds[d][c]
            if not latent:
                sfin_ref[sq, d] = state
        o_dirs.append(jnp.concatenate(o_chunks, axis=0))
    o = o_dirs[0] + o_dirs[1]
    ms = _group_sumsq(o, _group_ones(B_WIDTH, B_DV)) * (1.0 / B_DV)
    r = zb_ref[:, 2 * B_QK_WIDTH + B_WIDTH:2 * B_QK_WIDTH + 2 * B_WIDTH]
    ob_ref[...] = o * lax.rsqrt(ms + EPS) * gn_ref[0] * _silu(r)


GLA_CTX_SEQS = 4


def _gla(zb, wg_p, bg_p, gn_p, layer, latent, s0_bd=None):
    n_sub = 1 if latent else GLA_CTX_SEQS
    seq_len = (DEC_SEQ if latent else SEQ) * n_sub
    n_seq = (DEC_BATCH if latent else BATCH) // n_sub
    row0 = N_CTX_ROWS // seq_len if latent else 0
    in_specs = [pl.BlockSpec((seq_len, ZB_W), lambda s: (row0 + s, 0)),
                pl.BlockSpec((1, LANES, N_DIR * B_QK_WIDTH), lambda s: (layer, 0, 0)),
                pl.BlockSpec((1, 1, N_DIR * B_QK_WIDTH), lambda s: (layer, 0, 0)),
                pl.BlockSpec((1, 1, B_WIDTH), lambda s: (layer, 0, 0))]
    args = [zb, wg_p, bg_p, gn_p]
    out_specs = [pl.BlockSpec((seq_len, B_WIDTH), lambda s: (s, 0))]
    out_shape = [jax.ShapeDtypeStruct((n_seq * seq_len, B_WIDTH), F32)]
    if latent:
        in_specs.append(pl.BlockSpec((None, None, N_DIR, B_QK_WIDTH, B_WIDTH), lambda s: (s, layer, 0, 0, 0)))
        args.append(s0_bd)
    else:
        out_specs.append(pl.BlockSpec((n_sub, N_DIR, B_QK_WIDTH, B_WIDTH), lambda s: (s, 0, 0, 0)))
        out_shape.append(jax.ShapeDtypeStruct((n_seq * n_sub, N_DIR, B_QK_WIDTH, B_WIDTH), F32))
    return pl.pallas_call(
        functools.partial(_gla_kernel, seq_len=seq_len, n_sub=n_sub, latent=latent),
        grid=(n_seq,),
        in_specs=in_specs,
        out_specs=out_specs,
        out_shape=out_shape,
        compiler_params=pltpu.CompilerParams(vmem_limit_bytes=VMEM_LIMIT),
        name="gla_latent" if latent else "gla_ctx",
    )(*args)


TRI_BASE = 8
C_PAIRS = C_HEADS // 2
N_SYS = N_DIR * C_PAIRS
PREP_CHUNKS = 4
CUM_ROWS = 256
CONV_PAD = 8


def _bd(y, mask):
    return jnp.where(mask, jnp.concatenate([y, y], axis=0), 0.0).astype(BF16)


def _pair_tri_inverse(ms, bd_mask, tt, ss):
    pws = [jnp.where((tt // TRI_BASE) == (ss // TRI_BASE), -m, 0.0) for m in ms]
    invs = [(tt == ss).astype(F32) + pw for pw in pws]
    span = 2
    while span < TRI_BASE:
        pws = [_bdot(pw, _bd(pw, bd_mask)) for pw in pws]
        invs = [inv + _bdot(inv, _bd(pw, bd_mask)) for inv, pw in zip(invs, pws)]
        span *= 2
    size = 2 * TRI_BASE
    while size <= CHUNK:
        off = ((tt // size) == (ss // size)) & ((tt // (size // 2)) != (ss // (size // 2)))
        cxs = [_bdot(jnp.where(off, m, 0.0), _bd(inv, bd_mask)) for m, inv in zip(ms, invs)]
        invs = [inv - _bdot(inv, _bd(cx, bd_mask)) for inv, cx in zip(invs, cxs)]
        size *= 2
    return invs


def _delta_kernel(*refs, seq_len, n_sub, latent):
    if latent:
        (zc_ref, sm_ref, cw_ref, al_ref, dtb_ref, dn_ref, s0_ref, oc_ref,
         xp_s, yp_s, q_s, k_s, v_s, bx_s, gx_s, gr_s, o_s, u_s, wq_s, ak_s, eg_s) = refs
    else:
        (zc_ref, sm_ref, cw_ref, al_ref, dtb_ref, dn_ref, oc_ref, sfin_ref,
         xp_s, yp_s, q_s, k_s, v_s, bx_s, gx_s, gr_s, o_s, u_s, wq_s, ak_s, eg_s) = refs
    nc = seq_len // CHUNK
    per_seq = nc // n_sub

    len1 = seq_len // n_sub
    seg = len1 // 8 + 1
    sub = lax.broadcasted_iota(jnp.int32, (8, LANES), 0)
    half = CONV_WIDTH // 2
    for j in range(CONV_CH // LANES):
        cols = slice(j * LANES, (j + 1) * LANES)
        taps = [cw_ref[0, tap:tap + 1, cols] for tap in range(CONV_WIDTH)]
        for sq in range(n_sub):
            base = sq * 8 * seg
            xp_s[j, base:base + len1, :] = zc_ref[sq * len1:(sq + 1) * len1, cols]
            xp_s[j, base + len1:base + 8 * seg, :] = jnp.zeros((8 * seg - len1, LANES), F32)
            xr = [xp_s[j, pl.ds(base + a, 8, stride=seg), :] for a in range(seg)]
            nxt = [jnp.where(sub < 7, pltpu.roll(xr[a], 7, 0), 0.0) for a in range(half)]
            prv = [jnp.where(sub > 0, pltpu.roll(xr[seg - half + a], 1, 0), 0.0) for a in range(half)]
            window = prv + xr + nxt
            for a in range(seg):
                acc = window[a] * taps[0]
                for tap in range(1, CONV_WIDTH):
                    acc = acc + window[a + tap] * taps[tap]
                yp_s[j, pl.ds(base + a, 8, stride=seg), :] = acc
    y = _silu(jnp.concatenate(
        [jnp.concatenate([yp_s[j, sq * 8 * seg:sq * 8 * seg + len1, :] for sq in range(n_sub)], axis=0)
         for j in range(CONV_CH // LANES)], axis=1))
    ones_g = _group_ones(C_QK_WIDTH, C_DK)
    qc = y[:, 0:C_QK_WIDTH]
    kc = y[:, C_QK_WIDTH:2 * C_QK_WIDTH]
    q_s[...] = qc * lax.rsqrt(_group_sumsq(qc, ones_g) + EPS) * (C_DK ** -0.5)
    k_s[...] = kc * lax.rsqrt(_group_sumsq(kc, ones_g) + EPS)
    v_s[...] = y[:, 2 * C_QK_WIDTH:CONV_CH]

    small = sm_ref[...]
    beta = _sigmoid(small)
    glog = -jnp.exp(al_ref[0]) * _softplus(small + dtb_ref[0])
    lane_r = lax.broadcasted_iota(jnp.int32, (LANES, C_WIDTH), 0)
    head_c = lax.broadcasted_iota(jnp.int32, (LANES, C_WIDTH), 1) // C_DV
    cr = lax.broadcasted_iota(jnp.int32, (CUM_ROWS, CUM_ROWS), 0)
    cl = lax.broadcasted_iota(jnp.int32, (CUM_ROWS, CUM_ROWS), 1)
    same_chunk = ((cr // CHUNK) == (cl // CHUNK)).astype(BF16)
    on_diag = (lax.broadcasted_iota(jnp.int32, (CUM_ROWS, C_WIDTH), 0) % CHUNK
               == lax.broadcasted_iota(jnp.int32, (CUM_ROWS, C_WIDTH), 1) % C_DV)
    for d in range(N_DIR):
        bx_s[d] = _dot_rhs_exact(beta, (lane_r == BETA_LANE + d * C_HEADS + head_c).astype(BF16))
        pick_gam = (lane_r == DECAY_LANE + d * C_HEADS + head_c).astype(BF16)
        tri = _chunk_tri(CUM_ROWS, d == 1)
        for blk in range(seq_len // CUM_ROWS):
            r = slice(blk * CUM_ROWS, (blk + 1) * CUM_ROWS)
            g_x = _dot_rhs_exact(_dot_lhs_exact(tri, glog[r, :]), pick_gam)
            gx_s[d, r, :] = g_x
            gr_s[d, r, :] = _dot_lhs_exact(same_chunk, jnp.where(on_diag, g_x, 0.0))

    tt = lax.broadcasted_iota(jnp.int32, (CHUNK, LANES), 0)
    ss = lax.broadcasted_iota(jnp.int32, (CHUNK, LANES), 1) % CHUNK
    bd1 = (lax.broadcasted_iota(jnp.int32, (LANES, LANES), 0) // CHUNK
           == lax.broadcasted_iota(jnp.int32, (LANES, LANES), 1) // CHUNK)
    bd2 = jnp.concatenate([bd1, bd1], axis=1)

    prep = min(PREP_CHUNKS, nc)

    def prepare_chunks(step, carry):
        ids, ms, rhss, qgs, aks, egs = [], [], [], [], [], []
        for cj in range(prep):
            c = step * prep + cj
            rows = pl.ds(pl.multiple_of(c * CHUNK, CHUNK), CHUNK)
            k_t = k_s[rows, :].T
            for d in range(N_DIR):
                ahead = (tt - ss) if d == 1 else (ss - tt)
                last = 0 if d == 1 else CHUNK - 1
                for p in range(C_PAIRS):
                    lanes = slice(p * LANES, (p + 1) * LANES)
                    qp = q_s[rows, lanes]
                    kp = k_s[rows, lanes]
                    bx = bx_s[d, rows, lanes]
                    gx = gx_s[d, rows, lanes]
                    gr = gr_s[d, rows, lanes]
                    dec = jnp.exp(jnp.where(ahead <= 0, gx - gr, -jnp.inf))
                    kb = kp * bx
                    ma = lax.dot_general(jnp.concatenate([kb, qp], axis=0).astype(BF16), _bd(kp, bd1),
                                         (((1,), (1,)), ((), ())), preferred_element_type=F32)
                    e_gx = jnp.exp(gx)
                    k_t_pair = jnp.concatenate([k_t[p * LANES:p * LANES + C_DK, :],
                                                k_t[p * LANES + C_DK:(p + 1) * LANES, :]], axis=1)
                    g_last = gx[last:last + 1, :]
                    ids.append((c, d * C_PAIRS + p))
                    ms.append(jnp.where(ahead < 0, ma[0:CHUNK] * dec, 0.0))
                    rhss.append(jnp.concatenate([v_s[rows, lanes] * bx, kb * e_gx], axis=1))
                    qgs.append(qp * e_gx)
                    aks.append(jnp.concatenate([ma[CHUNK:2 * CHUNK] * dec, k_t_pair * jnp.exp(g_last - gr)],
                                               axis=0).astype(BF16))
                    egs.append(jnp.exp(g_last))
        invs = _pair_tri_inverse(ms, bd1, tt, ss)
        sols = [_bdot(inv, _bd(rhs, bd2)) for inv, rhs in zip(invs, rhss)]
        m_sols = []
        for m, sol in zip(ms, sols):
            sh, sl = _split(sol)
            mh, ml = _split(m)
            shb, slb = _bd(sh, bd2), _bd(sl, bd2)
            m_sols.append(jnp.dot(mh, shb, preferred_element_type=F32)
                          + jnp.dot(mh, slb, preferred_element_type=F32)
                          + jnp.dot(ml, shb, preferred_element_type=F32))
        sols = [sol + _bdot(inv, _bd(rhs - sol - m_sol, bd2))
                for inv, rhs, sol, m_sol in zip(invs, rhss, sols, m_sols)]
        for (c, sy), sol, qg, ak, eg in zip(ids, sols, qgs, aks, egs):
            u_s[c, sy] = sol[:, 0:LANES]
            wq_s[c, sy] = jnp.concatenate([sol[:, LANES:2 * LANES], qg], axis=0).astype(BF16)
            ak_s[c, sy] = ak
            eg_s[c, sy] = eg
        return carry

    lax.fori_loop(0, nc // prep, prepare_chunks, 0)

    def scan_chunk(i, state):
        ids = [(sq, d, sq * per_seq + ((per_seq - 1 - i) if d == 1 else i), p)
               for sq in range(n_sub) for d in range(N_DIR) for p in range(C_PAIRS)]
        states = [state[sq * N_SYS + d * C_PAIRS + p] for sq, d, c, p in ids]
        wq_states = [jnp.dot(wq_s[c, d * C_PAIRS + p], _bd(st, bd1), preferred_element_type=F32)
                     for (sq, d, c, p), st in zip(ids, states)]
        v_news = [u_s[c, d * C_PAIRS + p] - wqs[0:CHUNK] for (sq, d, c, p), wqs in zip(ids, wq_states)]
        aks = [jnp.dot(ak_s[c, d * C_PAIRS + p], _bd(v_new, bd1), preferred_element_type=F32)
               for (sq, d, c, p), v_new in zip(ids, v_news)]
        new_state = [eg_s[c, d * C_PAIRS + p] * st + ak[CHUNK:2 * CHUNK]
                     for (sq, d, c, p), st, ak in zip(ids, states, aks)]
        for (sq, d, c, p), wqs, ak in zip(ids, wq_states, aks):
            o_s[d, pl.ds(pl.multiple_of(c * CHUNK, CHUNK), CHUNK), p * LANES:(p + 1) * LANES] = (
                wqs[CHUNK:2 * CHUNK] + ak[0:CHUNK])
        return jnp.stack(new_state)

    if latent:
        state0 = s0_ref[...]
    else:
        state0 = jnp.zeros((n_sub * N_SYS, C_DK, LANES), F32)
    state = lax.fori_loop(0, per_seq, scan_chunk, state0)

    o = o_s[0] + o_s[1]
    ms = _group_sumsq(o, _group_ones(C_WIDTH, C_DV)) * (1.0 / C_DV)
    gate = zc_ref[:, CONV_CH:CONV_CH + C_WIDTH]
    oc_ref[...] = o * lax.rsqrt(ms + EPS) * dn_ref[0] * _silu(gate)
    if not latent:
        sfin_ref[...] = state.reshape(n_sub, N_SYS, C_DK, LANES)


DELTA_CTX_SEQS = 4


def _delta(zc, zb, conv_w, al_p, dtb_p, dn_p, layer, latent, state_delta=None):
    n_sub = 1 if latent else DELTA_CTX_SEQS
    seq_len = (DEC_SEQ if latent else SEQ) * n_sub
    n_seq = (DEC_BATCH if latent else BATCH) // n_sub
    row0 = N_CTX_ROWS // seq_len if latent else 0
    nc = seq_len // CHUNK
    in_specs = [pl.BlockSpec((seq_len, ZC_W), lambda s: (row0 + s, 0)),
                pl.BlockSpec((seq_len, LANES), lambda s: (row0 + s, SMALL_OFF // LANES)),
                pl.BlockSpec((1, CONV_WIDTH, CONV_CH), lambda s: (layer, 0, 0)),
                pl.BlockSpec((1, 1, LANES), lambda s: (layer, 0, 0)),
                pl.BlockSpec((1, 1, LANES), lambda s: (layer, 0, 0)),
                pl.BlockSpec((1, 1, C_WIDTH), lambda s: (layer, 0, 0))]
    args = [zc, zb, conv_w, al_p, dtb_p, dn_p]
    out_specs = [pl.BlockSpec((seq_len, C_WIDTH), lambda s: (s, 0))]
    out_shape = [jax.ShapeDtypeStruct((n_seq * seq_len, C_WIDTH), F32)]
    if latent:
        in_specs.append(pl.BlockSpec((None, None, N_SYS, C_DK, LANES), lambda s: (s, layer, 0, 0, 0)))
        args.append(state_delta)
    else:
        out_specs.append(pl.BlockSpec((n_sub, N_SYS, C_DK, LANES), lambda s: (s, 0, 0, 0)))
        out_shape.append(jax.ShapeDtypeStruct((n_seq * n_sub, N_SYS, C_DK, LANES), F32))
    return pl.pallas_call(
        functools.partial(_delta_kernel, seq_len=seq_len, n_sub=n_sub, latent=latent),
        grid=(n_seq,),
        in_specs=in_specs,
        out_specs=out_specs,
        out_shape=out_shape,
        scratch_shapes=[pltpu.VMEM((CONV_CH // LANES, seq_len + n_sub * CONV_PAD, LANES), F32),
                        pltpu.VMEM((CONV_CH // LANES, seq_len + n_sub * CONV_PAD, LANES), F32),
                        pltpu.VMEM((seq_len, C_QK_WIDTH), F32),
                        pltpu.VMEM((seq_len, C_QK_WIDTH), F32),
                        pltpu.VMEM((seq_len, C_WIDTH), F32),
                        pltpu.VMEM((N_DIR, seq_len, C_WIDTH), F32),
                        pltpu.VMEM((N_DIR, seq_len, C_WIDTH), F32),
                        pltpu.VMEM((N_DIR, seq_len, C_WIDTH), F32),
                        pltpu.VMEM((N_DIR, seq_len, C_WIDTH), F32),
                        pltpu.VMEM((nc, N_SYS, CHUNK, LANES), F32),
                        pltpu.VMEM((nc, N_SYS, 2 * CHUNK, LANES), BF16),
                        pltpu.VMEM((nc, N_SYS, 2 * CHUNK, LANES), BF16),
                        pltpu.VMEM((nc, N_SYS, 1, LANES), F32)],
        compiler_params=pltpu.CompilerParams(vmem_limit_bytes=VMEM_LIMIT),
        name="delta_latent" if latent else "delta_ctx",
    )(*args)


def _rope_tables():
    rows = DEC_SEQ // GRID_W
    row = jnp.repeat(jnp.arange(rows, dtype=F32), GRID_W)
    col = jnp.tile(jnp.arange(GRID_W, dtype=F32), rows)
    n_freq = HEAD_DIM // 4
    inv_freq = ROPE_THETA ** (-jnp.arange(n_freq, dtype=F32) / n_freq)
    ang_r = row[:, None] * inv_freq
    ang_c = col[:, None] * inv_freq
    ang = jnp.concatenate([ang_r, ang_r, ang_c, ang_c], axis=-1)
    cos, sin = jnp.cos(ang), jnp.sin(ang)
    first = (jnp.arange(HEAD_DIM) % 32) < 16
    sin_a = jnp.where(first, -sin, 0.0)
    sin_b = jnp.where(first, 0.0, sin)
    reps = LANES // HEAD_DIM
    return tuple(jnp.tile(t, (1, reps)) for t in (cos, sin_a, sin_b))


def _block_diag_state(s):
    eye = jnp.eye(B_HEADS, dtype=s.dtype)
    out = jnp.einsum('...hkv,hg->...hkgv', s, eye)
    return out.reshape(s.shape[:-3] + (B_QK_WIDTH, B_WIDTH))


def _pair_state(s):
    lead = s.shape[:-4]
    s = s.reshape(lead + (N_DIR, C_PAIRS, 2, C_DK, C_DV))
    s = jnp.moveaxis(s, -3, -2)
    return s.reshape(lead + (N_SYS, C_DK, 2 * C_DV))


def _unpair_state(s):
    lead = s.shape[:-3]
    s = s.reshape(lead + (N_DIR, C_PAIRS, C_DK, 2, C_DV))
    s = jnp.moveaxis(s, -2, -3)
    return s.reshape(lead + (N_DIR, C_HEADS, C_DK, C_DV))


def _diag_blocks(s_bd):
    s5 = s_bd.reshape(s_bd.shape[:-2] + (B_HEADS, B_DK, B_HEADS, B_DV))
    return jnp.stack([s5[..., h, :, h, :] for h in range(B_HEADS)], axis=-3)


def kernel(x_prompt, x_sample, cache_k, cache_v, state_gla, state_delta, c, c_ctx, w_mod, b_mod, norm_gains, w_in, qk_gain, w_gla_gate, b_gla_gate, gla_norm, conv_w, a_log, dt_bias, delta_norm, w_out, w_gate, w_up, w_down):
    xc = x_prompt.reshape(N_CTX_ROWS, D_MODEL)
    xl = x_sample.reshape(N_LAT_ROWS, D_MODEL)
    cond = jnp.concatenate([c_ctx[None, :], c, jnp.zeros((N_COND - 1 - DEC_BATCH, D_MODEL), F32)], axis=0)
    w_out_b = w_out.astype(BF16)
    w_gate_b = w_gate.astype(BF16)
    w_up_b = w_up.astype(BF16)
    w_down_b = w_down.astype(BF16)
    qk_gain2 = jnp.tile(qk_gain, (1, 1, LANES // HEAD_DIM))
    wg_p = jnp.zeros((DEPTH, LANES, N_DIR * B_QK_WIDTH), F32)
    for d in range(N_DIR):
        wg_p = wg_p.at[:, d * GATE_RANK:(d + 1) * GATE_RANK, d * B_QK_WIDTH:(d + 1) * B_QK_WIDTH].set(w_gla_gate[:, d])
    bg_p = b_gla_gate.reshape(DEPTH, 1, N_DIR * B_QK_WIDTH)
    gn_p = jnp.tile(gla_norm, (1, B_HEADS)).reshape(DEPTH, 1, B_WIDTH)
    dn_p = jnp.tile(delta_norm, (1, C_HEADS)).reshape(DEPTH, 1, C_WIDTH)
    al_p = jnp.zeros((DEPTH, 1, LANES), F32).at[:, 0, DECAY_LANE:DECAY_LANE + N_DIR * C_HEADS].set(
        a_log.reshape(DEPTH, N_DIR * C_HEADS))
    dtb_p = jnp.zeros((DEPTH, 1, LANES), F32).at[:, 0, DECAY_LANE:DECAY_LANE + N_DIR * C_HEADS].set(
        dt_bias.reshape(DEPTH, N_DIR * C_HEADS))
    cache_k2 = cache_k.reshape(DEC_BATCH, DEPTH, PAST_LEN, A_KV_WIDTH)
    cache_v2 = cache_v.reshape(DEC_BATCH, DEPTH, PAST_LEN, A_KV_WIDTH)
    s0_gla_bd = _block_diag_state(state_gla.astype(F32))
    s0_delta = _pair_state(state_delta.astype(F32))
    rope = _rope_tables()

    mods = _modulation(cond, w_mod, b_mod).reshape(DEPTH * N_COND, 1, 6 * D_MODEL)

    new_k, new_v, new_gla, new_delta = [], [], [], []
    for l in range(DEPTH):
        za, zb, zc = _in_projection(xc, xl, mods, norm_gains, w_in, l)
        oa_c, k_l, v_l = _attention(za, qk_gain2, l, False)
        (oa_l,) = _attention(za, qk_gain2, l, True, cache_k2, cache_v2, rope)
        ob_c, sg_l = _gla(zb, wg_p, bg_p, gn_p, l, False)
        (ob_l,) = _gla(zb, wg_p, bg_p, gn_p, l, True, s0_gla_bd)
        oc_c, sd_l = _delta(zc, zb, conv_w, al_p, dtb_p, dn_p, l, False)
        (oc_l,) = _delta(zc, zb, conv_w, al_p, dtb_p, dn_p, l, True, s0_delta)
        xc, xl = _out_ffn(xc, xl, (oa_c, oa_l, ob_c, ob_l, oc_c, oc_l), mods, norm_gains,
                          w_out_b, w_gate_b, w_up_b, w_down_b, l)
        new_k.append(k_l.reshape(BATCH, SEQ, A_KV_HEADS, HEAD_DIM))
        new_v.append(v_l.reshape(BATCH, SEQ, A_KV_HEADS, HEAD_DIM))
        new_gla.append(_diag_blocks(sg_l))
        new_delta.append(_unpair_state(sd_l))

    out_dtype = x_prompt.dtype
    y_prompt = xc.reshape(BATCH, SEQ, D_MODEL)
    y_sample = xl.reshape(DEC_BATCH, DEC_SEQ, D_MODEL)
    return (y_prompt, y_sample,
            jnp.stack(new_k, axis=1), jnp.stack(new_v, axis=1),
            jnp.stack(new_gla, axis=1).astype(out_dtype),
            jnp.stack(new_delta, axis=1).astype(out_dtype))
```

```python
import functools

import jax
import jax.numpy as jnp
import numpy as np
from jax import lax
from jax.experimental import pallas as pl
from jax.experimental.pallas import tpu as pltpu

F32 = jnp.float32
BF16 = jnp.bfloat16

D_MODEL = 1024
BATCH = 16
SEQ = 256
DEPTH = 4
DEC_BATCH = 4
DEC_SEQ = 1024
PAST_LEN = 512
GRID_W = 64
HEAD_DIM = 64
A_HEADS = 8
A_KV_HEADS = 2
A_REP = A_HEADS // A_KV_HEADS
A_WIDTH = A_HEADS * HEAD_DIM
A_KV_WIDTH = A_KV_HEADS * HEAD_DIM
ROPE_THETA = 10000.0
B_HEADS = 4
B_DK = 32
B_DV = 64
B_QK_WIDTH = B_HEADS * B_DK
B_WIDTH = B_HEADS * B_DV
GATE_RANK = 16
GATE_TAU = 16.0
C_HEADS = 4
C_DK = 64
C_DV = 64
C_QK_WIDTH = C_HEADS * C_DK
C_WIDTH = C_HEADS * C_DV
CONV_WIDTH = 5
CONV_CH = 2 * C_QK_WIDTH + C_WIDTH
CHUNK = 64
N_DIR = 2
MIX_WIDTH = A_WIDTH + B_WIDTH + C_WIDTH
D_FF = -(-(8 * D_MODEL) // (3 * 256)) * 256
EPS = 1e-6

LANES = 128
N_CTX_ROWS = BATCH * SEQ
N_LAT_ROWS = DEC_BATCH * DEC_SEQ
N_ROWS = N_CTX_ROWS + N_LAT_ROWS
N_COND = 8

ZA_W = A_WIDTH + 2 * A_KV_WIDTH
ZB_W = 2 * B_QK_WIDTH + 2 * B_WIDTH + LANES
ZC_W = 2 * C_QK_WIDTH + 2 * C_WIDTH
PROJ_PAD = ZA_W + ZB_W + ZC_W
SMALL_OFF = 2 * B_QK_WIDTH + 2 * B_WIDTH
NAT_SMALL = A_WIDTH + 2 * A_KV_WIDTH + 2 * B_QK_WIDTH + 2 * B_WIDTH
NAT_C = NAT_SMALL + N_DIR * GATE_RANK
NAT_BC = NAT_C + ZC_W
BETA_LANE = N_DIR * GATE_RANK
DECAY_LANE = BETA_LANE + N_DIR * C_HEADS

VMEM_LIMIT = 56 * 1024 * 1024


def _split(x):
    hi = x.astype(BF16)
    lo = (x - hi.astype(F32)).astype(BF16)
    return hi, lo


def _bdot(a, b):
    return jnp.dot(a.astype(BF16), b.astype(BF16), preferred_element_type=F32)


def _bdot_nt(a, b):
    return lax.dot_general(a.astype(BF16), b.astype(BF16), (((1,), (1,)), ((), ())),
                           preferred_element_type=F32)


def _bdot_tn(a, b):
    return lax.dot_general(a.astype(BF16), b.astype(BF16), (((0,), (0,)), ((), ())),
                           preferred_element_type=F32)


def _dot3(a, b):
    ah, al = _split(a)
    bh, bl = _split(b)
    return (jnp.dot(ah, bh, preferred_element_type=F32)
            + jnp.dot(ah, bl, preferred_element_type=F32)
            + jnp.dot(al, bh, preferred_element_type=F32))


def _dot_lhs_exact(m_bf16, x):
    xh, xl = _split(x)
    return (jnp.dot(m_bf16, xh, preferred_element_type=F32)
            + jnp.dot(m_bf16, xl, preferred_element_type=F32))


def _dot_rhs_exact(x, m_bf16):
    xh, xl = _split(x)
    return (jnp.dot(xh, m_bf16, preferred_element_type=F32)
            + jnp.dot(xl, m_bf16, preferred_element_type=F32))


def _group_sumsq(x, ones_g):
    return jnp.dot((x * x).astype(BF16), ones_g, preferred_element_type=F32)


def _group_ones(width, group):
    r = lax.broadcasted_iota(jnp.int32, (width, width), 0) // group
    c = lax.broadcasted_iota(jnp.int32, (width, width), 1) // group
    return (r == c).astype(BF16)


def _sigmoid(x):
    return 1.0 / (1.0 + jnp.exp(-x))


def _silu(x):
    return x * _sigmoid(x)


def _softplus(x):
    return jnp.maximum(x, 0.0) + jnp.log1p(jnp.exp(-jnp.abs(x)))


def _rms(x, g):
    return x * lax.rsqrt(jnp.mean(x * x, axis=-1, keepdims=True) + EPS) * g


def _chunk_tri(t, reverse):
    r = lax.broadcasted_iota(jnp.int32, (t, t), 0)
    c = lax.broadcasted_iota(jnp.int32, (t, t), 1)
    same = (r // CHUNK) == (c // CHUNK)
    tri = (c >= r) if reverse else (c <= r)
    return (same & tri).astype(BF16)


def _mod_row(i, tm):
    start = i * tm
    return jnp.where(start < N_CTX_ROWS, 0, 1 + (start - N_CTX_ROWS) // DEC_SEQ)


def _mod_kernel(cond_ref, w_ref, b_ref, o_ref):
    c = cond_ref[...]
    o_ref[0] = _bdot(_silu(c), w_ref[0]) + b_ref[0]


def _modulation(cond, w_mod, b_mod):
    tn = 1536
    return pl.pallas_call(
        _mod_kernel,
        grid=(DEPTH, 6 * D_MODEL // tn),
        in_specs=[pl.BlockSpec((N_COND, D_MODEL), lambda l, j: (0, 0)),
                  pl.BlockSpec((1, D_MODEL, tn), lambda l, j: (l, 0, j)),
                  pl.BlockSpec((1, 1, tn), lambda l, j: (l, 0, j))],
        out_specs=pl.BlockSpec((1, N_COND, tn), lambda l, j: (l, 0, j)),
        out_shape=jax.ShapeDtypeStruct((DEPTH, N_COND, 6 * D_MODEL), F32),
        compiler_params=pltpu.CompilerParams(vmem_limit_bytes=VMEM_LIMIT),
        name="adaln_mod",
    )(cond, w_mod, b_mod.reshape(DEPTH, 1, 6 * D_MODEL))


DENSE_TM = 512
N_CTX_BLOCKS = N_CTX_ROWS // DENSE_TM


def _ctx_map(i):
    return (jnp.minimum(i, N_CTX_BLOCKS - 1), 0)


def _lat_map(i):
    return (jnp.maximum(i - N_CTX_BLOCKS, 0), 0)


def _inproj_kernel(xc_ref, xl_ref, mod_ref, ng_ref, w_ref, za_ref, zb_ref, zc_ref, wp_s):
    @pl.when(pl.program_id(0) == 0)
    def _():
        n_gate = N_DIR * GATE_RANK
        n_bc = 2 * N_DIR * C_HEADS
        wp_s[:, 0:NAT_SMALL] = w_ref[0, :, 0:NAT_SMALL].astype(BF16)
        wp_s[:, NAT_SMALL:NAT_SMALL + n_gate] = w_ref[0, :, NAT_SMALL:NAT_C].astype(BF16)
        wp_s[:, NAT_SMALL + n_gate:NAT_SMALL + n_gate + n_bc] = w_ref[0, :, NAT_BC:NAT_BC + n_bc].astype(BF16)
        wp_s[:, NAT_SMALL + n_gate + n_bc:NAT_SMALL + LANES] = jnp.zeros(
            (D_MODEL, LANES - n_gate - n_bc), BF16)
        wp_s[:, NAT_SMALL + LANES:PROJ_PAD] = w_ref[0, :, NAT_C:NAT_BC].astype(BF16)

    x = jnp.where(pl.program_id(0) < N_CTX_BLOCKS, xc_ref[...], xl_ref[...])
    mod = mod_ref[0]
    shift = mod[:, 0:D_MODEL]
    scale = mod[:, D_MODEL:2 * D_MODEL]
    h = _rms(x, ng_ref[0, 0:1, :]) * (1.0 + scale) + shift
    hb = h.astype(BF16)
    za_ref[...] = jnp.dot(hb, wp_s[:, 0:ZA_W], preferred_element_type=F32)
    zb_ref[...] = jnp.dot(hb, wp_s[:, ZA_W:ZA_W + ZB_W], preferred_element_type=F32)
    zc_ref[...] = jnp.dot(hb, wp_s[:, ZA_W + ZB_W:PROJ_PAD], preferred_element_type=F32)


def _in_projection(xc, xl, mods, norm_gains, w_in, layer):
    tm = DENSE_TM
    proj_width = w_in.shape[-1]
    return pl.pallas_call(
        _inproj_kernel,
        grid=(N_ROWS // tm,),
        in_specs=[pl.BlockSpec((tm, D_MODEL), _ctx_map),
                  pl.BlockSpec((tm, D_MODEL), _lat_map),
                  pl.BlockSpec((1, 1, 6 * D_MODEL), lambda i: (layer * N_COND + _mod_row(i, tm), 0, 0)),
                  pl.BlockSpec((1, 4, D_MODEL), lambda i: (layer, 0, 0)),
                  pl.BlockSpec((1, D_MODEL, proj_width), lambda i: (layer, 0, 0),
                               pipeline_mode=pl.Buffered(1))],
        out_specs=[pl.BlockSpec((tm, ZA_W), lambda i: (i, 0)),
                   pl.BlockSpec((tm, ZB_W), lambda i: (i, 0)),
                   pl.BlockSpec((tm, ZC_W), lambda i: (i, 0))],
        out_shape=[jax.ShapeDtypeStruct((N_ROWS, ZA_W), F32),
                   jax.ShapeDtypeStruct((N_ROWS, ZB_W), F32),
                   jax.ShapeDtypeStruct((N_ROWS, ZC_W), F32)],
        scratch_shapes=[pltpu.VMEM((D_MODEL, PROJ_PAD), BF16)],
        compiler_params=pltpu.CompilerParams(
            dimension_semantics=("arbitrary",), vmem_limit_bytes=VMEM_LIMIT),
        name="in_proj",
    )(xc, xl, mods, norm_gains, w_in)


FF_TILE = 256


def _ffn_kernel(xc_ref, xl_ref, oac_ref, oal_ref, obc_ref, obl_ref, occ_ref, ocl_ref, mod_ref, ng_ref,
                wo_ref, wg_ref, wu_ref, wd_ref, yc_ref, yl_ref):
    is_ctx = pl.program_id(0) < N_CTX_BLOCKS
    x = jnp.where(is_ctx, xc_ref[...], xl_ref[...])
    mod = mod_ref[0]
    gate_m = mod[:, 2 * D_MODEL:3 * D_MODEL]
    shift_f = mod[:, 3 * D_MODEL:4 * D_MODEL]
    scale_f = mod[:, 4 * D_MODEL:5 * D_MODEL]
    gate_f = mod[:, 5 * D_MODEL:6 * D_MODEL]
    mix_in = jnp.concatenate([jnp.where(is_ctx, oac_ref[...], oal_ref[...]),
                              jnp.where(is_ctx, obc_ref[...], obl_ref[...]),
                              jnp.where(is_ctx, occ_ref[...], ocl_ref[...])], axis=-1)
    mix = _bdot(mix_in, wo_ref[0])
    x1 = x + gate_m * _rms(mix, ng_ref[0, 1:2, :])
    h = (_rms(x1, ng_ref[0, 2:3, :]) * (1.0 + scale_f) + shift_f).astype(BF16)
    f = jnp.zeros(x.shape, F32)
    for j in range(D_FF // FF_TILE):
        cols = slice(j * FF_TILE, (j + 1) * FF_TILE)
        g = jnp.dot(h, wg_ref[0, :, cols], preferred_element_type=F32)
        u = jnp.dot(h, wu_ref[0, :, cols], preferred_element_type=F32)
        f = f + _bdot(_silu(g) * u, wd_ref[0, cols, :])
    y = x1 + gate_f * _rms(f, ng_ref[0, 3:4, :])

    @pl.when(is_ctx)
    def _():
        yc_ref[...] = y

    @pl.when(jnp.logical_not(is_ctx))
    def _():
        yl_ref[...] = y


def _out_ffn(xc, xl, mixer_outs, mods, norm_gains, w_out_b, w_gate_b, w_up_b, w_down_b, layer):
    tm = DENSE_TM
    resident = dict(pipeline_mode=pl.Buffered(1))
    pair_specs = []
    for width in (D_MODEL, A_WIDTH, B_WIDTH, C_WIDTH):
        pair_specs += [pl.BlockSpec((tm, width), _ctx_map), pl.BlockSpec((tm, width), _lat_map)]
    return pl.pallas_call(
        _ffn_kernel,
        grid=(N_ROWS // tm,),
        in_specs=pair_specs + [
                  pl.BlockSpec((1, 1, 6 * D_MODEL), lambda i: (layer * N_COND + _mod_row(i, tm), 0, 0)),
                  pl.BlockSpec((1, 4, D_MODEL), lambda i: (layer, 0, 0)),
                  pl.BlockSpec((1, MIX_WIDTH, D_MODEL), lambda i: (layer, 0, 0), **resident),
                  pl.BlockSpec((1, D_MODEL, D_FF), lambda i: (layer, 0, 0), **resident),
                  pl.BlockSpec((1, D_MODEL, D_FF), lambda i: (layer, 0, 0), **resident),
                  pl.BlockSpec((1, D_FF, D_MODEL), lambda i: (layer, 0, 0), **resident)],
        out_specs=[pl.BlockSpec((tm, D_MODEL), _ctx_map), pl.BlockSpec((tm, D_MODEL), _lat_map)],
        out_shape=[jax.ShapeDtypeStruct((N_CTX_ROWS, D_MODEL), F32),
                   jax.ShapeDtypeStruct((N_LAT_ROWS, D_MODEL), F32)],
        compiler_params=pltpu.CompilerParams(
            dimension_semantics=("arbitrary",), vmem_limit_bytes=VMEM_LIMIT),
        name="out_ffn",
    )(xc, xl, *mixer_outs, mods, norm_gains, w_out_b, w_gate_b, w_up_b, w_down_b)


ATT_TQ = 512
ATT_LOOKAHEAD = 1


def _head_norm(x, gain_row, ones_g):
    ss = _group_sumsq(x, ones_g)
    return x * lax.rsqrt(ss * (1.0 / HEAD_DIM) + EPS) * gain_row


def _rope(x, cos, sin_a, sin_b):
    return x * cos + pltpu.roll(x, LANES - 16, 1) * sin_a + pltpu.roll(x, 16, 1) * sin_b


def _attn_kernel(*refs, seq_len, tq, latent):
    if latent:
        (q_ref, kv_ref, ck_ref, cv_ref, qkg_ref, cos_ref, sa_ref, sb_ref, o_ref, k_s, v_s) = refs
    else:
        (q_ref, kv_ref, qkg_ref, o_ref, kn_ref, vn_ref, k_s, v_s) = refs
    j = pl.program_id(1)
    ones_g = _group_ones(LANES, HEAD_DIM)
    gq = qkg_ref[0, 0:1, :]
    gk = qkg_ref[0, 1:2, :]
    past = PAST_LEN if latent else 0

    @pl.when(j == 0)
    def _():
        ka = kv_ref[:, 0:A_KV_WIDTH]
        va = kv_ref[:, A_KV_WIDTH:2 * A_KV_WIDTH]
        kn = _head_norm(ka, gk, ones_g)
        if latent:
            kr = _rope(kn, cos_ref[...], sa_ref[...], sb_ref[...])
            ck = ck_ref[...]
            cv = cv_ref[...]
            for g in range(A_KV_HEADS):
                cols = slice(g * HEAD_DIM, (g + 1) * HEAD_DIM)
                k_s[g, 0:past, :] = ck[:, cols].astype(BF16)
                v_s[g, 0:past, :] = cv[:, cols].astype(BF16)
                k_s[g, past:past + seq_len, :] = kr[:, cols].astype(BF16)
                v_s[g, past:past + seq_len, :] = va[:, cols].astype(BF16)
        else:
            kn_ref[...] = kn
            vn_ref[...] = va
            for g in range(A_KV_HEADS):
                cols = slice(g * HEAD_DIM, (g + 1) * HEAD_DIM)
                k_s[g] = kn[:, cols].astype(BF16)
                v_s[g] = va[:, cols].astype(BF16)

    if latent:
        rows = pl.ds(pl.multiple_of(j * tq, tq), tq)
        cos = cos_ref[rows, :]
        sin_a = sa_ref[rows, :]
        sin_b = sb_ref[rows, :]
    q_heads = []
    for t in range(A_WIDTH // LANES):
        qt = _head_norm(q_ref[:, t * LANES:(t + 1) * LANES], gq, ones_g)
        if latent:
            qt = _rope(qt, cos, sin_a, sin_b)
        qt = (qt * (HEAD_DIM ** -0.5)).astype(BF16)
        q_heads += [qt[:, hh * HEAD_DIM:(hh + 1) * HEAD_DIM] for hh in range(LANES // HEAD_DIM)]

    def scores(h):
        return lax.dot_general(q_heads[h], k_s[h // A_REP], (((1,), (1,)), ((), ())),
                               preferred_element_type=F32)

    outs = []
    pending = [scores(h) for h in range(ATT_LOOKAHEAD)]
    for h in range(A_HEADS):
        s = pending.pop(0)
        if h + ATT_LOOKAHEAD < A_HEADS:
            pending.append(scores(h + ATT_LOOKAHEAD))
        m = jnp.max(s, axis=-1, keepdims=True)
        p = jnp.exp(s - m)
        l = jnp.sum(p, axis=-1, keepdims=True)
        o = jnp.dot(p.astype(BF16), v_s[h // A_REP], preferred_element_type=F32)
        outs.append(o / l)
    o_ref[...] = jnp.concatenate(outs, axis=-1)


def _attention(za, qk_gain2, layer, latent, cache_k=None, cache_v=None, rope=None):
    seq_len = DEC_SEQ if latent else SEQ
    n_seq = DEC_BATCH if latent else BATCH
    row0 = N_CTX_ROWS // seq_len if latent else 0
    tq = min(ATT_TQ, seq_len)
    nq = seq_len // tq
    row0q = N_CTX_ROWS // tq if latent else 0
    s_len = seq_len + (PAST_LEN if latent else 0)
    in_specs = [pl.BlockSpec((tq, A_WIDTH), lambda s, j: (row0q + s * nq + j, 0)),
                pl.BlockSpec((seq_len, 2 * A_KV_WIDTH), lambda s, j: (row0 + s, 2))]
    args = [za, za]
    if latent:
        in_specs += [pl.BlockSpec((None, None, PAST_LEN, A_KV_WIDTH), lambda s, j: (s, layer, 0, 0)),
                     pl.BlockSpec((None, None, PAST_LEN, A_KV_WIDTH), lambda s, j: (s, layer, 0, 0))]
        args += [cache_k, cache_v]
    in_specs.append(pl.BlockSpec((1, 2, LANES), lambda s, j: (layer, 0, 0)))
    args.append(qk_gain2)
    if latent:
        in_specs += [pl.BlockSpec((seq_len, LANES), lambda s, j: (0, 0))] * 3
        args += list(rope)
    out_specs = [pl.BlockSpec((tq, A_WIDTH), lambda s, j: (s * nq + j, 0))]
    out_shape = [jax.ShapeDtypeStruct((n_seq * seq_len, A_WIDTH), F32)]
    if not latent:
        out_specs += [pl.BlockSpec((None, seq_len, A_KV_WIDTH), lambda s, j: (s, 0, 0))] * 2
        out_shape += [jax.ShapeDtypeStruct((n_seq, seq_len, A_KV_WIDTH), F32)] * 2
    return pl.pallas_call(
        functools.partial(_attn_kernel, seq_len=seq_len, tq=tq, latent=latent),
        grid=(n_seq, nq),
        in_specs=in_specs,
        out_specs=out_specs,
        out_shape=out_shape,
        scratch_shapes=[pltpu.VMEM((A_KV_HEADS, s_len, HEAD_DIM), BF16),
                        pltpu.VMEM((A_KV_HEADS, s_len, HEAD_DIM), BF16)],
        compiler_params=pltpu.CompilerParams(
            dimension_semantics=("arbitrary", "arbitrary"), vmem_limit_bytes=VMEM_LIMIT),
        name="attn_latent" if latent else "attn_ctx",
    )(*args)


def _gla_kernel(*refs, seq_len, n_sub, latent):
    if latent:
        zb_ref, wg_ref, bg_ref, gn_ref, s0_ref, ob_ref = refs
    else:
        zb_ref, wg_ref, bg_ref, gn_ref, ob_ref, sfin_ref = refs
    nc = seq_len // CHUNK
    q = zb_ref[:, 0:B_QK_WIDTH] * (B_DK ** -0.5)
    k = zb_ref[:, B_QK_WIDTH:2 * B_QK_WIDTH]
    v = zb_ref[:, 2 * B_QK_WIDTH:2 * B_QK_WIDTH + B_WIDTH]
    small = zb_ref[:, SMALL_OFF:SMALL_OFF + LANES]
    pre = _dot3(small, wg_ref[0]) + bg_ref[0]
    glog = (jnp.minimum(pre, 0.0) - jnp.log1p(jnp.exp(-jnp.abs(pre)))) * (1.0 / GATE_TAU)

    kr = lax.broadcasted_iota(jnp.int32, (B_QK_WIDTH, B_WIDTH), 0) // B_DK
    vc = lax.broadcasted_iota(jnp.int32, (B_QK_WIDTH, B_WIDTH), 1) // B_DV
    bd_kv = kr == vc
    sr = lax.broadcasted_iota(jnp.int32, (B_WIDTH, B_WIDTH), 0) // CHUNK
    sc = lax.broadcasted_iota(jnp.int32, (B_WIDTH, B_WIDTH), 1) // B_DV
    bd_sv = sr == sc
    tt = lax.broadcasted_iota(jnp.int32, (CHUNK, B_WIDTH), 0)
    ss = lax.broadcasted_iota(jnp.int32, (CHUNK, B_WIDTH), 1) % CHUNK

    q3 = q.reshape(nc, CHUNK, B_QK_WIDTH)
    k3 = k.reshape(nc, CHUNK, B_QK_WIDTH)
    intra, upds, decays, q_sts = [], [], [], []
    for d in range(N_DIR):
        g = glog[:, d * B_QK_WIDTH:(d + 1) * B_QK_WIDTH]
        tri = _chunk_tri(CUM_ROWS, d == 1)
        b = jnp.concatenate([_dot_lhs_exact(tri, g[r * CUM_ROWS:(r + 1) * CUM_ROWS, :])
                             for r in range(seq_len // CUM_ROWS)], axis=0)
        b3 = b.reshape(nc, CHUNK, B_QK_WIDTH)
        mid = b3[:, CHUNK // 2:CHUNK // 2 + 1, :]
        last = b3[:, 0:1, :] if d == 1 else b3[:, CHUNK - 1:CHUNK, :]
        q_in = q3 * jnp.exp(b3 - mid)
        k_in = k3 * jnp.exp(mid - b3)
        q_sts.append(q3 * jnp.exp(b3))
        k_st = k3 * jnp.exp(last - b3)
        decay_t = jnp.exp(jnp.broadcast_to(last, (nc, 8, B_QK_WIDTH)).reshape(nc * 8, B_QK_WIDTH)).T
        causal = (ss >= tt) if d == 1 else (ss <= tt)
        v_chunks = [v[c * CHUNK:(c + 1) * CHUNK, :] for c in range(nc)]
        atts = [_bdot(q_in[c], jnp.where(bd_kv, jnp.concatenate([k_in[c].T] * B_HEADS, axis=1), 0.0))
                for c in range(nc)]
        upds.append([jnp.where(bd_kv, _bdot_tn(k_st[c], v_chunks[c]), 0.0) for c in range(nc)])
        intra.append([_bdot(jnp.where(causal, atts[c], 0.0),
                            jnp.where(bd_sv, jnp.concatenate([v_chunks[c]] * B_HEADS, axis=0), 0.0))
                      for c in range(nc)])
        decays.append([decay_t[:, 8 * c:8 * c + 1] for c in range(nc)])
    o_dirs = []
    per_seq = nc // n_sub
    for d in range(N_DIR):
        o_chunks = [None] * nc
        for sq in range(n_sub):
            state = s0_ref[d] if latent else jnp.zeros((B_QK_WIDTH, B_WIDTH), F32)
            chunks = range(sq * per_seq, (sq + 1) * per_seq)
            for c in (reversed(chunks) if d == 1 else chunks):
                o_chunks[c] = intra[d][c] + _bdot(q_sts[d][c], state)
                state = decays[d][c] * state + upds[d][c]
            if not latent:
                sfin_ref[sq, d] = state
        o_dirs.append(jnp.concatenate(o_chunks, axis=0))
    o = o_dirs[0] + o_dirs[1]
    ms = _group_sumsq(o, _group_ones(B_WIDTH, B_DV)) * (1.0 / B_DV)
    r = zb_ref[:, 2 * B_QK_WIDTH + B_WIDTH:2 * B_QK_WIDTH + 2 * B_WIDTH]
    ob_ref[...] = o * lax.rsqrt(ms + EPS) * gn_ref[0] * _silu(r)


GLA_CTX_SEQS = 4


def _gla(zb, wg_p, bg_p, gn_p, layer, latent, s0_bd=None):
    n_sub = 1 if latent else GLA_CTX_SEQS
    seq_len = (DEC_SEQ if latent else SEQ) * n_sub
    n_seq = (DEC_BATCH if latent else BATCH) // n_sub
    row0 = N_CTX_ROWS // seq_len if latent else 0
    in_specs = [pl.BlockSpec((seq_len, ZB_W), lambda s: (row0 + s, 0)),
                pl.BlockSpec((1, LANES, N_DIR * B_QK_WIDTH), lambda s: (layer, 0, 0)),
                pl.BlockSpec((1, 1, N_DIR * B_QK_WIDTH), lambda s: (layer, 0, 0)),
                pl.BlockSpec((1, 1, B_WIDTH), lambda s: (layer, 0, 0))]
    args = [zb, wg_p, bg_p, gn_p]
    out_specs = [pl.BlockSpec((seq_len, B_WIDTH), lambda s: (s, 0))]
    out_shape = [jax.ShapeDtypeStruct((n_seq * seq_len, B_WIDTH), F32)]
    if latent:
        in_specs.append(pl.BlockSpec((None, None, N_DIR, B_QK_WIDTH, B_WIDTH), lambda s: (s, layer, 0, 0, 0)))
        args.append(s0_bd)
    else:
        out_specs.append(pl.BlockSpec((n_sub, N_DIR, B_QK_WIDTH, B_WIDTH), lambda s: (s, 0, 0, 0)))
        out_shape.append(jax.ShapeDtypeStruct((n_seq * n_sub, N_DIR, B_QK_WIDTH, B_WIDTH), F32))
    return pl.pallas_call(
        functools.partial(_gla_kernel, seq_len=seq_len, n_sub=n_sub, latent=latent),
        grid=(n_seq,),
        in_specs=in_specs,
        out_specs=out_specs,
        out_shape=out_shape,
        compiler_params=pltpu.CompilerParams(vmem_limit_bytes=VMEM_LIMIT),
        name="gla_latent" if latent else "gla_ctx",
    )(*args)


TRI_BASE = 8
GROUP_HEADS = 2
GW = GROUP_HEADS * C_DK
C_PAIRS = C_HEADS // GROUP_HEADS
N_SYS = N_DIR * C_PAIRS
PREP_CHUNKS = 4
CUM_ROWS = 256
CONV_PAD = 8


def _bd(y, mask):
    return jnp.where(mask, jnp.concatenate([y] * GROUP_HEADS, axis=0), 0.0).astype(BF16)


def _pair_tri_inverse(ms, bd_mask, tt, ss):
    pws = [jnp.where((tt // TRI_BASE) == (ss // TRI_BASE), -m, 0.0) for m in ms]
    invs = [(tt == ss).astype(F32) + pw for pw in pws]
    span = 2
    while span < TRI_BASE:
        pws = [_bdot(pw, _bd(pw, bd_mask)) for pw in pws]
        invs = [inv + _bdot(inv, _bd(pw, bd_mask)) for inv, pw in zip(invs, pws)]
        span *= 2
    size = 2 * TRI_BASE
    while size <= CHUNK:
        off = ((tt // size) == (ss // size)) & ((tt // (size // 2)) != (ss // (size // 2)))
        xcs = [_bdot(inv, _bd(jnp.where(off, m, 0.0), bd_mask)) for m, inv in zip(ms, invs)]
        invs = [inv - _bdot(xc, _bd(inv, bd_mask)) for inv, xc in zip(invs, xcs)]
        size *= 2
    return invs


def _delta_kernel(*refs, seq_len, n_sub, latent):
    if latent:
        (zc_ref, sm_ref, cw_ref, al_ref, dtb_ref, dn_ref, s0_ref, oc_ref,
         xp_s, yp_s, q_s, k_s, v_s, bx_s, gx_s, gr_s, o_s, u_s, wq_s, ak_s, eg_s) = refs
    else:
        (zc_ref, sm_ref, cw_ref, al_ref, dtb_ref, dn_ref, oc_ref, sfin_ref,
         xp_s, yp_s, q_s, k_s, v_s, bx_s, gx_s, gr_s, o_s, u_s, wq_s, ak_s, eg_s) = refs
    nc = seq_len // CHUNK
    per_seq = nc // n_sub

    len1 = seq_len // n_sub
    seg = len1 // 8 + 1
    sub = lax.broadcasted_iota(jnp.int32, (8, LANES), 0)
    half = CONV_WIDTH // 2
    for j in range(CONV_CH // LANES):
        cols = slice(j * LANES, (j + 1) * LANES)
        taps = [cw_ref[0, tap:tap + 1, cols] for tap in range(CONV_WIDTH)]
        for sq in range(n_sub):
            base = sq * 8 * seg
            xp_s[j, base:base + len1, :] = zc_ref[sq * len1:(sq + 1) * len1, cols]
            xp_s[j, base + len1:base + 8 * seg, :] = jnp.zeros((8 * seg - len1, LANES), F32)
            xr = [xp_s[j, pl.ds(base + a, 8, stride=seg), :] for a in range(seg)]
            nxt = [jnp.where(sub < 7, pltpu.roll(xr[a], 7, 0), 0.0) for a in range(half)]
            prv = [jnp.where(sub > 0, pltpu.roll(xr[seg - half + a], 1, 0), 0.0) for a in range(half)]
            window = prv + xr + nxt
            for a in range(seg):
                acc = window[a] * taps[0]
                for tap in range(1, CONV_WIDTH):
                    acc = acc + window[a + tap] * taps[tap]
                yp_s[j, pl.ds(base + a, 8, stride=seg), :] = acc
    y = _silu(jnp.concatenate(
        [jnp.concatenate([yp_s[j, sq * 8 * seg:sq * 8 * seg + len1, :] for sq in range(n_sub)], axis=0)
         for j in range(CONV_CH // LANES)], axis=1))
    ones_g = _group_ones(C_QK_WIDTH, C_DK)
    qc = y[:, 0:C_QK_WIDTH]
    kc = y[:, C_QK_WIDTH:2 * C_QK_WIDTH]
    q_s[...] = qc * lax.rsqrt(_group_sumsq(qc, ones_g) + EPS) * (C_DK ** -0.5)
    k_s[...] = kc * lax.rsqrt(_group_sumsq(kc, ones_g) + EPS)
    v_s[...] = y[:, 2 * C_QK_WIDTH:CONV_CH]

    small = sm_ref[...]
    beta = _sigmoid(small)
    glog = -jnp.exp(al_ref[0]) * _softplus(small + dtb_ref[0])
    lane_r = lax.broadcasted_iota(jnp.int32, (LANES, C_WIDTH), 0)
    head_c = lax.broadcasted_iota(jnp.int32, (LANES, C_WIDTH), 1) // C_DV
    cr = lax.broadcasted_iota(jnp.int32, (CUM_ROWS, CUM_ROWS), 0)
    cl = lax.broadcasted_iota(jnp.int32, (CUM_ROWS, CUM_ROWS), 1)
    same_chunk = ((cr // CHUNK) == (cl // CHUNK)).astype(BF16)
    on_diag = (lax.broadcasted_iota(jnp.int32, (CUM_ROWS, C_WIDTH), 0) % CHUNK
               == lax.broadcasted_iota(jnp.int32, (CUM_ROWS, C_WIDTH), 1) % C_DV)
    for d in range(N_DIR):
        bx_s[d] = _dot_rhs_exact(beta, (lane_r == BETA_LANE + d * C_HEADS + head_c).astype(BF16))
    pick_gam = [(lane_r == DECAY_LANE + d * C_HEADS + head_c).astype(BF16) for d in range(N_DIR)]
    tris = [_chunk_tri(CUM_ROWS, d == 1) for d in range(N_DIR)]
    blocks = [(d, slice(blk * CUM_ROWS, (blk + 1) * CUM_ROWS))
              for d in range(N_DIR) for blk in range(seq_len // CUM_ROWS)]
    cums = [_dot_lhs_exact(tris[d], glog[r, :]) for d, r in blocks]
    g_xs = [_dot_rhs_exact(cum, pick_gam[d]) for (d, r), cum in zip(blocks, cums)]
    g_rs = [_dot_lhs_exact(same_chunk, jnp.where(on_diag, g_x, 0.0)) for g_x in g_xs]
    for (d, r), g_x, g_r in zip(blocks, g_xs, g_rs):
        gx_s[d, r, :] = g_x
        gr_s[d, r, :] = g_r

    tt = lax.broadcasted_iota(jnp.int32, (CHUNK, GW), 0)
    ss = lax.broadcasted_iota(jnp.int32, (CHUNK, GW), 1) % CHUNK
    bd1 = (lax.broadcasted_iota(jnp.int32, (GW, GW), 0) // CHUNK
           == lax.broadcasted_iota(jnp.int32, (GW, GW), 1) // CHUNK)
    bd2 = jnp.concatenate([bd1, bd1], axis=1)

    prep = min(PREP_CHUNKS, nc)

    def prepare_chunks(step, carry):
        ids, ms, rhss, qgs, aks, egs = [], [], [], [], [], []
        for cj in range(prep):
            c = step * prep + cj
            rows = pl.ds(pl.multiple_of(c * CHUNK, CHUNK), CHUNK)
            k_t = k_s[rows, :].T
            for d in range(N_DIR):
                ahead = (tt - ss) if d == 1 else (ss - tt)
                last = 0 if d == 1 else CHUNK - 1
                for p in range(C_PAIRS):
                    lanes = slice(p * GW, (p + 1) * GW)
                    qp = q_s[rows, lanes]
                    kp = k_s[rows, lanes]
                    bx = bx_s[d, rows, lanes]
                    gx = gx_s[d, rows, lanes]
                    gr = gr_s[d, rows, lanes]
                    dec = jnp.exp(jnp.where(ahead <= 0, gx - gr, -jnp.inf))
                    kb = kp * bx
                    ma = lax.dot_general(jnp.concatenate([kb, qp], axis=0).astype(BF16), _bd(kp, bd1),
                                         (((1,), (1,)), ((), ())), preferred_element_type=F32)
                    e_gx = jnp.exp(gx)
                    k_t_pair = jnp.concatenate(
                        [k_t[p * GW + h * C_DK:p * GW + (h + 1) * C_DK, :] for h in range(GROUP_HEADS)], axis=1)
                    g_last = gx[last:last + 1, :]
                    ids.append((c, d * C_PAIRS + p))
                    ms.append(jnp.where(ahead < 0, ma[0:CHUNK] * dec, 0.0))
                    rhss.append(jnp.concatenate([v_s[rows, lanes] * bx, kb * e_gx], axis=1))
                    qgs.append(qp * e_gx)
                    aks.append(jnp.concatenate([ma[CHUNK:2 * CHUNK] * dec, k_t_pair * jnp.exp(g_last - gr)],
                                               axis=0).astype(BF16))
                    egs.append(jnp.exp(g_last))
        invs = _pair_tri_inverse(ms, bd1, tt, ss)
        sols = [_bdot(inv, _bd(rhs, bd2)) for inv, rhs in zip(invs, rhss)]
        m_sols = []
        for m, sol in zip(ms, sols):
            sh, sl = _split(sol)
            mh, ml = _split(m)
            shb, slb = _bd(sh, bd2), _bd(sl, bd2)
            m_sols.append(jnp.dot(mh, shb, preferred_element_type=F32)
                          + jnp.dot(mh, slb, preferred_element_type=F32)
                          + jnp.dot(ml, shb, preferred_element_type=F32))
        sols = [sol + _bdot(inv, _bd(rhs - sol - m_sol, bd2))
                for inv, rhs, sol, m_sol in zip(invs, rhss, sols, m_sols)]
        for (c, sy), sol, qg, ak, eg in zip(ids, sols, qgs, aks, egs):
            u_s[c, sy] = sol[:, 0:GW]
            wq_s[c, sy] = jnp.concatenate([sol[:, GW:2 * GW], qg], axis=0).astype(BF16)
            ak_s[c, sy] = ak
            eg_s[c, sy] = eg
        return carry

    lax.fori_loop(0, nc // prep, prepare_chunks, 0)

    def scan_chunk(i, state):
        ids = [(sq, d, sq * per_seq + ((per_seq - 1 - i) if d == 1 else i), p)
               for sq in range(n_sub) for d in range(N_DIR) for p in range(C_PAIRS)]
        states = [state[sq * N_SYS + d * C_PAIRS + p] for sq, d, c, p in ids]
        wq_states = [jnp.dot(wq_s[c, d * C_PAIRS + p], _bd(st, bd1), preferred_element_type=F32)
                     for (sq, d, c, p), st in zip(ids, states)]
        v_news = [u_s[c, d * C_PAIRS + p] - wqs[0:CHUNK] for (sq, d, c, p), wqs in zip(ids, wq_states)]
        aks = [jnp.dot(ak_s[c, d * C_PAIRS + p], _bd(v_new, bd1), preferred_element_type=F32)
               for (sq, d, c, p), v_new in zip(ids, v_news)]
        new_state = [eg_s[c, d * C_PAIRS + p] * st + ak[CHUNK:2 * CHUNK]
                     for (sq, d, c, p), st, ak in zip(ids, states, aks)]
        for (sq, d, c, p), wqs, ak in zip(ids, wq_states, aks):
            o_s[d, pl.ds(pl.multiple_of(c * CHUNK, CHUNK), CHUNK), p * GW:(p + 1) * GW] = (
                wqs[CHUNK:2 * CHUNK] + ak[0:CHUNK])
        return jnp.stack(new_state)

    if latent:
        state0 = s0_ref[...]
    else:
        state0 = jnp.zeros((n_sub * N_SYS, C_DK, GW), F32)
    state = lax.fori_loop(0, per_seq, scan_chunk, state0)

    o = o_s[0] + o_s[1]
    ms = _group_sumsq(o, _group_ones(C_WIDTH, C_DV)) * (1.0 / C_DV)
    gate = zc_ref[:, CONV_CH:CONV_CH + C_WIDTH]
    oc_ref[...] = o * lax.rsqrt(ms + EPS) * dn_ref[0] * _silu(gate)
    if not latent:
        sfin_ref[...] = state.reshape(n_sub, N_SYS, C_DK, GW)


DELTA_CTX_SEQS = 4


def _delta(zc, zb, conv_w, al_p, dtb_p, dn_p, layer, latent, state_delta=None):
    n_sub = 1 if latent else DELTA_CTX_SEQS
    seq_len = (DEC_SEQ if latent else SEQ) * n_sub
    n_seq = (DEC_BATCH if latent else BATCH) // n_sub
    row0 = N_CTX_ROWS // seq_len if latent else 0
    nc = seq_len // CHUNK
    in_specs = [pl.BlockSpec((seq_len, ZC_W), lambda s: (row0 + s, 0)),
                pl.BlockSpec((seq_len, LANES), lambda s: (row0 + s, SMALL_OFF // LANES)),
                pl.BlockSpec((1, CONV_WIDTH, CONV_CH), lambda s: (layer, 0, 0)),
                pl.BlockSpec((1, 1, LANES), lambda s: (layer, 0, 0)),
                pl.BlockSpec((1, 1, LANES), lambda s: (layer, 0, 0)),
                pl.BlockSpec((1, 1, C_WIDTH), lambda s: (layer, 0, 0))]
    args = [zc, zb, conv_w, al_p, dtb_p, dn_p]
    out_specs = [pl.BlockSpec((seq_len, C_WIDTH), lambda s: (s, 0))]
    out_shape = [jax.ShapeDtypeStruct((n_seq * seq_len, C_WIDTH), F32)]
    if latent:
        in_specs.append(pl.BlockSpec((None, None, N_SYS, C_DK, GW), lambda s: (s, layer, 0, 0, 0)))
        args.append(state_delta)
    else:
        out_specs.append(pl.BlockSpec((n_sub, N_SYS, C_DK, GW), lambda s: (s, 0, 0, 0)))
        out_shape.append(jax.ShapeDtypeStruct((n_seq * n_sub, N_SYS, C_DK, GW), F32))
    return pl.pallas_call(
        functools.partial(_delta_kernel, seq_len=seq_len, n_sub=n_sub, latent=latent),
        grid=(n_seq,),
        in_specs=in_specs,
        out_specs=out_specs,
        out_shape=out_shape,
        scratch_shapes=[pltpu.VMEM((CONV_CH // LANES, seq_len + n_sub * CONV_PAD, LANES), F32),
                        pltpu.VMEM((CONV_CH // LANES, seq_len + n_sub * CONV_PAD, LANES), F32),
                        pltpu.VMEM((seq_len, C_QK_WIDTH), F32),
                        pltpu.VMEM((seq_len, C_QK_WIDTH), F32),
                        pltpu.VMEM((seq_len, C_WIDTH), F32),
                        pltpu.VMEM((N_DIR, seq_len, C_WIDTH), F32),
                        pltpu.VMEM((N_DIR, seq_len, C_WIDTH), F32),
                        pltpu.VMEM((N_DIR, seq_len, C_WIDTH), F32),
                        pltpu.VMEM((N_DIR, seq_len, C_WIDTH), F32),
                        pltpu.VMEM((nc, N_SYS, CHUNK, GW), F32),
                        pltpu.VMEM((nc, N_SYS, 2 * CHUNK, GW), BF16),
                        pltpu.VMEM((nc, N_SYS, 2 * CHUNK, GW), BF16),
                        pltpu.VMEM((nc, N_SYS, 1, GW), F32)],
        compiler_params=pltpu.CompilerParams(vmem_limit_bytes=VMEM_LIMIT),
        name="delta_latent" if latent else "delta_ctx",
    )(*args)


def _rope_tables():
    rows = DEC_SEQ // GRID_W
    row = jnp.repeat(jnp.arange(rows, dtype=F32), GRID_W)
    col = jnp.tile(jnp.arange(GRID_W, dtype=F32), rows)
    n_freq = HEAD_DIM // 4
    inv_freq = ROPE_THETA ** (-jnp.arange(n_freq, dtype=F32) / n_freq)
    ang_r = row[:, None] * inv_freq
    ang_c = col[:, None] * inv_freq
    ang = jnp.concatenate([ang_r, ang_r, ang_c, ang_c], axis=-1)
    cos, sin = jnp.cos(ang), jnp.sin(ang)
    first = (jnp.arange(HEAD_DIM) % 32) < 16
    sin_a = jnp.where(first, -sin, 0.0)
    sin_b = jnp.where(first, 0.0, sin)
    reps = LANES // HEAD_DIM
    return tuple(jnp.tile(t, (1, reps)) for t in (cos, sin_a, sin_b))


def _block_diag_state(s):
    eye = jnp.eye(B_HEADS, dtype=s.dtype)
    out = jnp.einsum('...hkv,hg->...hkgv', s, eye)
    return out.reshape(s.shape[:-3] + (B_QK_WIDTH, B_WIDTH))


def _pair_state(s):
    lead = s.shape[:-4]
    s = s.reshape(lead + (N_DIR, C_PAIRS, GROUP_HEADS, C_DK, C_DV))
    s = jnp.moveaxis(s, -3, -2)
    return s.reshape(lead + (N_SYS, C_DK, GW))


def _unpair_state(s):
    lead = s.shape[:-3]
    s = s.reshape(lead + (N_DIR, C_PAIRS, C_DK, GROUP_HEADS, C_DV))
    s = jnp.moveaxis(s, -2, -3)
    return s.reshape(lead + (N_DIR, C_HEADS, C_DK, C_DV))


def _diag_blocks(s_bd):
    s5 = s_bd.reshape(s_bd.shape[:-2] + (B_HEADS, B_DK, B_HEADS, B_DV))
    return jnp.stack([s5[..., h, :, h, :] for h in range(B_HEADS)], axis=-3)


def kernel(x_prompt, x_sample, cache_k, cache_v, state_gla, state_delta, c, c_ctx, w_mod, b_mod, norm_gains, w_in, qk_gain, w_gla_gate, b_gla_gate, gla_norm, conv_w, a_log, dt_bias, delta_norm, w_out, w_gate, w_up, w_down):
    xc = x_prompt.reshape(N_CTX_ROWS, D_MODEL)
    xl = x_sample.reshape(N_LAT_ROWS, D_MODEL)
    cond = jnp.concatenate([c_ctx[None, :], c, jnp.zeros((N_COND - 1 - DEC_BATCH, D_MODEL), F32)], axis=0)
    w_out_b = w_out.astype(BF16)
    w_gate_b = w_gate.astype(BF16)
    w_up_b = w_up.astype(BF16)
    w_down_b = w_down.astype(BF16)
    qk_gain2 = jnp.tile(qk_gain, (1, 1, LANES // HEAD_DIM))
    wg_p = jnp.zeros((DEPTH, LANES, N_DIR * B_QK_WIDTH), F32)
    for d in range(N_DIR):
        wg_p = wg_p.at[:, d * GATE_RANK:(d + 1) * GATE_RANK, d * B_QK_WIDTH:(d + 1) * B_QK_WIDTH].set(w_gla_gate[:, d])
    bg_p = b_gla_gate.reshape(DEPTH, 1, N_DIR * B_QK_WIDTH)
    gn_p = jnp.tile(gla_norm, (1, B_HEADS)).reshape(DEPTH, 1, B_WIDTH)
    dn_p = jnp.tile(delta_norm, (1, C_HEADS)).reshape(DEPTH, 1, C_WIDTH)
    al_p = jnp.zeros((DEPTH, 1, LANES), F32).at[:, 0, DECAY_LANE:DECAY_LANE + N_DIR * C_HEADS].set(
        a_log.reshape(DEPTH, N_DIR * C_HEADS))
    dtb_p = jnp.zeros((DEPTH, 1, LANES), F32).at[:, 0, DECAY_LANE:DECAY_LANE + N_DIR * C_HEADS].set(
        dt_bias.reshape(DEPTH, N_DIR * C_HEADS))
    cache_k2 = cache_k.reshape(DEC_BATCH, DEPTH, PAST_LEN, A_KV_WIDTH)
    cache_v2 = cache_v.reshape(DEC_BATCH, DEPTH, PAST_LEN, A_KV_WIDTH)
    s0_gla_bd = _block_diag_state(state_gla.astype(F32))
    s0_delta = _pair_state(state_delta.astype(F32))
    rope = _rope_tables()

    mods = _modulation(cond, w_mod, b_mod).reshape(DEPTH * N_COND, 1, 6 * D_MODEL)

    new_k, new_v, new_gla, new_delta = [], [], [], []
    for l in range(DEPTH):
        za, zb, zc = _in_projection(xc, xl, mods, norm_gains, w_in, l)
        oa_c, k_l, v_l = _attention(za, qk_gain2, l, False)
        (oa_l,) = _attention(za, qk_gain2, l, True, cache_k2, cache_v2, rope)
        ob_c, sg_l = _gla(zb, wg_p, bg_p, gn_p, l, False)
        (ob_l,) = _gla(zb, wg_p, bg_p, gn_p, l, True, s0_gla_bd)
        oc_c, sd_l = _delta(zc, zb, conv_w, al_p, dtb_p, dn_p, l, False)
        (oc_l,) = _delta(zc, zb, conv_w, al_p, dtb_p, dn_p, l, True, s0_delta)
        xc, xl = _out_ffn(xc, xl, (oa_c, oa_l, ob_c, ob_l, oc_c, oc_l), mods, norm_gains,
                          w_out_b, w_gate_b, w_up_b, w_down_b, l)
        new_k.append(k_l.reshape(BATCH, SEQ, A_KV_HEADS, HEAD_DIM))
        new_v.append(v_l.reshape(BATCH, SEQ, A_KV_HEADS, HEAD_DIM))
        new_gla.append(_diag_blocks(sg_l))
        new_delta.append(_unpair_state(sd_l))

    out_dtype = x_prompt.dtype
    y_prompt = xc.reshape(BATCH, SEQ, D_MODEL)
    y_sample = xl.reshape(DEC_BATCH, DEC_SEQ, D_MODEL)
    return (y_prompt, y_sample,
            jnp.stack(new_k, axis=1), jnp.stack(new_v, axis=1),
            jnp.stack(new_gla, axis=1).astype(out_dtype),
            jnp.stack(new_delta, axis=1).astype(out_dtype))
```

```python
import functools

import jax
import jax.numpy as jnp
import numpy as np
from jax import lax
from jax.experimental import pallas as pl
from jax.experimental.pallas import tpu as pltpu

F32 = jnp.float32
BF16 = jnp.bfloat16

D_MODEL = 1024
BATCH = 16
SEQ = 256
DEPTH = 4
DEC_BATCH = 4
DEC_SEQ = 1024
PAST_LEN = 512
GRID_W = 64
HEAD_DIM = 64
A_HEADS = 8
A_KV_HEADS = 2
A_REP = A_HEADS // A_KV_HEADS
A_WIDTH = A_HEADS * HEAD_DIM
A_KV_WIDTH = A_KV_HEADS * HEAD_DIM
ROPE_THETA = 10000.0
B_HEADS = 4
B_DK = 32
B_DV = 64
B_QK_WIDTH = B_HEADS * B_DK
B_WIDTH = B_HEADS * B_DV
GATE_RANK = 16
GATE_TAU = 16.0
C_HEADS = 4
C_DK = 64
C_DV = 64
C_QK_WIDTH = C_HEADS * C_DK
C_WIDTH = C_HEADS * C_DV
CONV_WIDTH = 5
CONV_CH = 2 * C_QK_WIDTH + C_WIDTH
CHUNK = 64
N_DIR = 2
MIX_WIDTH = A_WIDTH + B_WIDTH + C_WIDTH
D_FF = -(-(8 * D_MODEL) // (3 * 256)) * 256
EPS = 1e-6

LANES = 128
N_CTX_ROWS = BATCH * SEQ
N_LAT_ROWS = DEC_BATCH * DEC_SEQ
N_ROWS = N_CTX_ROWS + N_LAT_ROWS
N_COND = 8

ZA_W = A_WIDTH + 2 * A_KV_WIDTH
ZB_W = 2 * B_QK_WIDTH + 2 * B_WIDTH + LANES
ZC_W = 2 * C_QK_WIDTH + 2 * C_WIDTH
PROJ_PAD = ZA_W + ZB_W + ZC_W
SMALL_OFF = 2 * B_QK_WIDTH + 2 * B_WIDTH
NAT_SMALL = A_WIDTH + 2 * A_KV_WIDTH + 2 * B_QK_WIDTH + 2 * B_WIDTH
NAT_C = NAT_SMALL + N_DIR * GATE_RANK
NAT_BC = NAT_C + ZC_W
BETA_LANE = N_DIR * GATE_RANK
DECAY_LANE = BETA_LANE + N_DIR * C_HEADS

VMEM_LIMIT = 56 * 1024 * 1024


def _split(x):
    hi = x.astype(BF16)
    lo = (x - hi.astype(F32)).astype(BF16)
    return hi, lo


def _bdot(a, b):
    return jnp.dot(a.astype(BF16), b.astype(BF16), preferred_element_type=F32)


def _bdot_nt(a, b):
    return lax.dot_general(a.astype(BF16), b.astype(BF16), (((1,), (1,)), ((), ())),
                           preferred_element_type=F32)


def _bdot_tn(a, b):
    return lax.dot_general(a.astype(BF16), b.astype(BF16), (((0,), (0,)), ((), ())),
                           preferred_element_type=F32)


def _dot3(a, b):
    ah, al = _split(a)
    bh, bl = _split(b)
    return (jnp.dot(ah, bh, preferred_element_type=F32)
            + jnp.dot(ah, bl, preferred_element_type=F32)
            + jnp.dot(al, bh, preferred_element_type=F32))


def _dot_lhs_exact(m_bf16, x):
    xh, xl = _split(x)
    return (jnp.dot(m_bf16, xh, preferred_element_type=F32)
            + jnp.dot(m_bf16, xl, preferred_element_type=F32))


def _dot_rhs_exact(x, m_bf16):
    xh, xl = _split(x)
    return (jnp.dot(xh, m_bf16, preferred_element_type=F32)
            + jnp.dot(xl, m_bf16, preferred_element_type=F32))


def _group_sumsq(x, ones_g):
    return jnp.dot((x * x).astype(BF16), ones_g, preferred_element_type=F32)


def _group_ones(width, group):
    r = lax.broadcasted_iota(jnp.int32, (width, width), 0) // group
    c = lax.broadcasted_iota(jnp.int32, (width, width), 1) // group
    return (r == c).astype(BF16)


def _sigmoid(x):
    return 1.0 / (1.0 + jnp.exp(-x))


def _silu(x):
    return x * _sigmoid(x)


def _softplus(x):
    return jnp.maximum(x, 0.0) + jnp.log1p(jnp.exp(-jnp.abs(x)))


def _rms(x, g):
    return x * lax.rsqrt(jnp.mean(x * x, axis=-1, keepdims=True) + EPS) * g


def _chunk_tri(t, reverse):
    r = lax.broadcasted_iota(jnp.int32, (t, t), 0)
    c = lax.broadcasted_iota(jnp.int32, (t, t), 1)
    same = (r // CHUNK) == (c // CHUNK)
    tri = (c >= r) if reverse else (c <= r)
    return (same & tri).astype(BF16)


def _mod_row(i, tm):
    start = i * tm
    return jnp.where(start < N_CTX_ROWS, 0, 1 + (start - N_CTX_ROWS) // DEC_SEQ)


def _mod_kernel(cond_ref, w_ref, b_ref, o_ref):
    c = cond_ref[...]
    o_ref[0] = _bdot(_silu(c), w_ref[0]) + b_ref[0]


def _modulation(cond, w_mod, b_mod):
    tn = 1536
    return pl.pallas_call(
        _mod_kernel,
        grid=(DEPTH, 6 * D_MODEL // tn),
        in_specs=[pl.BlockSpec((N_COND, D_MODEL), lambda l, j: (0, 0)),
                  pl.BlockSpec((1, D_MODEL, tn), lambda l, j: (l, 0, j)),
                  pl.BlockSpec((1, 1, tn), lambda l, j: (l, 0, j))],
        out_specs=pl.BlockSpec((1, N_COND, tn), lambda l, j: (l, 0, j)),
        out_shape=jax.ShapeDtypeStruct((DEPTH, N_COND, 6 * D_MODEL), F32),
        compiler_params=pltpu.CompilerParams(vmem_limit_bytes=VMEM_LIMIT),
        name="adaln_mod",
    )(cond, w_mod, b_mod.reshape(DEPTH, 1, 6 * D_MODEL))


DENSE_TM = 512
N_CTX_BLOCKS = N_CTX_ROWS // DENSE_TM


def _ctx_map(i):
    return (jnp.minimum(i, N_CTX_BLOCKS - 1), 0)


def _lat_map(i):
    return (jnp.maximum(i - N_CTX_BLOCKS, 0), 0)


def _inproj_kernel(xc_ref, xl_ref, mod_ref, ng_ref, w_ref, za_ref, zb_ref, zc_ref, wp_s):
    @pl.when(pl.program_id(0) == 0)
    def _():
        n_gate = N_DIR * GATE_RANK
        n_bc = 2 * N_DIR * C_HEADS
        wp_s[:, 0:NAT_SMALL] = w_ref[0, :, 0:NAT_SMALL].astype(BF16)
        wp_s[:, NAT_SMALL:NAT_SMALL + n_gate] = w_ref[0, :, NAT_SMALL:NAT_C].astype(BF16)
        wp_s[:, NAT_SMALL + n_gate:NAT_SMALL + n_gate + n_bc] = w_ref[0, :, NAT_BC:NAT_BC + n_bc].astype(BF16)
        wp_s[:, NAT_SMALL + n_gate + n_bc:NAT_SMALL + LANES] = jnp.zeros(
            (D_MODEL, LANES - n_gate - n_bc), BF16)
        wp_s[:, NAT_SMALL + LANES:PROJ_PAD] = w_ref[0, :, NAT_C:NAT_BC].astype(BF16)

    x = jnp.where(pl.program_id(0) < N_CTX_BLOCKS, xc_ref[...], xl_ref[...])
    mod = mod_ref[0]
    shift = mod[:, 0:D_MODEL]
    scale = mod[:, D_MODEL:2 * D_MODEL]
    h = _rms(x, ng_ref[0, 0:1, :]) * (1.0 + scale) + shift
    hb = h.astype(BF16)
    za_ref[...] = jnp.dot(hb, wp_s[:, 0:ZA_W], preferred_element_type=F32)
    zb_ref[...] = jnp.dot(hb, wp_s[:, ZA_W:ZA_W + ZB_W], preferred_element_type=F32)
    zc_ref[...] = jnp.dot(hb, wp_s[:, ZA_W + ZB_W:PROJ_PAD], preferred_element_type=F32)


def _in_projection(xc, xl, mods, norm_gains, w_in, layer):
    tm = DENSE_TM
    proj_width = w_in.shape[-1]
    return pl.pallas_call(
        _inproj_kernel,
        grid=(N_ROWS // tm,),
        in_specs=[pl.BlockSpec((tm, D_MODEL), _ctx_map),
                  pl.BlockSpec((tm, D_MODEL), _lat_map),
                  pl.BlockSpec((1, 1, 6 * D_MODEL), lambda i: (layer * N_COND + _mod_row(i, tm), 0, 0)),
                  pl.BlockSpec((1, 4, D_MODEL), lambda i: (layer, 0, 0)),
                  pl.BlockSpec((1, D_MODEL, proj_width), lambda i: (layer, 0, 0),
                               pipeline_mode=pl.Buffered(1))],
        out_specs=[pl.BlockSpec((tm, ZA_W), lambda i: (i, 0)),
                   pl.BlockSpec((tm, ZB_W), lambda i: (i, 0)),
                   pl.BlockSpec((tm, ZC_W), lambda i: (i, 0))],
        out_shape=[jax.ShapeDtypeStruct((N_ROWS, ZA_W), F32),
                   jax.ShapeDtypeStruct((N_ROWS, ZB_W), F32),
                   jax.ShapeDtypeStruct((N_ROWS, ZC_W), F32)],
        scratch_shapes=[pltpu.VMEM((D_MODEL, PROJ_PAD), BF16)],
        compiler_params=pltpu.CompilerParams(
            dimension_semantics=("arbitrary",), vmem_limit_bytes=VMEM_LIMIT),
        name="in_proj",
    )(xc, xl, mods, norm_gains, w_in)


FF_TILE = 256


def _ffn_kernel(xc_ref, xl_ref, oac_ref, oal_ref, obc_ref, obl_ref, occ_ref, ocl_ref, mod_ref, ng_ref,
                wo_ref, wg_ref, wu_ref, wd_ref, yc_ref, yl_ref):
    is_ctx = pl.program_id(0) < N_CTX_BLOCKS
    x = jnp.where(is_ctx, xc_ref[...], xl_ref[...])
    mod = mod_ref[0]
    gate_m = mod[:, 2 * D_MODEL:3 * D_MODEL]
    shift_f = mod[:, 3 * D_MODEL:4 * D_MODEL]
    scale_f = mod[:, 4 * D_MODEL:5 * D_MODEL]
    gate_f = mod[:, 5 * D_MODEL:6 * D_MODEL]
    mix_in = jnp.concatenate([jnp.where(is_ctx, oac_ref[...], oal_ref[...]),
                              jnp.where(is_ctx, obc_ref[...], obl_ref[...]),
                              jnp.where(is_ctx, occ_ref[...], ocl_ref[...])], axis=-1)
    mix = _bdot(mix_in, wo_ref[0])
    x1 = x + gate_m * _rms(mix, ng_ref[0, 1:2, :])
    h = (_rms(x1, ng_ref[0, 2:3, :]) * (1.0 + scale_f) + shift_f).astype(BF16)
    f = jnp.zeros(x.shape, F32)
    for j in range(D_FF // FF_TILE):
        cols = slice(j * FF_TILE, (j + 1) * FF_TILE)
        g = jnp.dot(h, wg_ref[0, :, cols], preferred_element_type=F32)
        u = jnp.dot(h, wu_ref[0, :, cols], preferred_element_type=F32)
        f = f + _bdot(_silu(g) * u, wd_ref[0, cols, :])
    y = x1 + gate_f * _rms(f, ng_ref[0, 3:4, :])

    @pl.when(is_ctx)
    def _():
        yc_ref[...] = y

    @pl.when(jnp.logical_not(is_ctx))
    def _():
        yl_ref[...] = y


def _out_ffn(xc, xl, mixer_outs, mods, norm_gains, w_out_b, w_gate_b, w_up_b, w_down_b, layer):
    tm = DENSE_TM
    resident = dict(pipeline_mode=pl.Buffered(1))
    pair_specs = []
    for width in (D_MODEL, A_WIDTH, B_WIDTH, C_WIDTH):
        pair_specs += [pl.BlockSpec((tm, width), _ctx_map), pl.BlockSpec((tm, width), _lat_map)]
    return pl.pallas_call(
        _ffn_kernel,
        grid=(N_ROWS // tm,),
        in_specs=pair_specs + [
                  pl.BlockSpec((1, 1, 6 * D_MODEL), lambda i: (layer * N_COND + _mod_row(i, tm), 0, 0)),
                  pl.BlockSpec((1, 4, D_MODEL), lambda i: (layer, 0, 0)),
                  pl.BlockSpec((1, MIX_WIDTH, D_MODEL), lambda i: (layer, 0, 0), **resident),
                  pl.BlockSpec((1, D_MODEL, D_FF), lambda i: (layer, 0, 0), **resident),
                  pl.BlockSpec((1, D_MODEL, D_FF), lambda i: (layer, 0, 0), **resident),
                  pl.BlockSpec((1, D_FF, D_MODEL), lambda i: (layer, 0, 0), **resident)],
        out_specs=[pl.BlockSpec((tm, D_MODEL), _ctx_map), pl.BlockSpec((tm, D_MODEL), _lat_map)],
        out_shape=[jax.ShapeDtypeStruct((N_CTX_ROWS, D_MODEL), F32),
                   jax.ShapeDtypeStruct((N_LAT_ROWS, D_MODEL), F32)],
        compiler_params=pltpu.CompilerParams(
            dimension_semantics=("arbitrary",), vmem_limit_bytes=VMEM_LIMIT),
        name="out_ffn",
    )(xc, xl, *mixer_outs, mods, norm_gains, w_out_b, w_gate_b, w_up_b, w_down_b)


ATT_TQ = 512
ATT_LOOKAHEAD = 1


def _head_norm(x, gain_row, ones_g):
    ss = _group_sumsq(x, ones_g)
    return x * lax.rsqrt(ss * (1.0 / HEAD_DIM) + EPS) * gain_row


def _rope(x, cos, sin_a, sin_b):
    return x * cos + pltpu.roll(x, LANES - 16, 1) * sin_a + pltpu.roll(x, 16, 1) * sin_b


def _attn_kernel(*refs, seq_len, tq, latent):
    if latent:
        (q_ref, kv_ref, ck_ref, cv_ref, qkg_ref, cos_ref, sa_ref, sb_ref, o_ref, k_s, v_s) = refs
    else:
        (q_ref, kv_ref, qkg_ref, o_ref, kn_ref, vn_ref, k_s, v_s) = refs
    j = pl.program_id(1)
    ones_g = _group_ones(LANES, HEAD_DIM)
    gq = qkg_ref[0, 0:1, :]
    gk = qkg_ref[0, 1:2, :]
    past = PAST_LEN if latent else 0

    @pl.when(j == 0)
    def _():
        ka = kv_ref[:, 0:A_KV_WIDTH]
        va = kv_ref[:, A_KV_WIDTH:2 * A_KV_WIDTH]
        kn = _head_norm(ka, gk, ones_g)
        if latent:
            kr = _rope(kn, cos_ref[...], sa_ref[...], sb_ref[...])
            ck = ck_ref[...]
            cv = cv_ref[...]
            for g in range(A_KV_HEADS):
                cols = slice(g * HEAD_DIM, (g + 1) * HEAD_DIM)
                k_s[g, 0:past, :] = ck[:, cols].astype(BF16)
                v_s[g, 0:past, :] = cv[:, cols].astype(BF16)
                k_s[g, past:past + seq_len, :] = kr[:, cols].astype(BF16)
                v_s[g, past:past + seq_len, :] = va[:, cols].astype(BF16)
        else:
            kn_ref[...] = kn
            vn_ref[...] = va
            for g in range(A_KV_HEADS):
                cols = slice(g * HEAD_DIM, (g + 1) * HEAD_DIM)
                k_s[g] = kn[:, cols].astype(BF16)
                v_s[g] = va[:, cols].astype(BF16)

    if latent:
        rows = pl.ds(pl.multiple_of(j * tq, tq), tq)
        cos = cos_ref[rows, :]
        sin_a = sa_ref[rows, :]
        sin_b = sb_ref[rows, :]
    q_heads = []
    for t in range(A_WIDTH // LANES):
        qt = _head_norm(q_ref[:, t * LANES:(t + 1) * LANES], gq, ones_g)
        if latent:
            qt = _rope(qt, cos, sin_a, sin_b)
        qt = (qt * (HEAD_DIM ** -0.5)).astype(BF16)
        q_heads += [qt[:, hh * HEAD_DIM:(hh + 1) * HEAD_DIM] for hh in range(LANES // HEAD_DIM)]

    def scores(h):
        return lax.dot_general(q_heads[h], k_s[h // A_REP], (((1,), (1,)), ((), ())),
                               preferred_element_type=F32)

    outs = []
    pending = [scores(h) for h in range(ATT_LOOKAHEAD)]
    for h in range(A_HEADS):
        s = pending.pop(0)
        if h + ATT_LOOKAHEAD < A_HEADS:
            pending.append(scores(h + ATT_LOOKAHEAD))
        m = jnp.max(s, axis=-1, keepdims=True)
        p = jnp.exp(s - m)
        l = jnp.sum(p, axis=-1, keepdims=True)
        o = jnp.dot(p.astype(BF16), v_s[h // A_REP], preferred_element_type=F32)
        outs.append(o / l)
    o_ref[...] = jnp.concatenate(outs, axis=-1)


def _attention(za, qk_gain2, layer, latent, cache_k=None, cache_v=None, rope=None):
    seq_len = DEC_SEQ if latent else SEQ
    n_seq = DEC_BATCH if latent else BATCH
    row0 = N_CTX_ROWS // seq_len if latent else 0
    tq = min(ATT_TQ, seq_len)
    nq = seq_len // tq
    row0q = N_CTX_ROWS // tq if latent else 0
    s_len = seq_len + (PAST_LEN if latent else 0)
    in_specs = [pl.BlockSpec((tq, A_WIDTH), lambda s, j: (row0q + s * nq + j, 0)),
                pl.BlockSpec((seq_len, 2 * A_KV_WIDTH), lambda s, j: (row0 + s, 2))]
    args = [za, za]
    if latent:
        in_specs += [pl.BlockSpec((None, None, PAST_LEN, A_KV_WIDTH), lambda s, j: (s, layer, 0, 0)),
                     pl.BlockSpec((None, None, PAST_LEN, A_KV_WIDTH), lambda s, j: (s, layer, 0, 0))]
        args += [cache_k, cache_v]
    in_specs.append(pl.BlockSpec((1, 2, LANES), lambda s, j: (layer, 0, 0)))
    args.append(qk_gain2)
    if latent:
        in_specs += [pl.BlockSpec((seq_len, LANES), lambda s, j: (0, 0))] * 3
        args += list(rope)
    out_specs = [pl.BlockSpec((tq, A_WIDTH), lambda s, j: (s * nq + j, 0))]
    out_shape = [jax.ShapeDtypeStruct((n_seq * seq_len, A_WIDTH), F32)]
    if not latent:
        out_specs += [pl.BlockSpec((None, seq_len, A_KV_WIDTH), lambda s, j: (s, 0, 0))] * 2
        out_shape += [jax.ShapeDtypeStruct((n_seq, seq_len, A_KV_WIDTH), F32)] * 2
    return pl.pallas_call(
        functools.partial(_attn_kernel, seq_len=seq_len, tq=tq, latent=latent),
        grid=(n_seq, nq),
        in_specs=in_specs,
        out_specs=out_specs,
        out_shape=out_shape,
        scratch_shapes=[pltpu.VMEM((A_KV_HEADS, s_len, HEAD_DIM), BF16),
                        pltpu.VMEM((A_KV_HEADS, s_len, HEAD_DIM), BF16)],
        compiler_params=pltpu.CompilerParams(
            dimension_semantics=("arbitrary", "arbitrary"), vmem_limit_bytes=VMEM_LIMIT),
        name="attn_latent" if latent else "attn_ctx",
    )(*args)


def _gla_kernel(*refs, seq_len, n_sub, latent):
    if latent:
        zb_ref, wg_ref, bg_ref, gn_ref, s0_ref, ob_ref = refs
    else:
        zb_ref, wg_ref, bg_ref, gn_ref, ob_ref, sfin_ref = refs
    nc = seq_len // CHUNK
    q = zb_ref[:, 0:B_QK_WIDTH] * (B_DK ** -0.5)
    k = zb_ref[:, B_QK_WIDTH:2 * B_QK_WIDTH]
    v = zb_ref[:, 2 * B_QK_WIDTH:2 * B_QK_WIDTH + B_WIDTH]
    small = zb_ref[:, SMALL_OFF:SMALL_OFF + LANES]
    pre = _dot3(small, wg_ref[0]) + bg_ref[0]
    glog = (jnp.minimum(pre, 0.0) - jnp.log1p(jnp.exp(-jnp.abs(pre)))) * (1.0 / GATE_TAU)

    kr = lax.broadcasted_iota(jnp.int32, (B_QK_WIDTH, B_WIDTH), 0) // B_DK
    vc = lax.broadcasted_iota(jnp.int32, (B_QK_WIDTH, B_WIDTH), 1) // B_DV
    bd_kv = kr == vc
    sr = lax.broadcasted_iota(jnp.int32, (B_WIDTH, B_WIDTH), 0) // CHUNK
    sc = lax.broadcasted_iota(jnp.int32, (B_WIDTH, B_WIDTH), 1) // B_DV
    bd_sv = sr == sc
    tt = lax.broadcasted_iota(jnp.int32, (CHUNK, B_WIDTH), 0)
    ss = lax.broadcasted_iota(jnp.int32, (CHUNK, B_WIDTH), 1) % CHUNK

    q3 = q.reshape(nc, CHUNK, B_QK_WIDTH)
    k3 = k.reshape(nc, CHUNK, B_QK_WIDTH)
    intra, upds, decays, q_sts = [], [], [], []
    for d in range(N_DIR):
        g = glog[:, d * B_QK_WIDTH:(d + 1) * B_QK_WIDTH]
        tri = _chunk_tri(CUM_ROWS, d == 1)
        b = jnp.concatenate([_dot_lhs_exact(tri, g[r * CUM_ROWS:(r + 1) * CUM_ROWS, :])
                             for r in range(seq_len // CUM_ROWS)], axis=0)
        b3 = b.reshape(nc, CHUNK, B_QK_WIDTH)
        mid = b3[:, CHUNK // 2:CHUNK // 2 + 1, :]
        last = b3[:, 0:1, :] if d == 1 else b3[:, CHUNK - 1:CHUNK, :]
        q_in = q3 * jnp.exp(b3 - mid)
        k_in = k3 * jnp.exp(mid - b3)
        q_sts.append(q3 * jnp.exp(b3))
        k_st = k3 * jnp.exp(last - b3)
        decay_t = jnp.exp(jnp.broadcast_to(last, (nc, 8, B_QK_WIDTH)).reshape(nc * 8, B_QK_WIDTH)).T
        causal = (ss >= tt) if d == 1 else (ss <= tt)
        v_chunks = [v[c * CHUNK:(c + 1) * CHUNK, :] for c in range(nc)]
        atts = [_bdot(q_in[c], jnp.where(bd_kv, jnp.concatenate([k_in[c].T] * B_HEADS, axis=1), 0.0))
                for c in range(nc)]
        upds.append([jnp.where(bd_kv, _bdot_tn(k_st[c], v_chunks[c]), 0.0) for c in range(nc)])
        intra.append([_bdot(jnp.where(causal, atts[c], 0.0),
                            jnp.where(bd_sv, jnp.concatenate([v_chunks[c]] * B_HEADS, axis=0), 0.0))
                      for c in range(nc)])
        decays.append([decay_t[:, 8 * c:8 * c + 1] for c in range(nc)])
    o_dirs = []
    per_seq = nc // n_sub
    for d in range(N_DIR):
        o_chunks = [None] * nc
        for sq in range(n_sub):
            state = s0_ref[d] if latent else jnp.zeros((B_QK_WIDTH, B_WIDTH), F32)
            chunks = range(sq * per_seq, (sq + 1) * per_seq)
            for c in (reversed(chunks) if d == 1 else chunks):
                o_chunks[c] = intra[d][c] + _bdot(q_sts[d][c], state)
                state = decays[d][c] * state + upds[d][c]
            if not latent:
                sfin_ref[sq, d] = state
        o_dirs.append(jnp.concatenate(o_chunks, axis=0))
    o = o_dirs[0] + o_dirs[1]
    ms = _group_sumsq(o, _group_ones(B_WIDTH, B_DV)) * (1.0 / B_DV)
    r = zb_ref[:, 2 * B_QK_WIDTH + B_WIDTH:2 * B_QK_WIDTH + 2 * B_WIDTH]
    ob_ref[...] = o * lax.rsqrt(ms + EPS) * gn_ref[0] * _silu(r)


GLA_CTX_SEQS = 4


def _gla(zb, wg_p, bg_p, gn_p, layer, latent, s0_bd=None):
    n_sub = 1 if latent else GLA_CTX_SEQS
    seq_len = (DEC_SEQ if latent else SEQ) * n_sub
    n_seq = (DEC_BATCH if latent else BATCH) // n_sub
    row0 = N_CTX_ROWS // seq_len if latent else 0
    in_specs = [pl.BlockSpec((seq_len, ZB_W), lambda s: (row0 + s, 0)),
                pl.BlockSpec((1, LANES, N_DIR * B_QK_WIDTH), lambda s: (layer, 0, 0)),
                pl.BlockSpec((1, 1, N_DIR * B_QK_WIDTH), lambda s: (layer, 0, 0)),
                pl.BlockSpec((1, 1, B_WIDTH), lambda s: (layer, 0, 0))]
    args = [zb, wg_p, bg_p, gn_p]
    out_specs = [pl.BlockSpec((seq_len, B_WIDTH), lambda s: (s, 0))]
    out_shape = [jax.ShapeDtypeStruct((n_seq * seq_len, B_WIDTH), F32)]
    if latent:
        in_specs.append(pl.BlockSpec((None, None, N_DIR, B_QK_WIDTH, B_WIDTH), lambda s: (s, layer, 0, 0, 0)))
        args.append(s0_bd)
    else:
        out_specs.append(pl.BlockSpec((n_sub, N_DIR, B_QK_WIDTH, B_WIDTH), lambda s: (s, 0, 0, 0)))
        out_shape.append(jax.ShapeDtypeStruct((n_seq * n_sub, N_DIR, B_QK_WIDTH, B_WIDTH), F32))
    return pl.pallas_call(
        functools.partial(_gla_kernel, seq_len=seq_len, n_sub=n_sub, latent=latent),
        grid=(n_seq,),
        in_specs=in_specs,
        out_specs=out_specs,
        out_shape=out_shape,
        compiler_params=pltpu.CompilerParams(vmem_limit_bytes=VMEM_LIMIT),
        name="gla_latent" if latent else "gla_ctx",
    )(*args)


TRI_BASE = 8
GROUP_HEADS = 2
GW = GROUP_HEADS * C_DK
C_PAIRS = C_HEADS // GROUP_HEADS
N_SYS = N_DIR * C_PAIRS
PREP_CHUNKS = 4
CUM_ROWS = 256
CONV_PAD = 8


def _bd(y, mask):
    return jnp.where(mask, jnp.concatenate([y] * GROUP_HEADS, axis=0), 0.0).astype(BF16)


def _pair_tri_inverse(ms, bd_mask, tt, ss):
    pws = [jnp.where((tt // TRI_BASE) == (ss // TRI_BASE), -m, 0.0) for m in ms]
    invs = [(tt == ss).astype(F32) + pw for pw in pws]
    span = 2
    while span < TRI_BASE:
        pws = [_bdot(pw, _bd(pw, bd_mask)) for pw in pws]
        invs = [inv + _bdot(inv, _bd(pw, bd_mask)) for inv, pw in zip(invs, pws)]
        span *= 2
    size = 2 * TRI_BASE
    while size <= CHUNK:
        off = ((tt // size) == (ss // size)) & ((tt // (size // 2)) != (ss // (size // 2)))
        xcs = [_bdot(inv, _bd(jnp.where(off, m, 0.0), bd_mask)) for m, inv in zip(ms, invs)]
        invs = [inv - _bdot(xc, _bd(inv, bd_mask)) for inv, xc in zip(invs, xcs)]
        size *= 2
    return invs


def _delta_kernel(*refs, seq_len, n_sub, latent):
    if latent:
        (zc_ref, sm_ref, cw_ref, al_ref, dtb_ref, dn_ref, s0_ref, oc_ref,
         xp_s, yp_s, q_s, k_s, v_s, bx_s, gx_s, gr_s, o_s, u_s, wq_s, ak_s, eg_s) = refs
    else:
        (zc_ref, sm_ref, cw_ref, al_ref, dtb_ref, dn_ref, oc_ref, sfin_ref,
         xp_s, yp_s, q_s, k_s, v_s, bx_s, gx_s, gr_s, o_s, u_s, wq_s, ak_s, eg_s) = refs
    nc = seq_len // CHUNK
    per_seq = nc // n_sub

    len1 = seq_len // n_sub
    seg = len1 // 8 + 1
    sub = lax.broadcasted_iota(jnp.int32, (8, LANES), 0)
    half = CONV_WIDTH // 2
    for j in range(CONV_CH // LANES):
        cols = slice(j * LANES, (j + 1) * LANES)
        taps = [cw_ref[0, tap:tap + 1, cols] for tap in range(CONV_WIDTH)]
        for sq in range(n_sub):
            base = sq * 8 * seg
            xp_s[j, base:base + len1, :] = zc_ref[sq * len1:(sq + 1) * len1, cols]
            xp_s[j, base + len1:base + 8 * seg, :] = jnp.zeros((8 * seg - len1, LANES), F32)
            xr = [xp_s[j, pl.ds(base + a, 8, stride=seg), :] for a in range(seg)]
            nxt = [jnp.where(sub < 7, pltpu.roll(xr[a], 7, 0), 0.0) for a in range(half)]
            prv = [jnp.where(sub > 0, pltpu.roll(xr[seg - half + a], 1, 0), 0.0) for a in range(half)]
            window = prv + xr + nxt
            for a in range(seg):
                acc = window[a] * taps[0]
                for tap in range(1, CONV_WIDTH):
                    acc = acc + window[a + tap] * taps[tap]
                yp_s[j, pl.ds(base + a, 8, stride=seg), :] = acc
    y = _silu(jnp.concatenate(
        [jnp.concatenate([yp_s[j, sq * 8 * seg:sq * 8 * seg + len1, :] for sq in range(n_sub)], axis=0)
         for j in range(CONV_CH // LANES)], axis=1))
    ones_g = _group_ones(C_QK_WIDTH, C_DK)
    qc = y[:, 0:C_QK_WIDTH]
    kc = y[:, C_QK_WIDTH:2 * C_QK_WIDTH]
    q_s[...] = qc * lax.rsqrt(_group_sumsq(qc, ones_g) + EPS) * (C_DK ** -0.5)
    k_s[...] = kc * lax.rsqrt(_group_sumsq(kc, ones_g) + EPS)
    v_s[...] = y[:, 2 * C_QK_WIDTH:CONV_CH]

    small = sm_ref[...]
    beta = _sigmoid(small)
    glog = -jnp.exp(al_ref[0]) * _softplus(small + dtb_ref[0])
    lane_r = lax.broadcasted_iota(jnp.int32, (LANES, C_WIDTH), 0)
    head_c = lax.broadcasted_iota(jnp.int32, (LANES, C_WIDTH), 1) // C_DV
    cr = lax.broadcasted_iota(jnp.int32, (CUM_ROWS, CUM_ROWS), 0)
    cl = lax.broadcasted_iota(jnp.int32, (CUM_ROWS, CUM_ROWS), 1)
    same_chunk = ((cr // CHUNK) == (cl // CHUNK)).astype(BF16)
    on_diag = (lax.broadcasted_iota(jnp.int32, (CUM_ROWS, C_WIDTH), 0) % CHUNK
               == lax.broadcasted_iota(jnp.int32, (CUM_ROWS, C_WIDTH), 1) % C_DV)
    for d in range(N_DIR):
        bx_s[d] = _dot_rhs_exact(beta, (lane_r == BETA_LANE + d * C_HEADS + head_c).astype(BF16))
    pick_gam = [(lane_r == DECAY_LANE + d * C_HEADS + head_c).astype(BF16) for d in range(N_DIR)]
    tris = [_chunk_tri(CUM_ROWS, d == 1) for d in range(N_DIR)]
    blocks = [(d, slice(blk * CUM_ROWS, (blk + 1) * CUM_ROWS))
              for d in range(N_DIR) for blk in range(seq_len // CUM_ROWS)]
    cums = [_dot_lhs_exact(tris[d], glog[r, :]) for d, r in blocks]
    g_xs = [_dot_rhs_exact(cum, pick_gam[d]) for (d, r), cum in zip(blocks, cums)]
    g_rs = [_dot_lhs_exact(same_chunk, jnp.where(on_diag, g_x, 0.0)) for g_x in g_xs]
    for (d, r), g_x, g_r in zip(blocks, g_xs, g_rs):
        gx_s[d, r, :] = g_x
        gr_s[d, r, :] = g_r

    tt = lax.broadcasted_iota(jnp.int32, (CHUNK, GW), 0)
    ss = lax.broadcasted_iota(jnp.int32, (CHUNK, GW), 1) % CHUNK
    bd1 = (lax.broadcasted_iota(jnp.int32, (GW, GW), 0) // CHUNK
           == lax.broadcasted_iota(jnp.int32, (GW, GW), 1) // CHUNK)
    bd2 = jnp.concatenate([bd1, bd1], axis=1)

    prep = min(PREP_CHUNKS, nc)

    def prepare_chunks(step, carry):
        ids, ms, rhss, qgs, aks, egs = [], [], [], [], [], []
        for cj in range(prep):
            c = step * prep + cj
            rows = pl.ds(pl.multiple_of(c * CHUNK, CHUNK), CHUNK)
            k_t = k_s[rows, :].T
            for d in range(N_DIR):
                ahead = (tt - ss) if d == 1 else (ss - tt)
                last = 0 if d == 1 else CHUNK - 1
                for p in range(C_PAIRS):
                    lanes = slice(p * GW, (p + 1) * GW)
                    qp = q_s[rows, lanes]
                    kp = k_s[rows, lanes]
                    bx = bx_s[d, rows, lanes]
                    gx = gx_s[d, rows, lanes]
                    gr = gr_s[d, rows, lanes]
                    dec = jnp.exp(jnp.where(ahead <= 0, gx - gr, -jnp.inf))
                    kb = kp * bx
                    ma = lax.dot_general(jnp.concatenate([kb, qp], axis=0).astype(BF16), _bd(kp, bd1),
                                         (((1,), (1,)), ((), ())), preferred_element_type=F32)
                    e_gx = jnp.exp(gx)
                    k_t_pair = jnp.concatenate(
                        [k_t[p * GW + h * C_DK:p * GW + (h + 1) * C_DK, :] for h in range(GROUP_HEADS)], axis=1)
                    g_last = gx[last:last + 1, :]
                    ids.append((c, d * C_PAIRS + p))
                    ms.append(jnp.where(ahead < 0, ma[0:CHUNK] * dec, 0.0))
                    rhss.append(jnp.concatenate([v_s[rows, lanes] * bx, kb * e_gx], axis=1))
                    qgs.append(qp * e_gx)
                    aks.append(jnp.concatenate([ma[CHUNK:2 * CHUNK] * dec, k_t_pair * jnp.exp(g_last - gr)],
                                               axis=0).astype(BF16))
                    egs.append(jnp.exp(g_last))
        invs = _pair_tri_inverse(ms, bd1, tt, ss)
        sols = [_bdot(inv, _bd(rhs, bd2)) for inv, rhs in zip(invs, rhss)]
        m_sols = []
        for m, sol in zip(ms, sols):
            sh, sl = _split(sol)
            mh, ml = _split(m)
            shb, slb = _bd(sh, bd2), _bd(sl, bd2)
            m_sols.append(jnp.dot(mh, shb, preferred_element_type=F32)
                          + jnp.dot(mh, slb, preferred_element_type=F32)
                          + jnp.dot(ml, shb, preferred_element_type=F32))
        sols = [sol + _bdot(inv, _bd(rhs - sol - m_sol, bd2))
                for inv, rhs, sol, m_sol in zip(invs, rhss, sols, m_sols)]
        for (c, sy), sol, qg, ak, eg in zip(ids, sols, qgs, aks, egs):
            u_s[c, sy] = sol[:, 0:GW]
            wq_s[c, sy] = jnp.concatenate([sol[:, GW:2 * GW], qg], axis=0).astype(BF16)
            ak_s[c, sy] = ak
            eg_s[c, sy] = eg
        return carry

    lax.fori_loop(0, nc // prep, prepare_chunks, 0)

    def scan_chunk(i, state):
        ids = [(sq, d, sq * per_seq + ((per_seq - 1 - i) if d == 1 else i), p)
               for sq in range(n_sub) for d in range(N_DIR) for p in range(C_PAIRS)]
        states = [state[sq * N_SYS + d * C_PAIRS + p] for sq, d, c, p in ids]
        wq_states = [jnp.dot(wq_s[c, d * C_PAIRS + p], _bd(st, bd1), preferred_element_type=F32)
                     for (sq, d, c, p), st in zip(ids, states)]
        v_news = [u_s[c, d * C_PAIRS + p] - wqs[0:CHUNK] for (sq, d, c, p), wqs in zip(ids, wq_states)]
        aks = [jnp.dot(ak_s[c, d * C_PAIRS + p], _bd(v_new, bd1), preferred_element_type=F32)
               for (sq, d, c, p), v_new in zip(ids, v_news)]
        new_state = [eg_s[c, d * C_PAIRS + p] * st + ak[CHUNK:2 * CHUNK]
                     for (sq, d, c, p), st, ak in zip(ids, states, aks)]
        for (sq, d, c, p), wqs, ak in zip(ids, wq_states, aks):
            o_s[d, pl.ds(pl.multiple_of(c * CHUNK, CHUNK), CHUNK), p * GW:(p + 1) * GW] = (
                wqs[CHUNK:2 * CHUNK] + ak[0:CHUNK])
        return jnp.stack(new_state)

    if latent:
        state0 = s0_ref[...]
    else:
        state0 = jnp.zeros((n_sub * N_SYS, C_DK, GW), F32)
    state = lax.fori_loop(0, per_seq, scan_chunk, state0)

    o = o_s[0] + o_s[1]
    ms = _group_sumsq(o, _group_ones(C_WIDTH, C_DV)) * (1.0 / C_DV)
    gate = zc_ref[:, CONV_CH:CONV_CH + C_WIDTH]
    oc_ref[...] = o * lax.rsqrt(ms + EPS) * dn_ref[0] * _silu(gate)
    if not latent:
        sfin_ref[...] = state.reshape(n_sub, N_SYS, C_DK, GW)


DELTA_CTX_SEQS = 4


def _delta(zc, zb, conv_w, al_p, dtb_p, dn_p, layer, latent, state_delta=None):
    n_sub = 1 if latent else DELTA_CTX_SEQS
    seq_len = (DEC_SEQ if latent else SEQ) * n_sub
    n_seq = (DEC_BATCH if latent else BATCH) // n_sub
    row0 = N_CTX_ROWS // seq_len if latent else 0
    nc = seq_len // CHUNK
    in_specs = [pl.BlockSpec((seq_len, ZC_W), lambda s: (row0 + s, 0)),
                pl.BlockSpec((seq_len, LANES), lambda s: (row0 + s, SMALL_OFF // LANES)),
                pl.BlockSpec((1, CONV_WIDTH, CONV_CH), lambda s: (layer, 0, 0)),
                pl.BlockSpec((1, 1, LANES), lambda s: (layer, 0, 0)),
                pl.BlockSpec((1, 1, LANES), lambda s: (layer, 0, 0)),
                pl.BlockSpec((1, 1, C_WIDTH), lambda s: (layer, 0, 0))]
    args = [zc, zb, conv_w, al_p, dtb_p, dn_p]
    out_specs = [pl.BlockSpec((seq_len, C_WIDTH), lambda s: (s, 0))]
    out_shape = [jax.ShapeDtypeStruct((n_seq * seq_len, C_WIDTH), F32)]
    if latent:
        in_specs.append(pl.BlockSpec((None, None, N_SYS, C_DK, GW), lambda s: (s, layer, 0, 0, 0)))
        args.append(state_delta)
    else:
        out_specs.append(pl.BlockSpec((n_sub, N_SYS, C_DK, GW), lambda s: (s, 0, 0, 0)))
        out_shape.append(jax.ShapeDtypeStruct((n_seq * n_sub, N_SYS, C_DK, GW), F32))
    return pl.pallas_call(
        functools.partial(_delta_kernel, seq_len=seq_len, n_sub=n_sub, latent=latent),
        grid=(n_seq,),
        in_specs=in_specs,
        out_specs=out_specs,
        out_shape=out_shape,
        scratch_shapes=[pltpu.VMEM((CONV_CH // LANES, seq_len + n_sub * CONV_PAD, LANES), F32),
                        pltpu.VMEM((CONV_CH // LANES, seq_len + n_sub * CONV_PAD, LANES), F32),
                        pltpu.VMEM((seq_len, C_QK_WIDTH), F32),
                        pltpu.VMEM((seq_len, C_QK_WIDTH), F32),
                        pltpu.VMEM((seq_len, C_WIDTH), F32),
                        pltpu.VMEM((N_DIR, seq_len, C_WIDTH), F32),
                        pltpu.VMEM((N_DIR, seq_len, C_WIDTH), F32),
                        pltpu.VMEM((N_DIR, seq_len, C_WIDTH), F32),
                        pltpu.VMEM((N_DIR, seq_len, C_WIDTH), F32),
                        pltpu.VMEM((nc, N_SYS, CHUNK, GW), F32),
                        pltpu.VMEM((nc, N_SYS, 2 * CHUNK, GW), BF16),
                        pltpu.VMEM((nc, N_SYS, 2 * CHUNK, GW), BF16),
                        pltpu.VMEM((nc, N_SYS, 1, GW), F32)],
        compiler_params=pltpu.CompilerParams(vmem_limit_bytes=VMEM_LIMIT),
        name="delta_latent" if latent else "delta_ctx",
    )(*args)


def _rope_tables():
    rows = DEC_SEQ // GRID_W
    row = jnp.repeat(jnp.arange(rows, dtype=F32), GRID_W)
    col = jnp.tile(jnp.arange(GRID_W, dtype=F32), rows)
    n_freq = HEAD_DIM // 4
    inv_freq = ROPE_THETA ** (-jnp.arange(n_freq, dtype=F32) / n_freq)
    ang_r = row[:, None] * inv_freq
    ang_c = col[:, None] * inv_freq
    ang = jnp.concatenate([ang_r, ang_r, ang_c, ang_c], axis=-1)
    cos, sin = jnp.cos(ang), jnp.sin(ang)
    first = (jnp.arange(HEAD_DIM) % 32) < 16
    sin_a = jnp.where(first, -sin, 0.0)
    sin_b = jnp.where(first, 0.0, sin)
    reps = LANES // HEAD_DIM
    return tuple(jnp.tile(t, (1, reps)) for t in (cos, sin_a, sin_b))


def _block_diag_state(s):
    eye = jnp.eye(B_HEADS, dtype=s.dtype)
    out = jnp.einsum('...hkv,hg->...hkgv', s, eye)
    return out.reshape(s.shape[:-3] + (B_QK_WIDTH, B_WIDTH))


def _pair_state(s):
    lead = s.shape[:-4]
    s = s.reshape(lead + (N_DIR, C_PAIRS, GROUP_HEADS, C_DK, C_DV))
    s = jnp.moveaxis(s, -3, -2)
    return s.reshape(lead + (N_SYS, C_DK, GW))


def _unpair_state(s):
    lead = s.shape[:-3]
    s = s.reshape(lead + (N_DIR, C_PAIRS, C_DK, GROUP_HEADS, C_DV))
    s = jnp.moveaxis(s, -2, -3)
    return s.reshape(lead + (N_DIR, C_HEADS, C_DK, C_DV))


def _diag_blocks(s_bd):
    s5 = s_bd.reshape(s_bd.shape[:-2] + (B_HEADS, B_DK, B_HEADS, B_DV))
    return jnp.stack([s5[..., h, :, h, :] for h in range(B_HEADS)], axis=-3)


def kernel(x_prompt, x_sample, cache_k, cache_v, state_gla, state_delta, c, c_ctx, w_mod, b_mod, norm_gains, w_in, qk_gain, w_gla_gate, b_gla_gate, gla_norm, conv_w, a_log, dt_bias, delta_norm, w_out, w_gate, w_up, w_down):
    xc = x_prompt.reshape(N_CTX_ROWS, D_MODEL)
    xl = x_sample.reshape(N_LAT_ROWS, D_MODEL)
    cond = jnp.concatenate([c_ctx[None, :], c, jnp.zeros((N_COND - 1 - DEC_BATCH, D_MODEL), F32)], axis=0)
    w_out_b = w_out.astype(BF16)
    w_gate_b = w_gate.astype(BF16)
    w_up_b = w_up.astype(BF16)
    w_down_b = w_down.astype(BF16)
    qk_gain2 = jnp.tile(qk_gain, (1, 1, LANES // HEAD_DIM))
    wg_p = jnp.zeros((DEPTH, LANES, N_DIR * B_QK_WIDTH), F32)
    for d in range(N_DIR):
        wg_p = wg_p.at[:, d * GATE_RANK:(d + 1) * GATE_RANK, d * B_QK_WIDTH:(d + 1) * B_QK_WIDTH].set(w_gla_gate[:, d])
    bg_p = b_gla_gate.reshape(DEPTH, 1, N_DIR * B_QK_WIDTH)
    gn_p = jnp.tile(gla_norm, (1, B_HEADS)).reshape(DEPTH, 1, B_WIDTH)
    dn_p = jnp.tile(delta_norm, (1, C_HEADS)).reshape(DEPTH, 1, C_WIDTH)
    al_p = jnp.zeros((DEPTH, 1, LANES), F32).at[:, 0, DECAY_LANE:DECAY_LANE + N_DIR * C_HEADS].set(
        a_log.reshape(DEPTH, N_DIR * C_HEADS))
    dtb_p = jnp.zeros((DEPTH, 1, LANES), F32).at[:, 0, DECAY_LANE:DECAY_LANE + N_DIR * C_HEADS].set(
        dt_bias.reshape(DEPTH, N_DIR * C_HEADS))
    cache_k2 = cache_k.reshape(DEC_BATCH, DEPTH, PAST_LEN, A_KV_WIDTH)
    cache_v2 = cache_v.reshape(DEC_BATCH, DEPTH, PAST_LEN, A_KV_WIDTH)
    s0_gla_bd = _block_diag_state(state_gla.astype(F32))
    s0_delta = _pair_state(state_delta.astype(F32))
    rope = _rope_tables()

    mods = _modulation(cond, w_mod, b_mod).reshape(DEPTH * N_COND, 1, 6 * D_MODEL)

    new_k, new_v, new_gla, new_delta = [], [], [], []
    for l in range(DEPTH):
        za, zb, zc = _in_projection(xc, xl, mods, norm_gains, w_in, l)
        oa_c, k_l, v_l = _attention(za, qk_gain2, l, False)
        (oa_l,) = _attention(za, qk_gain2, l, True, cache_k2, cache_v2, rope)
        ob_c, sg_l = _gla(zb, wg_p, bg_p, gn_p, l, False)
        (ob_l,) = _gla(zb, wg_p, bg_p, gn_p, l, True, s0_gla_bd)
        oc_c, sd_l = _delta(zc, zb, conv_w, al_p, dtb_p, dn_p, l, False)
        (oc_l,) = _delta(zc, zb, conv_w, al_p, dtb_p, dn_p, l, True, s0_delta)
        xc, xl = _out_ffn(xc, xl, (oa_c, oa_l, ob_c, ob_l, oc_c, oc_l), mods, norm_gains,
                          w_out_b, w_gate_b, w_up_b, w_down_b, l)
        new_k.append(k_l)
        new_v.append(v_l)
        new_gla.append(sg_l)
        new_delta.append(sd_l)

    out_dtype = x_prompt.dtype
    y_prompt = xc.reshape(BATCH, SEQ, D_MODEL)
    y_sample = xl.reshape(DEC_BATCH, DEC_SEQ, D_MODEL)
    cache_shape = (BATCH, DEPTH, SEQ, A_KV_HEADS, HEAD_DIM)
    return (y_prompt, y_sample,
            jnp.stack(new_k, axis=1).reshape(cache_shape), jnp.stack(new_v, axis=1).reshape(cache_shape),
            _diag_blocks(jnp.stack(new_gla, axis=1)).astype(out_dtype),
            _unpair_state(jnp.stack(new_delta, axis=1)).astype(out_dtype))
```

```python
import functools

import jax
import jax.numpy as jnp
from jax import lax
from jax.experimental import pallas as pl
from jax.experimental.pallas import tpu as pltpu

F32 = jnp.float32
BF16 = jnp.bfloat16

D_MODEL = 1024
BATCH = 16
SEQ = 256
DEPTH = 4
DEC_BATCH = 4
DEC_SEQ = 1024
PAST_LEN = 512
GRID_W = 64
HEAD_DIM = 64
A_HEADS = 8
A_KV_HEADS = 2
A_REP = A_HEADS // A_KV_HEADS
A_WIDTH = A_HEADS * HEAD_DIM
A_KV_WIDTH = A_KV_HEADS * HEAD_DIM
ROPE_THETA = 10000.0
B_HEADS = 4
B_DK = 32
B_DV = 64
B_QK_WIDTH = B_HEADS * B_DK
B_WIDTH = B_HEADS * B_DV
GATE_RANK = 16
GATE_TAU = 16.0
C_HEADS = 4
C_DK = 64
C_DV = 64
C_QK_WIDTH = C_HEADS * C_DK
C_WIDTH = C_HEADS * C_DV
CONV_WIDTH = 5
CONV_CH = 2 * C_QK_WIDTH + C_WIDTH
CHUNK = 64
N_DIR = 2
MIX_WIDTH = A_WIDTH + B_WIDTH + C_WIDTH
D_FF = -(-(8 * D_MODEL) // (3 * 256)) * 256
EPS = 1e-6

LANES = 128
N_CTX_ROWS = BATCH * SEQ
N_LAT_ROWS = DEC_BATCH * DEC_SEQ
N_ROWS = N_CTX_ROWS + N_LAT_ROWS
N_COND = 8

ZA_W = A_WIDTH + 2 * A_KV_WIDTH
ZB_W = 2 * B_QK_WIDTH + 2 * B_WIDTH + LANES
ZC_W = 2 * C_QK_WIDTH + 2 * C_WIDTH
PROJ_PAD = ZA_W + ZB_W + ZC_W
SMALL_OFF = 2 * B_QK_WIDTH + 2 * B_WIDTH
NAT_SMALL = A_WIDTH + 2 * A_KV_WIDTH + 2 * B_QK_WIDTH + 2 * B_WIDTH
NAT_C = NAT_SMALL + N_DIR * GATE_RANK
NAT_BC = NAT_C + ZC_W
BETA_LANE = N_DIR * GATE_RANK
DECAY_LANE = BETA_LANE + N_DIR * C_HEADS

VMEM_LIMIT = 56 * 1024 * 1024


def _split(x):
    hi = x.astype(BF16)
    lo = (x - hi.astype(F32)).astype(BF16)
    return hi, lo


def _bdot(a, b):
    return jnp.dot(a.astype(BF16), b.astype(BF16), preferred_element_type=F32)


def _bdot_tn(a, b):
    return lax.dot_general(a.astype(BF16), b.astype(BF16), (((0,), (0,)), ((), ())),
                           preferred_element_type=F32)


def _dot3(a, b):
    ah, al = _split(a)
    bh, bl = _split(b)
    return (jnp.dot(ah, bh, preferred_element_type=F32)
            + jnp.dot(ah, bl, preferred_element_type=F32)
            + jnp.dot(al, bh, preferred_element_type=F32))


def _dot_lhs_exact(m_bf16, x):
    xh, xl = _split(x)
    return (jnp.dot(m_bf16, xh, preferred_element_type=F32)
            + jnp.dot(m_bf16, xl, preferred_element_type=F32))


def _dot_rhs_exact(x, m_bf16):
    xh, xl = _split(x)
    return (jnp.dot(xh, m_bf16, preferred_element_type=F32)
            + jnp.dot(xl, m_bf16, preferred_element_type=F32))


def _group_sumsq(x, ones_g):
    return jnp.dot((x * x).astype(BF16), ones_g, preferred_element_type=F32)


def _group_ones(width, group):
    r = lax.broadcasted_iota(jnp.int32, (width, width), 0) // group
    c = lax.broadcasted_iota(jnp.int32, (width, width), 1) // group
    return (r == c).astype(BF16)


def _sigmoid(x):
    return 1.0 / (1.0 + jnp.exp(-x))


def _silu(x):
    return x * _sigmoid(x)


def _softplus(x):
    return jnp.maximum(x, 0.0) + jnp.log1p(jnp.exp(-jnp.abs(x)))


def _rms(x, g):
    return x * lax.rsqrt(jnp.mean(x * x, axis=-1, keepdims=True) + EPS) * g


def _chunk_tri(t, reverse):
    r = lax.broadcasted_iota(jnp.int32, (t, t), 0)
    c = lax.broadcasted_iota(jnp.int32, (t, t), 1)
    same = (r // CHUNK) == (c // CHUNK)
    tri = (c >= r) if reverse else (c <= r)
    return (same & tri).astype(BF16)


def _mod_row(i, tm):
    start = i * tm
    return jnp.where(start < N_CTX_ROWS, 0, 1 + (start - N_CTX_ROWS) // DEC_SEQ)


def _mod_kernel(cond_ref, w_ref, b_ref, o_ref):
    c = cond_ref[...]
    o_ref[0] = _bdot(_silu(c), w_ref[0]) + b_ref[0]


def _modulation(cond, w_mod, b_mod):
    tn = 1536
    return pl.pallas_call(
        _mod_kernel,
        grid=(DEPTH, 6 * D_MODEL // tn),
        in_specs=[pl.BlockSpec((N_COND, D_MODEL), lambda l, j: (0, 0)),
                  pl.BlockSpec((1, D_MODEL, tn), lambda l, j: (l, 0, j)),
                  pl.BlockSpec((1, 1, tn), lambda l, j: (l, 0, j))],
        out_specs=pl.BlockSpec((1, N_COND, tn), lambda l, j: (l, 0, j)),
        out_shape=jax.ShapeDtypeStruct((DEPTH, N_COND, 6 * D_MODEL), F32),
        compiler_params=pltpu.CompilerParams(vmem_limit_bytes=VMEM_LIMIT),
        name="adaln_mod",
    )(cond, w_mod, b_mod.reshape(DEPTH, 1, 6 * D_MODEL))


DENSE_TM = 512
N_CTX_BLOCKS = N_CTX_ROWS // DENSE_TM


def _ctx_map(i):
    return (jnp.minimum(i, N_CTX_BLOCKS - 1), 0)


def _lat_map(i):
    return (jnp.maximum(i - N_CTX_BLOCKS, 0), 0)


def _inproj_kernel(xc_ref, xl_ref, mod_ref, ng_ref, w_ref, za_ref, zb_ref, zc_ref, wp_s):
    @pl.when(pl.program_id(0) == 0)
    def _():
        n_gate = N_DIR * GATE_RANK
        n_bc = 2 * N_DIR * C_HEADS
        wp_s[:, 0:NAT_SMALL] = w_ref[0, :, 0:NAT_SMALL].astype(BF16)
        wp_s[:, NAT_SMALL:NAT_SMALL + n_gate] = w_ref[0, :, NAT_SMALL:NAT_C].astype(BF16)
        wp_s[:, NAT_SMALL + n_gate:NAT_SMALL + n_gate + n_bc] = w_ref[0, :, NAT_BC:NAT_BC + n_bc].astype(BF16)
        wp_s[:, NAT_SMALL + n_gate + n_bc:NAT_SMALL + LANES] = jnp.zeros(
            (D_MODEL, LANES - n_gate - n_bc), BF16)
        wp_s[:, NAT_SMALL + LANES:PROJ_PAD] = w_ref[0, :, NAT_C:NAT_BC].astype(BF16)

    x = jnp.where(pl.program_id(0) < N_CTX_BLOCKS, xc_ref[...], xl_ref[...])
    mod = mod_ref[0]
    shift = mod[:, 0:D_MODEL]
    scale = mod[:, D_MODEL:2 * D_MODEL]
    h = _rms(x, ng_ref[0, 0:1, :]) * (1.0 + scale) + shift
    hb = h.astype(BF16)
    za_ref[...] = jnp.dot(hb, wp_s[:, 0:ZA_W], preferred_element_type=F32)
    zb_ref[...] = jnp.dot(hb, wp_s[:, ZA_W:ZA_W + ZB_W], preferred_element_type=F32)
    zc_ref[...] = jnp.dot(hb, wp_s[:, ZA_W + ZB_W:PROJ_PAD], preferred_element_type=F32)


def _in_projection(xc, xl, mods, norm_gains, w_in, layer):
    tm = DENSE_TM
    proj_width = w_in.shape[-1]
    return pl.pallas_call(
        _inproj_kernel,
        grid=(N_ROWS // tm,),
        in_specs=[pl.BlockSpec((tm, D_MODEL), _ctx_map),
                  pl.BlockSpec((tm, D_MODEL), _lat_map),
                  pl.BlockSpec((1, 1, 6 * D_MODEL), lambda i: (layer * N_COND + _mod_row(i, tm), 0, 0)),
                  pl.BlockSpec((1, 4, D_MODEL), lambda i: (layer, 0, 0)),
                  pl.BlockSpec((1, D_MODEL, proj_width), lambda i: (layer, 0, 0),
                               pipeline_mode=pl.Buffered(1))],
        out_specs=[pl.BlockSpec((tm, ZA_W), lambda i: (i, 0)),
                   pl.BlockSpec((tm, ZB_W), lambda i: (i, 0)),
                   pl.BlockSpec((tm, ZC_W), lambda i: (i, 0))],
        out_shape=[jax.ShapeDtypeStruct((N_ROWS, ZA_W), F32),
                   jax.ShapeDtypeStruct((N_ROWS, ZB_W), F32),
                   jax.ShapeDtypeStruct((N_ROWS, ZC_W), F32)],
        scratch_shapes=[pltpu.VMEM((D_MODEL, PROJ_PAD), BF16)],
        compiler_params=pltpu.CompilerParams(
            dimension_semantics=("arbitrary",), vmem_limit_bytes=VMEM_LIMIT),
        name="in_proj",
    )(xc, xl, mods, norm_gains, w_in)


FF_TILE = 256


def _ffn_kernel(xc_ref, xl_ref, oac_ref, oal_ref, obc_ref, obl_ref, occ_ref, ocl_ref, mod_ref, ng_ref,
                wo_ref, wg_ref, wu_ref, wd_ref, yc_ref, yl_ref):
    is_ctx = pl.program_id(0) < N_CTX_BLOCKS
    x = jnp.where(is_ctx, xc_ref[...], xl_ref[...])
    mod = mod_ref[0]
    gate_m = mod[:, 2 * D_MODEL:3 * D_MODEL]
    shift_f = mod[:, 3 * D_MODEL:4 * D_MODEL]
    scale_f = mod[:, 4 * D_MODEL:5 * D_MODEL]
    gate_f = mod[:, 5 * D_MODEL:6 * D_MODEL]
    mix_in = jnp.concatenate([jnp.where(is_ctx, oac_ref[...], oal_ref[...]),
                              jnp.where(is_ctx, obc_ref[...], obl_ref[...]),
                              jnp.where(is_ctx, occ_ref[...], ocl_ref[...])], axis=-1)
    mix = _bdot(mix_in, wo_ref[0])
    x1 = x + gate_m * _rms(mix, ng_ref[0, 1:2, :])
    h = (_rms(x1, ng_ref[0, 2:3, :]) * (1.0 + scale_f) + shift_f).astype(BF16)
    f = jnp.zeros(x.shape, F32)
    for j in range(D_FF // FF_TILE):
        cols = slice(j * FF_TILE, (j + 1) * FF_TILE)
        g = jnp.dot(h, wg_ref[0, :, cols], preferred_element_type=F32)
        u = jnp.dot(h, wu_ref[0, :, cols], preferred_element_type=F32)
        f = f + _bdot(_silu(g) * u, wd_ref[0, cols, :])
    y = x1 + gate_f * _rms(f, ng_ref[0, 3:4, :])

    @pl.when(is_ctx)
    def _():
        yc_ref[...] = y

    @pl.when(jnp.logical_not(is_ctx))
    def _():
        yl_ref[...] = y


def _out_ffn(xc, xl, mixer_outs, mods, norm_gains, w_out_b, w_gate_b, w_up_b, w_down_b, layer):
    tm = DENSE_TM
    resident = dict(pipeline_mode=pl.Buffered(1))
    pair_specs = []
    for width in (D_MODEL, A_WIDTH, B_WIDTH, C_WIDTH):
        pair_specs += [pl.BlockSpec((tm, width), _ctx_map), pl.BlockSpec((tm, width), _lat_map)]
    return pl.pallas_call(
        _ffn_kernel,
        grid=(N_ROWS // tm,),
        in_specs=pair_specs + [
                  pl.BlockSpec((1, 1, 6 * D_MODEL), lambda i: (layer * N_COND + _mod_row(i, tm), 0, 0)),
                  pl.BlockSpec((1, 4, D_MODEL), lambda i: (layer, 0, 0)),
                  pl.BlockSpec((1, MIX_WIDTH, D_MODEL), lambda i: (layer, 0, 0), **resident),
                  pl.BlockSpec((1, D_MODEL, D_FF), lambda i: (layer, 0, 0), **resident),
                  pl.BlockSpec((1, D_MODEL, D_FF), lambda i: (layer, 0, 0), **resident),
                  pl.BlockSpec((1, D_FF, D_MODEL), lambda i: (layer, 0, 0), **resident)],
        out_specs=[pl.BlockSpec((tm, D_MODEL), _ctx_map), pl.BlockSpec((tm, D_MODEL), _lat_map)],
        out_shape=[jax.ShapeDtypeStruct((N_CTX_ROWS, D_MODEL), F32),
                   jax.ShapeDtypeStruct((N_LAT_ROWS, D_MODEL), F32)],
        compiler_params=pltpu.CompilerParams(
            dimension_semantics=("arbitrary",), vmem_limit_bytes=VMEM_LIMIT),
        name="out_ffn",
    )(xc, xl, *mixer_outs, mods, norm_gains, w_out_b, w_gate_b, w_up_b, w_down_b)


ATT_TQ = 512
ATT_LOOKAHEAD = 1


def _head_norm(x, gain_row, ones_g):
    ss = _group_sumsq(x, ones_g)
    return x * lax.rsqrt(ss * (1.0 / HEAD_DIM) + EPS) * gain_row


def _rope(x, cos, sin_a, sin_b):
    return x * cos + pltpu.roll(x, LANES - 16, 1) * sin_a + pltpu.roll(x, 16, 1) * sin_b


def _attn_kernel(*refs, seq_len, tq, latent):
    if latent:
        (q_ref, kv_ref, ck_ref, cv_ref, qkg_ref, cos_ref, sa_ref, sb_ref, o_ref, k_s, v_s) = refs
    else:
        (q_ref, kv_ref, qkg_ref, o_ref, kn_ref, vn_ref, k_s, v_s) = refs
    j = pl.program_id(1)
    ones_g = _group_ones(LANES, HEAD_DIM)
    gq = qkg_ref[0, 0:1, :]
    gk = qkg_ref[0, 1:2, :]
    past = PAST_LEN if latent else 0

    @pl.when(j == 0)
    def _():
        ka = kv_ref[:, 0:A_KV_WIDTH]
        va = kv_ref[:, A_KV_WIDTH:2 * A_KV_WIDTH]
        kn = _head_norm(ka, gk, ones_g)
        if latent:
            kr = _rope(kn, cos_ref[...], sa_ref[...], sb_ref[...])
            ck = ck_ref[...]
            cv = cv_ref[...]
            for g in range(A_KV_HEADS):
                cols = slice(g * HEAD_DIM, (g + 1) * HEAD_DIM)
                k_s[g, 0:past, :] = ck[:, cols].astype(BF16)
                v_s[g, 0:past, :] = cv[:, cols].astype(BF16)
                k_s[g, past:past + seq_len, :] = kr[:, cols].astype(BF16)
                v_s[g, past:past + seq_len, :] = va[:, cols].astype(BF16)
        else:
            kn_ref[...] = kn
            vn_ref[...] = va
            for g in range(A_KV_HEADS):
                cols = slice(g * HEAD_DIM, (g + 1) * HEAD_DIM)
                k_s[g] = kn[:, cols].astype(BF16)
                v_s[g] = va[:, cols].astype(BF16)

    if latent:
        rows = pl.ds(pl.multiple_of(j * tq, tq), tq)
        cos = cos_ref[rows, :]
        sin_a = sa_ref[rows, :]
        sin_b = sb_ref[rows, :]
    q_heads = []
    for t in range(A_WIDTH // LANES):
        qt = _head_norm(q_ref[:, t * LANES:(t + 1) * LANES], gq, ones_g)
        if latent:
            qt = _rope(qt, cos, sin_a, sin_b)
        qt = (qt * (HEAD_DIM ** -0.5)).astype(BF16)
        q_heads += [qt[:, hh * HEAD_DIM:(hh + 1) * HEAD_DIM] for hh in range(LANES // HEAD_DIM)]

    def scores(h):
        return lax.dot_general(q_heads[h], k_s[h // A_REP], (((1,), (1,)), ((), ())),
                               preferred_element_type=F32)

    outs = []
    pending = [scores(h) for h in range(ATT_LOOKAHEAD)]
    for h in range(A_HEADS):
        s = pending.pop(0)
        if h + ATT_LOOKAHEAD < A_HEADS:
            pending.append(scores(h + ATT_LOOKAHEAD))
        m = jnp.max(s, axis=-1, keepdims=True)
        p = jnp.exp(s - m)
        l = jnp.sum(p, axis=-1, keepdims=True)
        o = jnp.dot(p.astype(BF16), v_s[h // A_REP], preferred_element_type=F32)
        outs.append(o / l)
    o_ref[...] = jnp.concatenate(outs, axis=-1)


def _attention(za, qk_gain2, layer, latent, cache_k=None, cache_v=None, rope=None):
    seq_len = DEC_SEQ if latent else SEQ
    n_seq = DEC_BATCH if latent else BATCH
    row0 = N_CTX_ROWS // seq_len if latent else 0
    tq = min(ATT_TQ, seq_len)
    nq = seq_len // tq
    row0q = N_CTX_ROWS // tq if latent else 0
    s_len = seq_len + (PAST_LEN if latent else 0)
    in_specs = [pl.BlockSpec((tq, A_WIDTH), lambda s, j: (row0q + s * nq + j, 0)),
                pl.BlockSpec((seq_len, 2 * A_KV_WIDTH), lambda s, j: (row0 + s, 2))]
    args = [za, za]
    if latent:
        in_specs += [pl.BlockSpec((None, None, PAST_LEN, A_KV_WIDTH), lambda s, j: (s, layer, 0, 0)),
                     pl.BlockSpec((None, None, PAST_LEN, A_KV_WIDTH), lambda s, j: (s, layer, 0, 0))]
        args += [cache_k, cache_v]
    in_specs.append(pl.BlockSpec((1, 2, LANES), lambda s, j: (layer, 0, 0)))
    args.append(qk_gain2)
    if latent:
        in_specs += [pl.BlockSpec((seq_len, LANES), lambda s, j: (0, 0))] * 3
        args += list(rope)
    out_specs = [pl.BlockSpec((tq, A_WIDTH), lambda s, j: (s * nq + j, 0))]
    out_shape = [jax.ShapeDtypeStruct((n_seq * seq_len, A_WIDTH), F32)]
    if not latent:
        out_specs += [pl.BlockSpec((None, seq_len, A_KV_WIDTH), lambda s, j: (s, 0, 0))] * 2
        out_shape += [jax.ShapeDtypeStruct((n_seq, seq_len, A_KV_WIDTH), F32)] * 2
    return pl.pallas_call(
        functools.partial(_attn_kernel, seq_len=seq_len, tq=tq, latent=latent),
        grid=(n_seq, nq),
        in_specs=in_specs,
        out_specs=out_specs,
        out_shape=out_shape,
        scratch_shapes=[pltpu.VMEM((A_KV_HEADS, s_len, HEAD_DIM), BF16),
                        pltpu.VMEM((A_KV_HEADS, s_len, HEAD_DIM), BF16)],
        compiler_params=pltpu.CompilerParams(
            dimension_semantics=("arbitrary", "arbitrary"), vmem_limit_bytes=VMEM_LIMIT),
        name="attn_latent" if latent else "attn_ctx",
    )(*args)


def _gla_kernel(*refs, seq_len, n_sub, latent):
    if latent:
        zb_ref, wg_ref, bg_ref, gn_ref, s0_ref, ob_ref = refs
    else:
        zb_ref, wg_ref, bg_ref, gn_ref, ob_ref, sfin_ref = refs
    nc = seq_len // CHUNK
    q = zb_ref[:, 0:B_QK_WIDTH] * (B_DK ** -0.5)
    k = zb_ref[:, B_QK_WIDTH:2 * B_QK_WIDTH]
    v = zb_ref[:, 2 * B_QK_WIDTH:2 * B_QK_WIDTH + B_WIDTH]
    small = zb_ref[:, SMALL_OFF:SMALL_OFF + LANES]
    pre = _dot3(small, wg_ref[0]) + bg_ref[0]
    glog = (jnp.minimum(pre, 0.0) - jnp.log1p(jnp.exp(-jnp.abs(pre)))) * (1.0 / GATE_TAU)

    kr = lax.broadcasted_iota(jnp.int32, (B_QK_WIDTH, B_WIDTH), 0) // B_DK
    vc = lax.broadcasted_iota(jnp.int32, (B_QK_WIDTH, B_WIDTH), 1) // B_DV
    bd_kv = kr == vc
    sr = lax.broadcasted_iota(jnp.int32, (B_WIDTH, B_WIDTH), 0) // CHUNK
    sc = lax.broadcasted_iota(jnp.int32, (B_WIDTH, B_WIDTH), 1) // B_DV
    bd_sv = sr == sc
    tt = lax.broadcasted_iota(jnp.int32, (CHUNK, B_WIDTH), 0)
    ss = lax.broadcasted_iota(jnp.int32, (CHUNK, B_WIDTH), 1) % CHUNK

    q3 = q.reshape(nc, CHUNK, B_QK_WIDTH)
    k3 = k.reshape(nc, CHUNK, B_QK_WIDTH)
    intra, upds, decays, q_sts = [], [], [], []
    for d in range(N_DIR):
        g = glog[:, d * B_QK_WIDTH:(d + 1) * B_QK_WIDTH]
        tri = _chunk_tri(CUM_ROWS, d == 1)
        b = jnp.concatenate([_dot_lhs_exact(tri, g[r * CUM_ROWS:(r + 1) * CUM_ROWS, :])
                             for r in range(seq_len // CUM_ROWS)], axis=0)
        b3 = b.reshape(nc, CHUNK, B_QK_WIDTH)
        mid = b3[:, CHUNK // 2:CHUNK // 2 + 1, :]
        last = b3[:, 0:1, :] if d == 1 else b3[:, CHUNK - 1:CHUNK, :]
        q_in = q3 * jnp.exp(b3 - mid)
        k_in = k3 * jnp.exp(mid - b3)
        q_sts.append(q3 * jnp.exp(b3))
        k_st = k3 * jnp.exp(last - b3)
        decay_t = jnp.exp(jnp.broadcast_to(last, (nc, 8, B_QK_WIDTH)).reshape(nc * 8, B_QK_WIDTH)).T
        causal = (ss >= tt) if d == 1 else (ss <= tt)
        v_chunks = [v[c * CHUNK:(c + 1) * CHUNK, :] for c in range(nc)]
        atts = [_bdot(q_in[c], jnp.where(bd_kv, jnp.concatenate([k_in[c].T] * B_HEADS, axis=1), 0.0))
                for c in range(nc)]
        upds.append([jnp.where(bd_kv, _bdot_tn(k_st[c], v_chunks[c]), 0.0) for c in range(nc)])
        intra.append([_bdot(jnp.where(causal, atts[c], 0.0),
                            jnp.where(bd_sv, jnp.concatenate([v_chunks[c]] * B_HEADS, axis=0), 0.0))
                      for c in range(nc)])
        decays.append([decay_t[:, 8 * c:8 * c + 1] for c in range(nc)])
    o_dirs = []
    per_seq = nc // n_sub
    for d in range(N_DIR):
        o_chunks = [None] * nc
        for sq in range(n_sub):
            state = s0_ref[d] if latent else jnp.zeros((B_QK_WIDTH, B_WIDTH), F32)
            chunks = range(sq * per_seq, (sq + 1) * per_seq)
            for c in (reversed(chunks) if d == 1 else chunks):
                o_chunks[c] = intra[d][c] + _bdot(q_sts[d][c], state)
                state = decays[d][c] * state + upds[d][c]
            if not latent:
                sfin_ref[sq, d] = state
        o_dirs.append(jnp.concatenate(o_chunks, axis=0))
    o = o_dirs[0] + o_dirs[1]
    ms = _group_sumsq(o, _group_ones(B_WIDTH, B_DV)) * (1.0 / B_DV)
    r = zb_ref[:, 2 * B_QK_WIDTH + B_WIDTH:2 * B_QK_WIDTH + 2 * B_WIDTH]
    ob_ref[...] = o * lax.rsqrt(ms + EPS) * gn_ref[0] * _silu(r)


GLA_CTX_SEQS = 4


def _gla(zb, wg_p, bg_p, gn_p, layer, latent, s0_bd=None):
    n_sub = 1 if latent else GLA_CTX_SEQS
    seq_len = (DEC_SEQ if latent else SEQ) * n_sub
    n_seq = (DEC_BATCH if latent else BATCH) // n_sub
    row0 = N_CTX_ROWS // seq_len if latent else 0
    in_specs = [pl.BlockSpec((seq_len, ZB_W), lambda s: (row0 + s, 0)),
                pl.BlockSpec((1, LANES, N_DIR * B_QK_WIDTH), lambda s: (layer, 0, 0)),
                pl.BlockSpec((1, 1, N_DIR * B_QK_WIDTH), lambda s: (layer, 0, 0)),
                pl.BlockSpec((1, 1, B_WIDTH), lambda s: (layer, 0, 0))]
    args = [zb, wg_p, bg_p, gn_p]
    out_specs = [pl.BlockSpec((seq_len, B_WIDTH), lambda s: (s, 0))]
    out_shape = [jax.ShapeDtypeStruct((n_seq * seq_len, B_WIDTH), F32)]
    if latent:
        in_specs.append(pl.BlockSpec((None, None, N_DIR, B_QK_WIDTH, B_WIDTH), lambda s: (s, layer, 0, 0, 0)))
        args.append(s0_bd)
    else:
        out_specs.append(pl.BlockSpec((n_sub, N_DIR, B_QK_WIDTH, B_WIDTH), lambda s: (s, 0, 0, 0)))
        out_shape.append(jax.ShapeDtypeStruct((n_seq * n_sub, N_DIR, B_QK_WIDTH, B_WIDTH), F32))
    return pl.pallas_call(
        functools.partial(_gla_kernel, seq_len=seq_len, n_sub=n_sub, latent=latent),
        grid=(n_seq,),
        in_specs=in_specs,
        out_specs=out_specs,
        out_shape=out_shape,
        compiler_params=pltpu.CompilerParams(vmem_limit_bytes=VMEM_LIMIT),
        name="gla_latent" if latent else "gla_ctx",
    )(*args)


TRI_BASE = 8
GROUP_HEADS = 2
GW = GROUP_HEADS * C_DK
C_GROUPS = C_HEADS // GROUP_HEADS
N_SYS = N_DIR * C_GROUPS
PREP_CHUNKS = 4
CUM_ROWS = 256
CONV_PAD = 8


def _bd(y, mask):
    return jnp.where(mask, jnp.concatenate([y] * GROUP_HEADS, axis=0), 0.0).astype(BF16)


def _group_tri_inverse(ms, bd_mask, tt, ss):
    pws = [jnp.where((tt // TRI_BASE) == (ss // TRI_BASE), -m, 0.0) for m in ms]
    invs = [(tt == ss).astype(F32) + pw for pw in pws]
    span = 2
    while span < TRI_BASE:
        pws = [_bdot(pw, _bd(pw, bd_mask)) for pw in pws]
        invs = [inv + _bdot(inv, _bd(pw, bd_mask)) for inv, pw in zip(invs, pws)]
        span *= 2
    size = 2 * TRI_BASE
    while size <= CHUNK:
        off = ((tt // size) == (ss // size)) & ((tt // (size // 2)) != (ss // (size // 2)))
        xcs = [_bdot(inv, _bd(jnp.where(off, m, 0.0), bd_mask)) for m, inv in zip(ms, invs)]
        invs = [inv - _bdot(xc, _bd(inv, bd_mask)) for inv, xc in zip(invs, xcs)]
        size *= 2
    return invs


def _delta_kernel(*refs, seq_len, n_sub, latent):
    if latent:
        (zc_ref, sm_ref, cw_ref, al_ref, dtb_ref, dn_ref, s0_ref, oc_ref,
         xp_s, yp_s, q_s, k_s, v_s, bx_s, gx_s, gr_s, o_s, u_s, wq_s, ak_s, eg_s) = refs
    else:
        (zc_ref, sm_ref, cw_ref, al_ref, dtb_ref, dn_ref, oc_ref, sfin_ref,
         xp_s, yp_s, q_s, k_s, v_s, bx_s, gx_s, gr_s, o_s, u_s, wq_s, ak_s, eg_s) = refs
    nc = seq_len // CHUNK
    per_seq = nc // n_sub

    len1 = seq_len // n_sub
    seg = len1 // 8 + 1
    sub = lax.broadcasted_iota(jnp.int32, (8, LANES), 0)
    half = CONV_WIDTH // 2
    for j in range(CONV_CH // LANES):
        cols = slice(j * LANES, (j + 1) * LANES)
        taps = [cw_ref[0, tap:tap + 1, cols] for tap in range(CONV_WIDTH)]
        for sq in range(n_sub):
            base = sq * 8 * seg
            xp_s[j, base:base + len1, :] = zc_ref[sq * len1:(sq + 1) * len1, cols]
            xp_s[j, base + len1:base + 8 * seg, :] = jnp.zeros((8 * seg - len1, LANES), F32)
            xr = [xp_s[j, pl.ds(base + a, 8, stride=seg), :] for a in range(seg)]
            nxt = [jnp.where(sub < 7, pltpu.roll(xr[a], 7, 0), 0.0) for a in range(half)]
            prv = [jnp.where(sub > 0, pltpu.roll(xr[seg - half + a], 1, 0), 0.0) for a in range(half)]
            window = prv + xr + nxt
            for a in range(seg):
                acc = window[a] * taps[0]
                for tap in range(1, CONV_WIDTH):
                    acc = acc + window[a + tap] * taps[tap]
                yp_s[j, pl.ds(base + a, 8, stride=seg), :] = acc
    y = _silu(jnp.concatenate(
        [jnp.concatenate([yp_s[j, sq * 8 * seg:sq * 8 * seg + len1, :] for sq in range(n_sub)], axis=0)
         for j in range(CONV_CH // LANES)], axis=1))
    ones_g = _group_ones(C_QK_WIDTH, C_DK)
    qc = y[:, 0:C_QK_WIDTH]
    kc = y[:, C_QK_WIDTH:2 * C_QK_WIDTH]
    q_s[...] = qc * lax.rsqrt(_group_sumsq(qc, ones_g) + EPS) * (C_DK ** -0.5)
    k_s[...] = kc * lax.rsqrt(_group_sumsq(kc, ones_g) + EPS)
    v_s[...] = y[:, 2 * C_QK_WIDTH:CONV_CH]

    small = sm_ref[...]
    beta = _sigmoid(small)
    glog = -jnp.exp(al_ref[0]) * _softplus(small + dtb_ref[0])
    lane_r = lax.broadcasted_iota(jnp.int32, (LANES, C_WIDTH), 0)
    head_c = lax.broadcasted_iota(jnp.int32, (LANES, C_WIDTH), 1) // C_DV
    cr = lax.broadcasted_iota(jnp.int32, (CUM_ROWS, CUM_ROWS), 0)
    cl = lax.broadcasted_iota(jnp.int32, (CUM_ROWS, CUM_ROWS), 1)
    same_chunk = ((cr // CHUNK) == (cl // CHUNK)).astype(BF16)
    on_diag = (lax.broadcasted_iota(jnp.int32, (CUM_ROWS, C_WIDTH), 0) % CHUNK
               == lax.broadcasted_iota(jnp.int32, (CUM_ROWS, C_WIDTH), 1) % C_DV)
    for d in range(N_DIR):
        bx_s[d] = _dot_rhs_exact(beta, (lane_r == BETA_LANE + d * C_HEADS + head_c).astype(BF16))
    pick_gam = [(lane_r == DECAY_LANE + d * C_HEADS + head_c).astype(BF16) for d in range(N_DIR)]
    tris = [_chunk_tri(CUM_ROWS, d == 1) for d in range(N_DIR)]
    blocks = [(d, slice(blk * CUM_ROWS, (blk + 1) * CUM_ROWS))
              for d in range(N_DIR) for blk in range(seq_len // CUM_ROWS)]
    cums = [_dot_lhs_exact(tris[d], glog[r, :]) for d, r in blocks]
    g_xs = [_dot_rhs_exact(cum, pick_gam[d]) for (d, r), cum in zip(blocks, cums)]
    g_rs = [_dot_lhs_exact(same_chunk, jnp.where(on_diag, g_x, 0.0)) for g_x in g_xs]
    for (d, r), g_x, g_r in zip(blocks, g_xs, g_rs):
        gx_s[d, r, :] = g_x
        gr_s[d, r, :] = g_r

    tt = lax.broadcasted_iota(jnp.int32, (CHUNK, GW), 0)
    ss = lax.broadcasted_iota(jnp.int32, (CHUNK, GW), 1) % CHUNK
    bd1 = (lax.broadcasted_iota(jnp.int32, (GW, GW), 0) // CHUNK
           == lax.broadcasted_iota(jnp.int32, (GW, GW), 1) // CHUNK)
    bd2 = jnp.concatenate([bd1, bd1], axis=1)

    prep = min(PREP_CHUNKS, nc)

    def prepare_chunks(step, carry):
        ids, ms, rhss, qgs, aks, egs = [], [], [], [], [], []
        for cj in range(prep):
            c = step * prep + cj
            rows = pl.ds(pl.multiple_of(c * CHUNK, CHUNK), CHUNK)
            k_t = k_s[rows, :].T
            k_bds = [_bd(k_s[rows, p * GW:(p + 1) * GW], bd1) for p in range(C_GROUPS)]
            k_t_groups = [jnp.concatenate(
                [k_t[p * GW + h * C_DK:p * GW + (h + 1) * C_DK, :] for h in range(GROUP_HEADS)], axis=1)
                for p in range(C_GROUPS)]
            for d in range(N_DIR):
                ahead = (tt - ss) if d == 1 else (ss - tt)
                last = 0 if d == 1 else CHUNK - 1
                for p in range(C_GROUPS):
                    lanes = slice(p * GW, (p + 1) * GW)
                    qp = q_s[rows, lanes]
                    kp = k_s[rows, lanes]
                    bx = bx_s[d, rows, lanes]
                    gx = gx_s[d, rows, lanes]
                    gr = gr_s[d, rows, lanes]
                    dec = jnp.exp(jnp.where(ahead <= 0, gx - gr, -jnp.inf))
                    kb = kp * bx
                    ma = lax.dot_general(jnp.concatenate([kb, qp], axis=0).astype(BF16), k_bds[p],
                                         (((1,), (1,)), ((), ())), preferred_element_type=F32)
                    e_gx = jnp.exp(gx)
                    g_last = gx[last:last + 1, :]
                    ids.append((c, d * C_GROUPS + p))
                    ms.append(jnp.where(ahead < 0, ma[0:CHUNK] * dec, 0.0))
                    rhss.append(jnp.concatenate([v_s[rows, lanes] * bx, kb * e_gx], axis=1))
                    qgs.append(qp * e_gx)
                    aks.append(jnp.concatenate([ma[CHUNK:2 * CHUNK] * dec, k_t_groups[p] * jnp.exp(g_last - gr)],
                                               axis=0).astype(BF16))
                    egs.append(jnp.exp(g_last))
        invs = _group_tri_inverse(ms, bd1, tt, ss)
        sols = [_bdot(inv, _bd(rhs, bd2)) for inv, rhs in zip(invs, rhss)]
        m_sols = []
        for m, sol in zip(ms, sols):
            sh, sl = _split(sol)
            mh, ml = _split(m)
            shb, slb = _bd(sh, bd2), _bd(sl, bd2)
            m_sols.append(jnp.dot(mh, shb, preferred_element_type=F32)
                          + jnp.dot(mh, slb, preferred_element_type=F32)
                          + jnp.dot(ml, shb, preferred_element_type=F32))
        sols = [sol + _bdot(inv, _bd(rhs - sol - m_sol, bd2))
                for inv, rhs, sol, m_sol in zip(invs, rhss, sols, m_sols)]
        for (c, sy), sol, qg, ak, eg in zip(ids, sols, qgs, aks, egs):
            u_s[c, sy] = sol[:, 0:GW]
            wq_s[c, sy] = jnp.concatenate([sol[:, GW:2 * GW], qg], axis=0).astype(BF16)
            ak_s[c, sy] = ak
            eg_s[c, sy] = eg
        return carry

    lax.fori_loop(0, nc // prep, prepare_chunks, 0)

    def scan_chunk(i, state):
        ids = [(sq, d, sq * per_seq + ((per_seq - 1 - i) if d == 1 else i), p)
               for sq in range(n_sub) for d in range(N_DIR) for p in range(C_GROUPS)]
        states = [state[sq * N_SYS + d * C_GROUPS + p] for sq, d, c, p in ids]
        wq_states = [jnp.dot(wq_s[c, d * C_GROUPS + p], _bd(st, bd1), preferred_element_type=F32)
                     for (sq, d, c, p), st in zip(ids, states)]
        v_news = [u_s[c, d * C_GROUPS + p] - wqs[0:CHUNK] for (sq, d, c, p), wqs in zip(ids, wq_states)]
        aks = [jnp.dot(ak_s[c, d * C_GROUPS + p], _bd(v_new, bd1), preferred_element_type=F32)
               for (sq, d, c, p), v_new in zip(ids, v_news)]
        new_state = [eg_s[c, d * C_GROUPS + p] * st + ak[CHUNK:2 * CHUNK]
                     for (sq, d, c, p), st, ak in zip(ids, states, aks)]
        for (sq, d, c, p), wqs, ak in zip(ids, wq_states, aks):
            o_s[d, pl.ds(pl.multiple_of(c * CHUNK, CHUNK), CHUNK), p * GW:(p + 1) * GW] = (
                wqs[CHUNK:2 * CHUNK] + ak[0:CHUNK])
        return jnp.stack(new_state)

    if latent:
        state0 = s0_ref[...]
    else:
        state0 = jnp.zeros((n_sub * N_SYS, C_DK, GW), F32)
    state = lax.fori_loop(0, per_seq, scan_chunk, state0)

    o = o_s[0] + o_s[1]
    ms = _group_sumsq(o, _group_ones(C_WIDTH, C_DV)) * (1.0 / C_DV)
    gate = zc_ref[:, CONV_CH:CONV_CH + C_WIDTH]
    oc_ref[...] = o * lax.rsqrt(ms + EPS) * dn_ref[0] * _silu(gate)
    if not latent:
        sfin_ref[...] = state.reshape(n_sub, N_SYS, C_DK, GW)


DELTA_CTX_SEQS = 4


def _delta(zc, zb, conv_w, al_p, dtb_p, dn_p, layer, latent, state_delta=None):
    n_sub = 1 if latent else DELTA_CTX_SEQS
    seq_len = (DEC_SEQ if latent else SEQ) * n_sub
    n_seq = (DEC_BATCH if latent else BATCH) // n_sub
    row0 = N_CTX_ROWS // seq_len if latent else 0
    nc = seq_len // CHUNK
    in_specs = [pl.BlockSpec((seq_len, ZC_W), lambda s: (row0 + s, 0)),
                pl.BlockSpec((seq_len, LANES), lambda s: (row0 + s, SMALL_OFF // LANES)),
                pl.BlockSpec((1, CONV_WIDTH, CONV_CH), lambda s: (layer, 0, 0)),
                pl.BlockSpec((1, 1, LANES), lambda s: (layer, 0, 0)),
                pl.BlockSpec((1, 1, LANES), lambda s: (layer, 0, 0)),
                pl.BlockSpec((1, 1, C_WIDTH), lambda s: (layer, 0, 0))]
    args = [zc, zb, conv_w, al_p, dtb_p, dn_p]
    out_specs = [pl.BlockSpec((seq_len, C_WIDTH), lambda s: (s, 0))]
    out_shape = [jax.ShapeDtypeStruct((n_seq * seq_len, C_WIDTH), F32)]
    if latent:
        in_specs.append(pl.BlockSpec((None, None, N_SYS, C_DK, GW), lambda s: (s, layer, 0, 0, 0)))
        args.append(state_delta)
    else:
        out_specs.append(pl.BlockSpec((n_sub, N_SYS, C_DK, GW), lambda s: (s, 0, 0, 0)))
        out_shape.append(jax.ShapeDtypeStruct((n_seq * n_sub, N_SYS, C_DK, GW), F32))
    return pl.pallas_call(
        functools.partial(_delta_kernel, seq_len=seq_len, n_sub=n_sub, latent=latent),
        grid=(n_seq,),
        in_specs=in_specs,
        out_specs=out_specs,
        out_shape=out_shape,
        scratch_shapes=[pltpu.VMEM((CONV_CH // LANES, seq_len + n_sub * CONV_PAD, LANES), F32),
                        pltpu.VMEM((CONV_CH // LANES, seq_len + n_sub * CONV_PAD, LANES), F32),
                        pltpu.VMEM((seq_len, C_QK_WIDTH), F32),
                        pltpu.VMEM((seq_len, C_QK_WIDTH), F32),
                        pltpu.VMEM((seq_len, C_WIDTH), F32),
                        pltpu.VMEM((N_DIR, seq_len, C_WIDTH), F32),
                        pltpu.VMEM((N_DIR, seq_len, C_WIDTH), F32),
                        pltpu.VMEM((N_DIR, seq_len, C_WIDTH), F32),
                        pltpu.VMEM((N_DIR, seq_len, C_WIDTH), F32),
                        pltpu.VMEM((nc, N_SYS, CHUNK, GW), F32),
                        pltpu.VMEM((nc, N_SYS, 2 * CHUNK, GW), BF16),
                        pltpu.VMEM((nc, N_SYS, 2 * CHUNK, GW), BF16),
                        pltpu.VMEM((nc, N_SYS, 1, GW), F32)],
        compiler_params=pltpu.CompilerParams(vmem_limit_bytes=VMEM_LIMIT),
        name="delta_latent" if latent else "delta_ctx",
    )(*args)


def _rope_tables():
    rows = DEC_SEQ // GRID_W
    row = jnp.repeat(jnp.arange(rows, dtype=F32), GRID_W)
    col = jnp.tile(jnp.arange(GRID_W, dtype=F32), rows)
    n_freq = HEAD_DIM // 4
    inv_freq = ROPE_THETA ** (-jnp.arange(n_freq, dtype=F32) / n_freq)
    ang_r = row[:, None] * inv_freq
    ang_c = col[:, None] * inv_freq
    ang = jnp.concatenate([ang_r, ang_r, ang_c, ang_c], axis=-1)
    cos, sin = jnp.cos(ang), jnp.sin(ang)
    first = (jnp.arange(HEAD_DIM) % 32) < 16
    sin_a = jnp.where(first, -sin, 0.0)
    sin_b = jnp.where(first, 0.0, sin)
    reps = LANES // HEAD_DIM
    return tuple(jnp.tile(t, (1, reps)) for t in (cos, sin_a, sin_b))


def _block_diag_state(s):
    eye = jnp.eye(B_HEADS, dtype=s.dtype)
    out = jnp.einsum('...hkv,hg->...hkgv', s, eye)
    return out.reshape(s.shape[:-3] + (B_QK_WIDTH, B_WIDTH))


def _group_state(s):
    lead = s.shape[:-4]
    s = s.reshape(lead + (N_DIR, C_GROUPS, GROUP_HEADS, C_DK, C_DV))
    s = jnp.moveaxis(s, -3, -2)
    return s.reshape(lead + (N_SYS, C_DK, GW))


def _ungroup_state(s):
    lead = s.shape[:-3]
    s = s.reshape(lead + (N_DIR, C_GROUPS, C_DK, GROUP_HEADS, C_DV))
    s = jnp.moveaxis(s, -2, -3)
    return s.reshape(lead + (N_DIR, C_HEADS, C_DK, C_DV))


def _diag_blocks(s_bd):
    s5 = s_bd.reshape(s_bd.shape[:-2] + (B_HEADS, B_DK, B_HEADS, B_DV))
    return jnp.stack([s5[..., h, :, h, :] for h in range(B_HEADS)], axis=-3)


def kernel(x_prompt, x_sample, cache_k, cache_v, state_gla, state_delta, c, c_ctx, w_mod, b_mod, norm_gains, w_in, qk_gain, w_gla_gate, b_gla_gate, gla_norm, conv_w, a_log, dt_bias, delta_norm, w_out, w_gate, w_up, w_down):
    xc = x_prompt.reshape(N_CTX_ROWS, D_MODEL)
    xl = x_sample.reshape(N_LAT_ROWS, D_MODEL)
    cond = jnp.concatenate([c_ctx[None, :], c, jnp.zeros((N_COND - 1 - DEC_BATCH, D_MODEL), F32)], axis=0)
    w_out_b = w_out.astype(BF16)
    w_gate_b = w_gate.astype(BF16)
    w_up_b = w_up.astype(BF16)
    w_down_b = w_down.astype(BF16)
    qk_gain2 = jnp.tile(qk_gain, (1, 1, LANES // HEAD_DIM))
    wg_p = jnp.zeros((DEPTH, LANES, N_DIR * B_QK_WIDTH), F32)
    for d in range(N_DIR):
        wg_p = wg_p.at[:, d * GATE_RANK:(d + 1) * GATE_RANK, d * B_QK_WIDTH:(d + 1) * B_QK_WIDTH].set(w_gla_gate[:, d])
    bg_p = b_gla_gate.reshape(DEPTH, 1, N_DIR * B_QK_WIDTH)
    gn_p = jnp.tile(gla_norm, (1, B_HEADS)).reshape(DEPTH, 1, B_WIDTH)
    dn_p = jnp.tile(delta_norm, (1, C_HEADS)).reshape(DEPTH, 1, C_WIDTH)
    al_p = jnp.zeros((DEPTH, 1, LANES), F32).at[:, 0, DECAY_LANE:DECAY_LANE + N_DIR * C_HEADS].set(
        a_log.reshape(DEPTH, N_DIR * C_HEADS))
    dtb_p = jnp.zeros((DEPTH, 1, LANES), F32).at[:, 0, DECAY_LANE:DECAY_LANE + N_DIR * C_HEADS].set(
        dt_bias.reshape(DEPTH, N_DIR * C_HEADS))
    cache_k2 = cache_k.reshape(DEC_BATCH, DEPTH, PAST_LEN, A_KV_WIDTH)
    cache_v2 = cache_v.reshape(DEC_BATCH, DEPTH, PAST_LEN, A_KV_WIDTH)
    s0_gla_bd = _block_diag_state(state_gla.astype(F32))
    s0_delta = _group_state(state_delta.astype(F32))
    rope = _rope_tables()

    mods = _modulation(cond, w_mod, b_mod).reshape(DEPTH * N_COND, 1, 6 * D_MODEL)

    new_k, new_v, new_gla, new_delta = [], [], [], []
    for l in range(DEPTH):
        za, zb, zc = _in_projection(xc, xl, mods, norm_gains, w_in, l)
        oa_c, k_l, v_l = _attention(za, qk_gain2, l, False)
        (oa_l,) = _attention(za, qk_gain2, l, True, cache_k2, cache_v2, rope)
        ob_c, sg_l = _gla(zb, wg_p, bg_p, gn_p, l, False)
        (ob_l,) = _gla(zb, wg_p, bg_p, gn_p, l, True, s0_gla_bd)
        oc_c, sd_l = _delta(zc, zb, conv_w, al_p, dtb_p, dn_p, l, False)
        (oc_l,) = _delta(zc, zb, conv_w, al_p, dtb_p, dn_p, l, True, s0_delta)
        xc, xl = _out_ffn(xc, xl, (oa_c, oa_l, ob_c, ob_l, oc_c, oc_l), mods, norm_gains,
                          w_out_b, w_gate_b, w_up_b, w_down_b, l)
        new_k.append(k_l)
        new_v.append(v_l)
        new_gla.append(sg_l)
        new_delta.append(sd_l)

    out_dtype = x_prompt.dtype
    y_prompt = xc.reshape(BATCH, SEQ, D_MODEL)
    y_sample = xl.reshape(DEC_BATCH, DEC_SEQ, D_MODEL)
    cache_shape = (BATCH, DEPTH, SEQ, A_KV_HEADS, HEAD_DIM)
    return (y_prompt, y_sample,
            jnp.stack(new_k, axis=1).reshape(cache_shape), jnp.stack(new_v, axis=1).reshape(cache_shape),
            _diag_blocks(jnp.stack(new_gla, axis=1)).astype(out_dtype),
            _ungroup_state(jnp.stack(new_delta, axis=1)).astype(out_dtype))
```

```python
import functools

import jax
import jax.numpy as jnp
from jax import lax
from jax.experimental import pallas as pl
from jax.experimental.pallas import tpu as pltpu

F32 = jnp.float32
BF16 = jnp.bfloat16

D_MODEL = 1024
BATCH = 16
SEQ = 256
DEPTH = 4
DEC_BATCH = 4
DEC_SEQ = 1024
PAST_LEN = 512
GRID_W = 64
HEAD_DIM = 64
A_HEADS = 8
A_KV_HEADS = 2
A_REP = A_HEADS // A_KV_HEADS
A_WIDTH = A_HEADS * HEAD_DIM
A_KV_WIDTH = A_KV_HEADS * HEAD_DIM
ROPE_THETA = 10000.0
B_HEADS = 4
B_DK = 32
B_DV = 64
B_QK_WIDTH = B_HEADS * B_DK
B_WIDTH = B_HEADS * B_DV
GATE_RANK = 16
GATE_TAU = 16.0
C_HEADS = 4
C_DK = 64
C_DV = 64
C_QK_WIDTH = C_HEADS * C_DK
C_WIDTH = C_HEADS * C_DV
CONV_WIDTH = 5
CONV_CH = 2 * C_QK_WIDTH + C_WIDTH
CHUNK = 64
N_DIR = 2
MIX_WIDTH = A_WIDTH + B_WIDTH + C_WIDTH
D_FF = -(-(8 * D_MODEL) // (3 * 256)) * 256
EPS = 1e-6

LANES = 128
N_CTX_ROWS = BATCH * SEQ
N_LAT_ROWS = DEC_BATCH * DEC_SEQ
N_ROWS = N_CTX_ROWS + N_LAT_ROWS
N_COND = 8

ZA_W = A_WIDTH + 2 * A_KV_WIDTH
ZB_W = 2 * B_QK_WIDTH + 2 * B_WIDTH + LANES
ZC_W = 2 * C_QK_WIDTH + 2 * C_WIDTH
PROJ_PAD = ZA_W + ZB_W + ZC_W
SMALL_OFF = 2 * B_QK_WIDTH + 2 * B_WIDTH
NAT_SMALL = A_WIDTH + 2 * A_KV_WIDTH + 2 * B_QK_WIDTH + 2 * B_WIDTH
NAT_C = NAT_SMALL + N_DIR * GATE_RANK
NAT_BC = NAT_C + ZC_W
BETA_LANE = N_DIR * GATE_RANK
DECAY_LANE = BETA_LANE + N_DIR * C_HEADS

VMEM_LIMIT = 56 * 1024 * 1024


def _split(x):
    hi = x.astype(BF16)
    lo = (x - hi.astype(F32)).astype(BF16)
    return hi, lo


def _bdot(a, b):
    return jnp.dot(a.astype(BF16), b.astype(BF16), preferred_element_type=F32)


def _bdot_tn(a, b):
    return lax.dot_general(a.astype(BF16), b.astype(BF16), (((0,), (0,)), ((), ())),
                           preferred_element_type=F32)


def _dot3(a, b):
    ah, al = _split(a)
    bh, bl = _split(b)
    return (jnp.dot(ah, bh, preferred_element_type=F32)
            + jnp.dot(ah, bl, preferred_element_type=F32)
            + jnp.dot(al, bh, preferred_element_type=F32))


def _dot_lhs_exact(m_bf16, x):
    xh, xl = _split(x)
    return (jnp.dot(m_bf16, xh, preferred_element_type=F32)
            + jnp.dot(m_bf16, xl, preferred_element_type=F32))


def _dot_rhs_exact(x, m_bf16):
    xh, xl = _split(x)
    return (jnp.dot(xh, m_bf16, preferred_element_type=F32)
            + jnp.dot(xl, m_bf16, preferred_element_type=F32))


def _group_sumsq(x, ones_g):
    return jnp.dot((x * x).astype(BF16), ones_g, preferred_element_type=F32)


def _group_ones(width, group):
    r = lax.broadcasted_iota(jnp.int32, (width, width), 0) // group
    c = lax.broadcasted_iota(jnp.int32, (width, width), 1) // group
    return (r == c).astype(BF16)


def _sigmoid(x):
    return 1.0 / (1.0 + jnp.exp(-x))


def _silu(x):
    return x * _sigmoid(x)


def _softplus(x):
    return jnp.maximum(x, 0.0) + jnp.log1p(jnp.exp(-jnp.abs(x)))


def _rms(x, g):
    return x * lax.rsqrt(jnp.mean(x * x, axis=-1, keepdims=True) + EPS) * g


def _chunk_tri(t, reverse):
    r = lax.broadcasted_iota(jnp.int32, (t, t), 0)
    c = lax.broadcasted_iota(jnp.int32, (t, t), 1)
    same = (r // CHUNK) == (c // CHUNK)
    tri = (c >= r) if reverse else (c <= r)
    return (same & tri).astype(BF16)


def _mod_row(i, tm):
    start = i * tm
    return jnp.where(start < N_CTX_ROWS, 0, 1 + (start - N_CTX_ROWS) // DEC_SEQ)


def _mod_kernel(cond_ref, w_ref, b_ref, o_ref):
    c = cond_ref[...]
    o_ref[0] = _bdot(_silu(c), w_ref[0]) + b_ref[0]


def _modulation(cond, w_mod, b_mod):
    tn = 1536
    return pl.pallas_call(
        _mod_kernel,
        grid=(DEPTH, 6 * D_MODEL // tn),
        in_specs=[pl.BlockSpec((N_COND, D_MODEL), lambda l, j: (0, 0)),
                  pl.BlockSpec((1, D_MODEL, tn), lambda l, j: (l, 0, j)),
                  pl.BlockSpec((1, 1, tn), lambda l, j: (l, 0, j))],
        out_specs=pl.BlockSpec((1, N_COND, tn), lambda l, j: (l, 0, j)),
        out_shape=jax.ShapeDtypeStruct((DEPTH, N_COND, 6 * D_MODEL), F32),
        compiler_params=pltpu.CompilerParams(vmem_limit_bytes=VMEM_LIMIT),
        name="adaln_mod",
    )(cond, w_mod, b_mod.reshape(DEPTH, 1, 6 * D_MODEL))


DENSE_TM = 512
N_CTX_BLOCKS = N_CTX_ROWS // DENSE_TM


def _ctx_map(i):
    return (jnp.minimum(i, N_CTX_BLOCKS - 1), 0)


def _lat_map(i):
    return (jnp.maximum(i - N_CTX_BLOCKS, 0), 0)


def _inproj_kernel(xc_ref, xl_ref, mod_ref, ng_ref, w_ref, za_ref, zb_ref, zc_ref, wp_s):
    @pl.when(pl.program_id(0) == 0)
    def _():
        n_gate = N_DIR * GATE_RANK
        n_bc = 2 * N_DIR * C_HEADS
        wp_s[:, 0:NAT_SMALL] = w_ref[0, :, 0:NAT_SMALL].astype(BF16)
        wp_s[:, NAT_SMALL:NAT_SMALL + n_gate] = w_ref[0, :, NAT_SMALL:NAT_C].astype(BF16)
        wp_s[:, NAT_SMALL + n_gate:NAT_SMALL + n_gate + n_bc] = w_ref[0, :, NAT_BC:NAT_BC + n_bc].astype(BF16)
        wp_s[:, NAT_SMALL + n_gate + n_bc:NAT_SMALL + LANES] = jnp.zeros(
            (D_MODEL, LANES - n_gate - n_bc), BF16)
        wp_s[:, NAT_SMALL + LANES:PROJ_PAD] = w_ref[0, :, NAT_C:NAT_BC].astype(BF16)

    x = jnp.where(pl.program_id(0) < N_CTX_BLOCKS, xc_ref[...], xl_ref[...])
    mod = mod_ref[0]
    shift = mod[:, 0:D_MODEL]
    scale = mod[:, D_MODEL:2 * D_MODEL]
    h = _rms(x, ng_ref[0, 0:1, :]) * (1.0 + scale) + shift
    hb = h.astype(BF16)
    za_ref[...] = jnp.dot(hb, wp_s[:, 0:ZA_W], preferred_element_type=F32)
    zb_ref[...] = jnp.dot(hb, wp_s[:, ZA_W:ZA_W + ZB_W], preferred_element_type=F32)
    zc_ref[...] = jnp.dot(hb, wp_s[:, ZA_W + ZB_W:PROJ_PAD], preferred_element_type=F32)


def _in_projection(xc, xl, mods, norm_gains, w_in, layer):
    tm = DENSE_TM
    proj_width = w_in.shape[-1]
    return pl.pallas_call(
        _inproj_kernel,
        grid=(N_ROWS // tm,),
        in_specs=[pl.BlockSpec((tm, D_MODEL), _ctx_map),
                  pl.BlockSpec((tm, D_MODEL), _lat_map),
                  pl.BlockSpec((1, 1, 6 * D_MODEL), lambda i: (layer * N_COND + _mod_row(i, tm), 0, 0)),
                  pl.BlockSpec((1, 4, D_MODEL), lambda i: (layer, 0, 0)),
                  pl.BlockSpec((1, D_MODEL, proj_width), lambda i: (layer, 0, 0),
                               pipeline_mode=pl.Buffered(1))],
        out_specs=[pl.BlockSpec((tm, ZA_W), lambda i: (i, 0)),
                   pl.BlockSpec((tm, ZB_W), lambda i: (i, 0)),
                   pl.BlockSpec((tm, ZC_W), lambda i: (i, 0))],
        out_shape=[jax.ShapeDtypeStruct((N_ROWS, ZA_W), F32),
                   jax.ShapeDtypeStruct((N_ROWS, ZB_W), F32),
                   jax.ShapeDtypeStruct((N_ROWS, ZC_W), F32)],
        scratch_shapes=[pltpu.VMEM((D_MODEL, PROJ_PAD), BF16)],
        compiler_params=pltpu.CompilerParams(
            dimension_semantics=("arbitrary",), vmem_limit_bytes=VMEM_LIMIT),
        name="in_proj",
    )(xc, xl, mods, norm_gains, w_in)


FF_TILE = 256


def _ffn_kernel(xc_ref, xl_ref, oac_ref, oal_ref, obc_ref, obl_ref, occ_ref, ocl_ref, mod_ref, ng_ref,
                wo_ref, wg_ref, wu_ref, wd_ref, yc_ref, yl_ref):
    is_ctx = pl.program_id(0) < N_CTX_BLOCKS
    x = jnp.where(is_ctx, xc_ref[...], xl_ref[...])
    mod = mod_ref[0]
    gate_m = mod[:, 2 * D_MODEL:3 * D_MODEL]
    shift_f = mod[:, 3 * D_MODEL:4 * D_MODEL]
    scale_f = mod[:, 4 * D_MODEL:5 * D_MODEL]
    gate_f = mod[:, 5 * D_MODEL:6 * D_MODEL]
    mix_in = jnp.concatenate([jnp.where(is_ctx, oac_ref[...], oal_ref[...]),
                              jnp.where(is_ctx, obc_ref[...], obl_ref[...]),
                              jnp.where(is_ctx, occ_ref[...], ocl_ref[...])], axis=-1)
    mix = _bdot(mix_in, wo_ref[0])
    x1 = x + gate_m * _rms(mix, ng_ref[0, 1:2, :])
    h = (_rms(x1, ng_ref[0, 2:3, :]) * (1.0 + scale_f) + shift_f).astype(BF16)
    f = jnp.zeros(x.shape, F32)
    for j in range(D_FF // FF_TILE):
        cols = slice(j * FF_TILE, (j + 1) * FF_TILE)
        g = jnp.dot(h, wg_ref[0, :, cols], preferred_element_type=F32)
        u = jnp.dot(h, wu_ref[0, :, cols], preferred_element_type=F32)
        f = f + _bdot(_silu(g) * u, wd_ref[0, cols, :])
    y = x1 + gate_f * _rms(f, ng_ref[0, 3:4, :])

    @pl.when(is_ctx)
    def _():
        yc_ref[...] = y

    @pl.when(jnp.logical_not(is_ctx))
    def _():
        yl_ref[...] = y


def _out_ffn(xc, xl, mixer_outs, mods, norm_gains, w_out_b, w_gate_b, w_up_b, w_down_b, layer):
    tm = DENSE_TM
    resident = dict(pipeline_mode=pl.Buffered(1))
    pair_specs = []
    for width in (D_MODEL, A_WIDTH, B_WIDTH, C_WIDTH):
        pair_specs += [pl.BlockSpec((tm, width), _ctx_map), pl.BlockSpec((tm, width), _lat_map)]
    return pl.pallas_call(
        _ffn_kernel,
        grid=(N_ROWS // tm,),
        in_specs=pair_specs + [
                  pl.BlockSpec((1, 1, 6 * D_MODEL), lambda i: (layer * N_COND + _mod_row(i, tm), 0, 0)),
                  pl.BlockSpec((1, 4, D_MODEL), lambda i: (layer, 0, 0)),
                  pl.BlockSpec((1, MIX_WIDTH, D_MODEL), lambda i: (layer, 0, 0), **resident),
                  pl.BlockSpec((1, D_MODEL, D_FF), lambda i: (layer, 0, 0), **resident),
                  pl.BlockSpec((1, D_MODEL, D_FF), lambda i: (layer, 0, 0), **resident),
                  pl.BlockSpec((1, D_FF, D_MODEL), lambda i: (layer, 0, 0), **resident)],
        out_specs=[pl.BlockSpec((tm, D_MODEL), _ctx_map), pl.BlockSpec((tm, D_MODEL), _lat_map)],
        out_shape=[jax.ShapeDtypeStruct((N_CTX_ROWS, D_MODEL), F32),
                   jax.ShapeDtypeStruct((N_LAT_ROWS, D_MODEL), F32)],
        compiler_params=pltpu.CompilerParams(
            dimension_semantics=("arbitrary",), vmem_limit_bytes=VMEM_LIMIT),
        name="out_ffn",
    )(xc, xl, *mixer_outs, mods, norm_gains, w_out_b, w_gate_b, w_up_b, w_down_b)


ATT_TQ = 512
ATT_LOOKAHEAD = 1


def _head_norm(x, gain_row, ones_g):
    ss = _group_sumsq(x, ones_g)
    return x * lax.rsqrt(ss * (1.0 / HEAD_DIM) + EPS) * gain_row


def _rope(x, cos, sin_a, sin_b):
    return x * cos + pltpu.roll(x, LANES - 16, 1) * sin_a + pltpu.roll(x, 16, 1) * sin_b


def _attn_kernel(*refs, seq_len, tq, latent):
    if latent:
        (q_ref, kv_ref, ck_ref, cv_ref, qkg_ref, cos_ref, sa_ref, sb_ref, o_ref, k_s, v_s) = refs
    else:
        (q_ref, kv_ref, qkg_ref, o_ref, kn_ref, vn_ref, k_s, v_s) = refs
    j = pl.program_id(1)
    ones_g = _group_ones(LANES, HEAD_DIM)
    gq = qkg_ref[0, 0:1, :]
    gk = qkg_ref[0, 1:2, :]
    past = PAST_LEN if latent else 0

    @pl.when(j == 0)
    def _():
        ka = kv_ref[:, 0:A_KV_WIDTH]
        va = kv_ref[:, A_KV_WIDTH:2 * A_KV_WIDTH]
        kn = _head_norm(ka, gk, ones_g)
        if latent:
            kr = _rope(kn, cos_ref[...], sa_ref[...], sb_ref[...])
            ck = ck_ref[...]
            cv = cv_ref[...]
            for g in range(A_KV_HEADS):
                cols = slice(g * HEAD_DIM, (g + 1) * HEAD_DIM)
                k_s[g, 0:past, :] = ck[:, cols].astype(BF16)
                v_s[g, 0:past, :] = cv[:, cols].astype(BF16)
                k_s[g, past:past + seq_len, :] = kr[:, cols].astype(BF16)
                v_s[g, past:past + seq_len, :] = va[:, cols].astype(BF16)
        else:
            kn_ref[...] = kn
            vn_ref[...] = va
            for g in range(A_KV_HEADS):
                cols = slice(g * HEAD_DIM, (g + 1) * HEAD_DIM)
                k_s[g] = kn[:, cols].astype(BF16)
                v_s[g] = va[:, cols].astype(BF16)

    if latent:
        rows = pl.ds(pl.multiple_of(j * tq, tq), tq)
        cos = cos_ref[rows, :]
        sin_a = sa_ref[rows, :]
        sin_b = sb_ref[rows, :]
    q_heads = []
    for t in range(A_WIDTH // LANES):
        qt = _head_norm(q_ref[:, t * LANES:(t + 1) * LANES], gq, ones_g)
        if latent:
            qt = _rope(qt, cos, sin_a, sin_b)
        qt = (qt * (HEAD_DIM ** -0.5)).astype(BF16)
        q_heads += [qt[:, hh * HEAD_DIM:(hh + 1) * HEAD_DIM] for hh in range(LANES // HEAD_DIM)]

    def scores(h):
        return lax.dot_general(q_heads[h], k_s[h // A_REP], (((1,), (1,)), ((), ())),
                               preferred_element_type=F32)

    outs = []
    pending = [scores(h) for h in range(ATT_LOOKAHEAD)]
    for h in range(A_HEADS):
        s = pending.pop(0)
        if h + ATT_LOOKAHEAD < A_HEADS:
            pending.append(scores(h + ATT_LOOKAHEAD))
        m = jnp.max(s, axis=-1, keepdims=True)
        p = jnp.exp(s - m)
        l = jnp.sum(p, axis=-1, keepdims=True)
        o = jnp.dot(p.astype(BF16), v_s[h // A_REP], preferred_element_type=F32)
        outs.append(o / l)
    o_ref[...] = jnp.concatenate(outs, axis=-1)


def _attention(za, qk_gain2, layer, latent, cache_k=None, cache_v=None, rope=None):
    seq_len = DEC_SEQ if latent else SEQ
    n_seq = DEC_BATCH if latent else BATCH
    row0 = N_CTX_ROWS // seq_len if latent else 0
    tq = min(ATT_TQ, seq_len)
    nq = seq_len // tq
    row0q = N_CTX_ROWS // tq if latent else 0
    s_len = seq_len + (PAST_LEN if latent else 0)
    in_specs = [pl.BlockSpec((tq, A_WIDTH), lambda s, j: (row0q + s * nq + j, 0)),
                pl.BlockSpec((seq_len, 2 * A_KV_WIDTH), lambda s, j: (row0 + s, 2))]
    args = [za, za]
    if latent:
        in_specs += [pl.BlockSpec((None, None, PAST_LEN, A_KV_WIDTH), lambda s, j: (s, layer, 0, 0)),
                     pl.BlockSpec((None, None, PAST_LEN, A_KV_WIDTH), lambda s, j: (s, layer, 0, 0))]
        args += [cache_k, cache_v]
    in_specs.append(pl.BlockSpec((1, 2, LANES), lambda s, j: (layer, 0, 0)))
    args.append(qk_gain2)
    if latent:
        in_specs += [pl.BlockSpec((seq_len, LANES), lambda s, j: (0, 0))] * 3
        args += list(rope)
    out_specs = [pl.BlockSpec((tq, A_WIDTH), lambda s, j: (s * nq + j, 0))]
    out_shape = [jax.ShapeDtypeStruct((n_seq * seq_len, A_WIDTH), F32)]
    if not latent:
        out_specs += [pl.BlockSpec((None, seq_len, A_KV_WIDTH), lambda s, j: (s, 0, 0))] * 2
        out_shape += [jax.ShapeDtypeStruct((n_seq, seq_len, A_KV_WIDTH), F32)] * 2
    return pl.pallas_call(
        functools.partial(_attn_kernel, seq_len=seq_len, tq=tq, latent=latent),
        grid=(n_seq, nq),
        in_specs=in_specs,
        out_specs=out_specs,
        out_shape=out_shape,
        scratch_shapes=[pltpu.VMEM((A_KV_HEADS, s_len, HEAD_DIM), BF16),
                        pltpu.VMEM((A_KV_HEADS, s_len, HEAD_DIM), BF16)],
        compiler_params=pltpu.CompilerParams(
            dimension_semantics=("arbitrary", "arbitrary"), vmem_limit_bytes=VMEM_LIMIT),
        name="attn_latent" if latent else "attn_ctx",
    )(*args)


def _gla_kernel(*refs, seq_len, n_sub, latent):
    if latent:
        zb_ref, wg_ref, bg_ref, gn_ref, s0_ref, ob_ref = refs
    else:
        zb_ref, wg_ref, bg_ref, gn_ref, ob_ref, sfin_ref = refs
    nc = seq_len // CHUNK
    q = zb_ref[:, 0:B_QK_WIDTH] * (B_DK ** -0.5)
    k = zb_ref[:, B_QK_WIDTH:2 * B_QK_WIDTH]
    v = zb_ref[:, 2 * B_QK_WIDTH:2 * B_QK_WIDTH + B_WIDTH]
    small = zb_ref[:, SMALL_OFF:SMALL_OFF + LANES]
    pre = _dot3(small, wg_ref[0]) + bg_ref[0]
    glog = (jnp.minimum(pre, 0.0) - jnp.log1p(jnp.exp(-jnp.abs(pre)))) * (1.0 / GATE_TAU)

    kr = lax.broadcasted_iota(jnp.int32, (B_QK_WIDTH, B_WIDTH), 0) // B_DK
    vc = lax.broadcasted_iota(jnp.int32, (B_QK_WIDTH, B_WIDTH), 1) // B_DV
    bd_kv = kr == vc
    sr = lax.broadcasted_iota(jnp.int32, (B_WIDTH, B_WIDTH), 0) // CHUNK
    sc = lax.broadcasted_iota(jnp.int32, (B_WIDTH, B_WIDTH), 1) // B_DV
    bd_sv = sr == sc
    tt = lax.broadcasted_iota(jnp.int32, (CHUNK, B_WIDTH), 0)
    ss = lax.broadcasted_iota(jnp.int32, (CHUNK, B_WIDTH), 1) % CHUNK

    q3 = q.reshape(nc, CHUNK, B_QK_WIDTH)
    k3 = k.reshape(nc, CHUNK, B_QK_WIDTH)
    intra, upds, decays, q_sts = [], [], [], []
    for d in range(N_DIR):
        g = glog[:, d * B_QK_WIDTH:(d + 1) * B_QK_WIDTH]
        tri = _chunk_tri(CUM_ROWS, d == 1)
        b = jnp.concatenate([_dot_lhs_exact(tri, g[r * CUM_ROWS:(r + 1) * CUM_ROWS, :])
                             for r in range(seq_len // CUM_ROWS)], axis=0)
        b3 = b.reshape(nc, CHUNK, B_QK_WIDTH)
        mid = b3[:, CHUNK // 2:CHUNK // 2 + 1, :]
        last = b3[:, 0:1, :] if d == 1 else b3[:, CHUNK - 1:CHUNK, :]
        q_in = q3 * jnp.exp(b3 - mid)
        k_in = k3 * jnp.exp(mid - b3)
        q_sts.append(q3 * jnp.exp(b3))
        k_st = k3 * jnp.exp(last - b3)
        decay_t = jnp.exp(jnp.broadcast_to(last, (nc, 8, B_QK_WIDTH)).reshape(nc * 8, B_QK_WIDTH)).T
        causal = (ss >= tt) if d == 1 else (ss <= tt)
        v_chunks = [v[c * CHUNK:(c + 1) * CHUNK, :] for c in range(nc)]
        atts = [_bdot(q_in[c], jnp.where(bd_kv, jnp.concatenate([k_in[c].T] * B_HEADS, axis=1), 0.0))
                for c in range(nc)]
        upds.append([jnp.where(bd_kv, _bdot_tn(k_st[c], v_chunks[c]), 0.0) for c in range(nc)])
        intra.append([_bdot(jnp.where(causal, atts[c], 0.0),
                            jnp.where(bd_sv, jnp.concatenate([v_chunks[c]] * B_HEADS, axis=0), 0.0))
                      for c in range(nc)])
        decays.append([decay_t[:, 8 * c:8 * c + 1] for c in range(nc)])
    o_dirs = []
    per_seq = nc // n_sub
    for d in range(N_DIR):
        o_chunks = [None] * nc
        for sq in range(n_sub):
            state = s0_ref[d] if latent else jnp.zeros((B_QK_WIDTH, B_WIDTH), F32)
            chunks = range(sq * per_seq, (sq + 1) * per_seq)
            for c in (reversed(chunks) if d == 1 else chunks):
                o_chunks[c] = intra[d][c] + _bdot(q_sts[d][c], state)
                state = decays[d][c] * state + upds[d][c]
            if not latent:
                sfin_ref[sq, d] = state
        o_dirs.append(jnp.concatenate(o_chunks, axis=0))
    o = o_dirs[0] + o_dirs[1]
    ms = _group_sumsq(o, _group_ones(B_WIDTH, B_DV)) * (1.0 / B_DV)
    r = zb_ref[:, 2 * B_QK_WIDTH + B_WIDTH:2 * B_QK_WIDTH + 2 * B_WIDTH]
    ob_ref[...] = o * lax.rsqrt(ms + EPS) * gn_ref[0] * _silu(r)


GLA_CTX_SEQS = 8


def _gla(zb, wg_p, bg_p, gn_p, layer, latent, s0_bd=None):
    n_sub = 1 if latent else GLA_CTX_SEQS
    seq_len = (DEC_SEQ if latent else SEQ) * n_sub
    n_seq = (DEC_BATCH if latent else BATCH) // n_sub
    row0 = N_CTX_ROWS // seq_len if latent else 0
    in_specs = [pl.BlockSpec((seq_len, ZB_W), lambda s: (row0 + s, 0)),
                pl.BlockSpec((1, LANES, N_DIR * B_QK_WIDTH), lambda s: (layer, 0, 0)),
                pl.BlockSpec((1, 1, N_DIR * B_QK_WIDTH), lambda s: (layer, 0, 0)),
                pl.BlockSpec((1, 1, B_WIDTH), lambda s: (layer, 0, 0))]
    args = [zb, wg_p, bg_p, gn_p]
    out_specs = [pl.BlockSpec((seq_len, B_WIDTH), lambda s: (s, 0))]
    out_shape = [jax.ShapeDtypeStruct((n_seq * seq_len, B_WIDTH), F32)]
    if latent:
        in_specs.append(pl.BlockSpec((None, None, N_DIR, B_QK_WIDTH, B_WIDTH), lambda s: (s, layer, 0, 0, 0)))
        args.append(s0_bd)
    else:
        out_specs.append(pl.BlockSpec((n_sub, N_DIR, B_QK_WIDTH, B_WIDTH), lambda s: (s, 0, 0, 0)))
        out_shape.append(jax.ShapeDtypeStruct((n_seq * n_sub, N_DIR, B_QK_WIDTH, B_WIDTH), F32))
    return pl.pallas_call(
        functools.partial(_gla_kernel, seq_len=seq_len, n_sub=n_sub, latent=latent),
        grid=(n_seq,),
        in_specs=in_specs,
        out_specs=out_specs,
        out_shape=out_shape,
        compiler_params=pltpu.CompilerParams(vmem_limit_bytes=VMEM_LIMIT),
        name="gla_latent" if latent else "gla_ctx",
    )(*args)


TRI_BASE = 8
GROUP_HEADS = 2
GW = GROUP_HEADS * C_DK
C_GROUPS = C_HEADS // GROUP_HEADS
N_SYS = N_DIR * C_GROUPS
PREP_CHUNKS = 4
CUM_ROWS = 256
CONV_PAD = 8


def _bd(y, mask):
    return jnp.where(mask, jnp.concatenate([y] * GROUP_HEADS, axis=0), 0.0).astype(BF16)


def _group_tri_inverse(ms, bd_mask, tt, ss):
    pws = [jnp.where((tt // TRI_BASE) == (ss // TRI_BASE), -m, 0.0) for m in ms]
    invs = [(tt == ss).astype(F32) + pw for pw in pws]
    span = 2
    while span < TRI_BASE:
        pws = [_bdot(pw, _bd(pw, bd_mask)) for pw in pws]
        invs = [inv + _bdot(inv, _bd(pw, bd_mask)) for inv, pw in zip(invs, pws)]
        span *= 2
    size = 2 * TRI_BASE
    while size <= CHUNK:
        off = ((tt // size) == (ss // size)) & ((tt // (size // 2)) != (ss // (size // 2)))
        xcs = [_bdot(inv, _bd(jnp.where(off, m, 0.0), bd_mask)) for m, inv in zip(ms, invs)]
        invs = [inv - _bdot(xc, _bd(inv, bd_mask)) for inv, xc in zip(invs, xcs)]
        size *= 2
    return invs


def _delta_kernel(*refs, seq_len, n_sub, latent):
    if latent:
        (zc_ref, sm_ref, cw_ref, al_ref, dtb_ref, dn_ref, s0_ref, oc_ref,
         xp_s, yp_s, q_s, k_s, v_s, bx_s, gx_s, gr_s, o_s, u_s, wq_s, ak_s, eg_s) = refs
    else:
        (zc_ref, sm_ref, cw_ref, al_ref, dtb_ref, dn_ref, oc_ref, sfin_ref,
         xp_s, yp_s, q_s, k_s, v_s, bx_s, gx_s, gr_s, o_s, u_s, wq_s, ak_s, eg_s) = refs
    nc = seq_len // CHUNK
    per_seq = nc // n_sub

    len1 = seq_len // n_sub
    seg = len1 // 8 + 1
    sub = lax.broadcasted_iota(jnp.int32, (8, LANES), 0)
    half = CONV_WIDTH // 2
    for j in range(CONV_CH // LANES):
        cols = slice(j * LANES, (j + 1) * LANES)
        taps = [cw_ref[0, tap:tap + 1, cols] for tap in range(CONV_WIDTH)]
        for sq in range(n_sub):
            base = sq * 8 * seg
            xp_s[j, base:base + len1, :] = zc_ref[sq * len1:(sq + 1) * len1, cols]
            xp_s[j, base + len1:base + 8 * seg, :] = jnp.zeros((8 * seg - len1, LANES), F32)
            xr = [xp_s[j, pl.ds(base + a, 8, stride=seg), :] for a in range(seg)]
            nxt = [jnp.where(sub < 7, pltpu.roll(xr[a], 7, 0), 0.0) for a in range(half)]
            prv = [jnp.where(sub > 0, pltpu.roll(xr[seg - half + a], 1, 0), 0.0) for a in range(half)]
            window = prv + xr + nxt
            for a in range(seg):
                acc = window[a] * taps[0]
                for tap in range(1, CONV_WIDTH):
                    acc = acc + window[a + tap] * taps[tap]
                yp_s[j, pl.ds(base + a, 8, stride=seg), :] = acc
    y = _silu(jnp.concatenate(
        [jnp.concatenate([yp_s[j, sq * 8 * seg:sq * 8 * seg + len1, :] for sq in range(n_sub)], axis=0)
         for j in range(CONV_CH // LANES)], axis=1))
    ones_g = _group_ones(C_QK_WIDTH, C_DK)
    qc = y[:, 0:C_QK_WIDTH]
    kc = y[:, C_QK_WIDTH:2 * C_QK_WIDTH]
    q_s[...] = qc * lax.rsqrt(_group_sumsq(qc, ones_g) + EPS) * (C_DK ** -0.5)
    k_s[...] = kc * lax.rsqrt(_group_sumsq(kc, ones_g) + EPS)
    v_s[...] = y[:, 2 * C_QK_WIDTH:CONV_CH]

    small = sm_ref[...]
    beta = _sigmoid(small)
    glog = -jnp.exp(al_ref[0]) * _softplus(small + dtb_ref[0])
    lane_r = lax.broadcasted_iota(jnp.int32, (LANES, C_WIDTH), 0)
    head_c = lax.broadcasted_iota(jnp.int32, (LANES, C_WIDTH), 1) // C_DV
    cr = lax.broadcasted_iota(jnp.int32, (CUM_ROWS, CUM_ROWS), 0)
    cl = lax.broadcasted_iota(jnp.int32, (CUM_ROWS, CUM_ROWS), 1)
    same_chunk = ((cr // CHUNK) == (cl // CHUNK)).astype(BF16)
    on_diag = (lax.broadcasted_iota(jnp.int32, (CUM_ROWS, C_WIDTH), 0) % CHUNK
               == lax.broadcasted_iota(jnp.int32, (CUM_ROWS, C_WIDTH), 1) % C_DV)
    for d in range(N_DIR):
        bx_s[d] = _dot_rhs_exact(beta, (lane_r == BETA_LANE + d * C_HEADS + head_c).astype(BF16))
    pick_gam = [(lane_r == DECAY_LANE + d * C_HEADS + head_c).astype(BF16) for d in range(N_DIR)]
    tris = [_chunk_tri(CUM_ROWS, d == 1) for d in range(N_DIR)]
    blocks = [(d, slice(blk * CUM_ROWS, (blk + 1) * CUM_ROWS))
              for d in range(N_DIR) for blk in range(seq_len // CUM_ROWS)]
    cums = [_dot_lhs_exact(tris[d], glog[r, :]) for d, r in blocks]
    g_xs = [_dot_rhs_exact(cum, pick_gam[d]) for (d, r), cum in zip(blocks, cums)]
    g_rs = [_dot_lhs_exact(same_chunk, jnp.where(on_diag, g_x, 0.0)) for g_x in g_xs]
    for (d, r), g_x, g_r in zip(blocks, g_xs, g_rs):
        gx_s[d, r, :] = g_x
        gr_s[d, r, :] = g_r

    tt = lax.broadcasted_iota(jnp.int32, (CHUNK, GW), 0)
    ss = lax.broadcasted_iota(jnp.int32, (CHUNK, GW), 1) % CHUNK
    bd1 = (lax.broadcasted_iota(jnp.int32, (GW, GW), 0) // CHUNK
           == lax.broadcasted_iota(jnp.int32, (GW, GW), 1) // CHUNK)
    bd2 = jnp.concatenate([bd1, bd1], axis=1)

    prep = min(PREP_CHUNKS, nc)

    def prepare_chunks(step, carry):
        ids, ms, rhss, qgs, aks, egs = [], [], [], [], [], []
        for cj in range(prep):
            c = step * prep + cj
            rows = pl.ds(pl.multiple_of(c * CHUNK, CHUNK), CHUNK)
            k_t = k_s[rows, :].T
            k_bds = [_bd(k_s[rows, p * GW:(p + 1) * GW], bd1) for p in range(C_GROUPS)]
            k_t_groups = [jnp.concatenate(
                [k_t[p * GW + h * C_DK:p * GW + (h + 1) * C_DK, :] for h in range(GROUP_HEADS)], axis=1)
                for p in range(C_GROUPS)]
            for d in range(N_DIR):
                ahead = (tt - ss) if d == 1 else (ss - tt)
                last = 0 if d == 1 else CHUNK - 1
                for p in range(C_GROUPS):
                    lanes = slice(p * GW, (p + 1) * GW)
                    qp = q_s[rows, lanes]
                    kp = k_s[rows, lanes]
                    bx = bx_s[d, rows, lanes]
                    gx = gx_s[d, rows, lanes]
                    gr = gr_s[d, rows, lanes]
                    dec = jnp.exp(jnp.where(ahead <= 0, gx - gr, -jnp.inf))
                    kb = kp * bx
                    ma = lax.dot_general(jnp.concatenate([kb, qp], axis=0).astype(BF16), k_bds[p],
                                         (((1,), (1,)), ((), ())), preferred_element_type=F32)
                    e_gx = jnp.exp(gx)
                    g_last = gx[last:last + 1, :]
                    ids.append((c, d * C_GROUPS + p))
                    ms.append(jnp.where(ahead < 0, ma[0:CHUNK] * dec, 0.0))
                    rhss.append(jnp.concatenate([v_s[rows, lanes] * bx, kb * e_gx], axis=1))
                    qgs.append(qp * e_gx)
                    aks.append(jnp.concatenate([ma[CHUNK:2 * CHUNK] * dec, k_t_groups[p] * jnp.exp(g_last - gr)],
                                               axis=0).astype(BF16))
                    egs.append(jnp.exp(g_last))
        invs = _group_tri_inverse(ms, bd1, tt, ss)
        sols = [_bdot(inv, _bd(rhs, bd2)) for inv, rhs in zip(invs, rhss)]
        m_sols = []
        for m, sol in zip(ms, sols):
            sh, sl = _split(sol)
            mh, ml = _split(m)
            shb, slb = _bd(sh, bd2), _bd(sl, bd2)
            m_sols.append(jnp.dot(mh, shb, preferred_element_type=F32)
                          + jnp.dot(mh, slb, preferred_element_type=F32)
                          + jnp.dot(ml, shb, preferred_element_type=F32))
        sols = [sol + _bdot(inv, _bd(rhs - sol - m_sol, bd2))
                for inv, rhs, sol, m_sol in zip(invs, rhss, sols, m_sols)]
        for (c, sy), sol, qg, ak, eg in zip(ids, sols, qgs, aks, egs):
            u_s[c, sy] = sol[:, 0:GW]
            wq_s[c, sy] = jnp.concatenate([sol[:, GW:2 * GW], qg], axis=0).astype(BF16)
            ak_s[c, sy] = ak
            eg_s[c, sy] = eg
        return carry

    lax.fori_loop(0, nc // prep, prepare_chunks, 0)

    def scan_chunk(i, state):
        ids = [(sq, d, sq * per_seq + ((per_seq - 1 - i) if d == 1 else i), p)
               for sq in range(n_sub) for d in range(N_DIR) for p in range(C_GROUPS)]
        states = [state[sq * N_SYS + d * C_GROUPS + p] for sq, d, c, p in ids]
        wq_states = [jnp.dot(wq_s[c, d * C_GROUPS + p], _bd(st, bd1), preferred_element_type=F32)
                     for (sq, d, c, p), st in zip(ids, states)]
        v_news = [u_s[c, d * C_GROUPS + p] - wqs[0:CHUNK] for (sq, d, c, p), wqs in zip(ids, wq_states)]
        aks = [jnp.dot(ak_s[c, d * C_GROUPS + p], _bd(v_new, bd1), preferred_element_type=F32)
               for (sq, d, c, p), v_new in zip(ids, v_news)]
        new_state = [eg_s[c, d * C_GROUPS + p] * st + ak[CHUNK:2 * CHUNK]
                     for (sq, d, c, p), st, ak in zip(ids, states, aks)]
        for (sq, d, c, p), wqs, ak in zip(ids, wq_states, aks):
            o_s[d, pl.ds(pl.multiple_of(c * CHUNK, CHUNK), CHUNK), p * GW:(p + 1) * GW] = (
                wqs[CHUNK:2 * CHUNK] + ak[0:CHUNK])
        return jnp.stack(new_state)

    if latent:
        state0 = s0_ref[...]
    else:
        state0 = jnp.zeros((n_sub * N_SYS, C_DK, GW), F32)
    state = lax.fori_loop(0, per_seq, scan_chunk, state0)

    o = o_s[0] + o_s[1]
    ms = _group_sumsq(o, _group_ones(C_WIDTH, C_DV)) * (1.0 / C_DV)
    gate = zc_ref[:, CONV_CH:CONV_CH + C_WIDTH]
    oc_ref[...] = o * lax.rsqrt(ms + EPS) * dn_ref[0] * _silu(gate)
    if not latent:
        sfin_ref[...] = state.reshape(n_sub, N_SYS, C_DK, GW)


DELTA_CTX_SEQS = 4


def _delta(zc, zb, conv_w, al_p, dtb_p, dn_p, layer, latent, state_delta=None):
    n_sub = 1 if latent else DELTA_CTX_SEQS
    seq_len = (DEC_SEQ if latent else SEQ) * n_sub
    n_seq = (DEC_BATCH if latent else BATCH) // n_sub
    row0 = N_CTX_ROWS // seq_len if latent else 0
    nc = seq_len // CHUNK
    in_specs = [pl.BlockSpec((seq_len, ZC_W), lambda s: (row0 + s, 0)),
                pl.BlockSpec((seq_len, LANES), lambda s: (row0 + s, SMALL_OFF // LANES)),
                pl.BlockSpec((1, CONV_WIDTH, CONV_CH), lambda s: (layer, 0, 0)),
                pl.BlockSpec((1, 1, LANES), lambda s: (layer, 0, 0)),
                pl.BlockSpec((1, 1, LANES), lambda s: (layer, 0, 0)),
                pl.BlockSpec((1, 1, C_WIDTH), lambda s: (layer, 0, 0))]
    args = [zc, zb, conv_w, al_p, dtb_p, dn_p]
    out_specs = [pl.BlockSpec((seq_len, C_WIDTH), lambda s: (s, 0))]
    out_shape = [jax.ShapeDtypeStruct((n_seq * seq_len, C_WIDTH), F32)]
    if latent:
        in_specs.append(pl.BlockSpec((None, None, N_SYS, C_DK, GW), lambda s: (s, layer, 0, 0, 0)))
        args.append(state_delta)
    else:
        out_specs.append(pl.BlockSpec((n_sub, N_SYS, C_DK, GW), lambda s: (s, 0, 0, 0)))
        out_shape.append(jax.ShapeDtypeStruct((n_seq * n_sub, N_SYS, C_DK, GW), F32))
    return pl.pallas_call(
        functools.partial(_delta_kernel, seq_len=seq_len, n_sub=n_sub, latent=latent),
        grid=(n_seq,),
        in_specs=in_specs,
        out_specs=out_specs,
        out_shape=out_shape,
        scratch_shapes=[pltpu.VMEM((CONV_CH // LANES, seq_len + n_sub * CONV_PAD, LANES), F32),
                        pltpu.VMEM((CONV_CH // LANES, seq_len + n_sub * CONV_PAD, LANES), F32),
                        pltpu.VMEM((seq_len, C_QK_WIDTH), F32),
                        pltpu.VMEM((seq_len, C_QK_WIDTH), F32),
                        pltpu.VMEM((seq_len, C_WIDTH), F32),
                        pltpu.VMEM((N_DIR, seq_len, C_WIDTH), F32),
                        pltpu.VMEM((N_DIR, seq_len, C_WIDTH), F32),
                        pltpu.VMEM((N_DIR, seq_len, C_WIDTH), F32),
                        pltpu.VMEM((N_DIR, seq_len, C_WIDTH), F32),
                        pltpu.VMEM((nc, N_SYS, CHUNK, GW), F32),
                        pltpu.VMEM((nc, N_SYS, 2 * CHUNK, GW), BF16),
                        pltpu.VMEM((nc, N_SYS, 2 * CHUNK, GW), BF16),
                        pltpu.VMEM((nc, N_SYS, 1, GW), F32)],
        compiler_params=pltpu.CompilerParams(vmem_limit_bytes=VMEM_LIMIT),
        name="delta_latent" if latent else "delta_ctx",
    )(*args)


def _rope_tables():
    rows = DEC_SEQ // GRID_W
    row = jnp.repeat(jnp.arange(rows, dtype=F32), GRID_W)
    col = jnp.tile(jnp.arange(GRID_W, dtype=F32), rows)
    n_freq = HEAD_DIM // 4
    inv_freq = ROPE_THETA ** (-jnp.arange(n_freq, dtype=F32) / n_freq)
    ang_r = row[:, None] * inv_freq
    ang_c = col[:, None] * inv_freq
    ang = jnp.concatenate([ang_r, ang_r, ang_c, ang_c], axis=-1)
    cos, sin = jnp.cos(ang), jnp.sin(ang)
    first = (jnp.arange(HEAD_DIM) % 32) < 16
    sin_a = jnp.where(first, -sin, 0.0)
    sin_b = jnp.where(first, 0.0, sin)
    reps = LANES // HEAD_DIM
    return tuple(jnp.tile(t, (1, reps)) for t in (cos, sin_a, sin_b))


def _block_diag_state(s):
    eye = jnp.eye(B_HEADS, dtype=s.dtype)
    out = jnp.einsum('...hkv,hg->...hkgv', s, eye)
    return out.reshape(s.shape[:-3] + (B_QK_WIDTH, B_WIDTH))


def _group_state(s):
    lead = s.shape[:-4]
    s = s.reshape(lead + (N_DIR, C_GROUPS, GROUP_HEADS, C_DK, C_DV))
    s = jnp.moveaxis(s, -3, -2)
    return s.reshape(lead + (N_SYS, C_DK, GW))


def _ungroup_state(s):
    lead = s.shape[:-3]
    s = s.reshape(lead + (N_DIR, C_GROUPS, C_DK, GROUP_HEADS, C_DV))
    s = jnp.moveaxis(s, -2, -3)
    return s.reshape(lead + (N_DIR, C_HEADS, C_DK, C_DV))


def _diag_blocks(s_bd):
    s5 = s_bd.reshape(s_bd.shape[:-2] + (B_HEADS, B_DK, B_HEADS, B_DV))
    return jnp.stack([s5[..., h, :, h, :] for h in range(B_HEADS)], axis=-3)


def kernel(x_prompt, x_sample, cache_k, cache_v, state_gla, state_delta, c, c_ctx, w_mod, b_mod, norm_gains, w_in, qk_gain, w_gla_gate, b_gla_gate, gla_norm, conv_w, a_log, dt_bias, delta_norm, w_out, w_gate, w_up, w_down):
    xc = x_prompt.reshape(N_CTX_ROWS, D_MODEL)
    xl = x_sample.reshape(N_LAT_ROWS, D_MODEL)
    cond = jnp.concatenate([c_ctx[None, :], c, jnp.zeros((N_COND - 1 - DEC_BATCH, D_MODEL), F32)], axis=0)
    w_out_b = w_out.astype(BF16)
    w_gate_b = w_gate.astype(BF16)
    w_up_b = w_up.astype(BF16)
    w_down_b = w_down.astype(BF16)
    qk_gain2 = jnp.tile(qk_gain, (1, 1, LANES // HEAD_DIM))
    wg_p = jnp.zeros((DEPTH, LANES, N_DIR * B_QK_WIDTH), F32)
    for d in range(N_DIR):
        wg_p = wg_p.at[:, d * GATE_RANK:(d + 1) * GATE_RANK, d * B_QK_WIDTH:(d + 1) * B_QK_WIDTH].set(w_gla_gate[:, d])
    bg_p = b_gla_gate.reshape(DEPTH, 1, N_DIR * B_QK_WIDTH)
    gn_p = jnp.tile(gla_norm, (1, B_HEADS)).reshape(DEPTH, 1, B_WIDTH)
    dn_p = jnp.tile(delta_norm, (1, C_HEADS)).reshape(DEPTH, 1, C_WIDTH)
    al_p = jnp.zeros((DEPTH, 1, LANES), F32).at[:, 0, DECAY_LANE:DECAY_LANE + N_DIR * C_HEADS].set(
        a_log.reshape(DEPTH, N_DIR * C_HEADS))
    dtb_p = jnp.zeros((DEPTH, 1, LANES), F32).at[:, 0, DECAY_LANE:DECAY_LANE + N_DIR * C_HEADS].set(
        dt_bias.reshape(DEPTH, N_DIR * C_HEADS))
    cache_k2 = cache_k.reshape(DEC_BATCH, DEPTH, PAST_LEN, A_KV_WIDTH)
    cache_v2 = cache_v.reshape(DEC_BATCH, DEPTH, PAST_LEN, A_KV_WIDTH)
    s0_gla_bd = _block_diag_state(state_gla.astype(F32))
    s0_delta = _group_state(state_delta.astype(F32))
    rope = _rope_tables()

    mods = _modulation(cond, w_mod, b_mod).reshape(DEPTH * N_COND, 1, 6 * D_MODEL)

    new_k, new_v, new_gla, new_delta = [], [], [], []
    for l in range(DEPTH):
        za, zb, zc = _in_projection(xc, xl, mods, norm_gains, w_in, l)
        oa_c, k_l, v_l = _attention(za, qk_gain2, l, False)
        (oa_l,) = _attention(za, qk_gain2, l, True, cache_k2, cache_v2, rope)
        ob_c, sg_l = _gla(zb, wg_p, bg_p, gn_p, l, False)
        (ob_l,) = _gla(zb, wg_p, bg_p, gn_p, l, True, s0_gla_bd)
        oc_c, sd_l = _delta(zc, zb, conv_w, al_p, dtb_p, dn_p, l, False)
        (oc_l,) = _delta(zc, zb, conv_w, al_p, dtb_p, dn_p, l, True, s0_delta)
        xc, xl = _out_ffn(xc, xl, (oa_c, oa_l, ob_c, ob_l, oc_c, oc_l), mods, norm_gains,
                          w_out_b, w_gate_b, w_up_b, w_down_b, l)
        new_k.append(k_l)
        new_v.append(v_l)
        new_gla.append(sg_l)
        new_delta.append(sd_l)

    out_dtype = x_prompt.dtype
    y_prompt = xc.reshape(BATCH, SEQ, D_MODEL)
    y_sample = xl.reshape(DEC_BATCH, DEC_SEQ, D_MODEL)
    cache_shape = (BATCH, DEPTH, SEQ, A_KV_HEADS, HEAD_DIM)
    return (y_prompt, y_sample,
            jnp.stack(new_k, axis=1).reshape(cache_shape), jnp.stack(new_v, axis=1).reshape(cache_shape),
            _diag_blocks(jnp.stack(new_gla, axis=1)).astype(out_dtype),
            _ungroup_state(jnp.stack(new_delta, axis=1)).astype(out_dtype))
```
